```python
import math
import jax
import jax.numpy as jnp
from jax import lax
import numpy as np

D_MODEL = 1024
BATCH = 8
SEQ = 8192
DEPTH = 4

GRID_W = 64
CTX_LEN = 256
N_MIXERS = 2

N_HEADS = 16
N_KV_HEADS = 4
HEAD_DIM = 64
GROUP = N_HEADS // N_KV_HEADS
Q_DIM = N_HEADS * HEAD_DIM
KV_DIM = N_KV_HEADS * HEAD_DIM
QKV_DIM = Q_DIM + 2 * KV_DIM
WINDOW = 128
Q_BLOCK = 128
ROPE_THETA = 10000.0

HY_ORDER = 2
HY_SHORT = 3
HY_FILTER_ORDER = 64
HY_BANDS = 16
HY_EMB = 1 + 2 * HY_BANDS
HY_TARGET = 1e-2
HY_FAST_DECAY = 0.3
HY_SLOW_DECAY = 1.5

N_EXPERTS = 32
TOP_K = 4
D_EXPERT = D_MODEL
SWIGLU_LIMIT = 7.0
SWIGLU_ALPHA = 1.702
MOE_BLOCK = 256

NORM_EPS = 1e-6

kernel_name = "hybrid_swa_hyena_moe_prefix_dit"


def rmsnorm(x, g):
    xf = x.astype(jnp.float32)
    y = xf * lax.rsqrt(jnp.mean(xf * xf, axis=-1, keepdims=True) + NORM_EPS)
    return (y * g.astype(jnp.float32)).astype(x.dtype)


def modulate(x, g, shift, scale):
    return rmsnorm(x, g) * (1 + scale) + shift


def axial_rope(x):
    n = x.shape[1]
    rows = n // GRID_W
    row = jnp.repeat(jnp.arange(rows), GRID_W).astype(jnp.float32)
    col = jnp.tile(jnp.arange(GRID_W), rows).astype(jnp.float32)
    half = HEAD_DIM // 2
    n_freq = half // 2
    inv = ROPE_THETA ** (-jnp.arange(n_freq, dtype=jnp.float32) / n_freq)
    ang = jnp.concatenate([row[:, None] * inv, col[:, None] * inv], axis=-1)
    cos = jnp.cos(ang)[None, :, None, :]
    sin = jnp.sin(ang)[None, :, None, :]
    xf = x.astype(jnp.float32)
    x1, x2 = xf[..., :half], xf[..., half:]
    return jnp.concatenate([x1 * cos - x2 * sin, x1 * sin + x2 * cos], axis=-1).astype(x.dtype)


def sink_softmax(s, sinks_f):
    sink = jnp.broadcast_to(sinks_f[None, :, :, None, None], s.shape[:-1] + (1,))
    return jax.nn.softmax(jnp.concatenate([s, sink], axis=-1), axis=-1)[..., :-1]


def attention_mixer(h, hc, w_qkv, b_qkv, w_o, b_o, sinks, ctx_out):
    B, N, _ = h.shape
    C = hc.shape[1]
    scale = HEAD_DIM ** -0.5
    sinks_f = sinks.astype(jnp.float32).reshape(N_KV_HEADS, GROUP)

    qkv = h @ w_qkv + b_qkv
    q = axial_rope(qkv[..., :Q_DIM].reshape(B, N, N_HEADS, HEAD_DIM))
    k = axial_rope(qkv[..., Q_DIM:Q_DIM + KV_DIM].reshape(B, N, N_KV_HEADS, HEAD_DIM))
    v = qkv[..., Q_DIM + KV_DIM:].reshape(B, N, N_KV_HEADS, HEAD_DIM)

    kv_c = hc @ w_qkv[:, Q_DIM:] + b_qkv[Q_DIM:]
    kc = kv_c[..., :KV_DIM].reshape(B, C, N_KV_HEADS, HEAD_DIM)
    vc = kv_c[..., KV_DIM:].reshape(B, C, N_KV_HEADS, HEAD_DIM)

    nb = N // Q_BLOCK
    qb = q.reshape(B, nb, Q_BLOCK, N_KV_HEADS, GROUP, HEAD_DIM)
    pad = ((0, 0), (Q_BLOCK, Q_BLOCK), (0, 0), (0, 0))
    kp = jnp.pad(k, pad)
    vp = jnp.pad(v, pad)

    def block(bi):
        qi = lax.dynamic_index_in_dim(qb, bi, axis=1, keepdims=False)
        kw = lax.dynamic_slice_in_dim(kp, bi * Q_BLOCK, 3 * Q_BLOCK, axis=1)
        vw = lax.dynamic_slice_in_dim(vp, bi * Q_BLOCK, 3 * Q_BLOCK, axis=1)
        qpos = bi * Q_BLOCK + jnp.arange(Q_BLOCK)
        kpos = (bi - 1) * Q_BLOCK + jnp.arange(3 * Q_BLOCK)
        valid = (jnp.abs(qpos[:, None] - kpos[None, :]) <= WINDOW) & (kpos >= 0)[None, :] & (kpos < N)[None, :]
        s_loc = jnp.einsum("bqkgd,bskd->bkgqs", qi, kw).astype(jnp.float32) * scale
        s_loc = jnp.where(valid, s_loc, -jnp.inf)
        s_ctx = jnp.einsum("bqkgd,bckd->bkgqc", qi, kc).astype(jnp.float32) * scale
        p = sink_softmax(jnp.concatenate([s_loc, s_ctx], axis=-1), sinks_f).astype(v.dtype)
        return (jnp.einsum("bkgqs,bskd->bqkgd", p[..., :3 * Q_BLOCK], vw)
                + jnp.einsum("bkgqc,bckd->bqkgd", p[..., 3 * Q_BLOCK:], vc))

    o = lax.map(block, jnp.arange(nb))
    o = jnp.moveaxis(o, 0, 1).reshape(B, N, Q_DIM)
    y = o @ w_o + b_o

    yc = None
    if ctx_out:
        qc = (hc @ w_qkv[:, :Q_DIM] + b_qkv[:Q_DIM]).reshape(B, C, N_KV_HEADS, GROUP, HEAD_DIM)
        s = jnp.einsum("bqkgd,bckd->bkgqc", qc, kc).astype(jnp.float32) * scale
        p = sink_softmax(s, sinks_f).astype(vc.dtype)
        oc = jnp.einsum("bkgqc,bckd->bqkgd", p, vc).reshape(B, C, Q_DIM)
        yc = oc @ w_o + b_o
    return y, yc


def short_conv(u, w, b):
    L = u.shape[1]
    r = HY_SHORT // 2
    up = jnp.pad(u, ((0, 0), (r, r), (0, 0)))
    out = b
    for s in range(HY_SHORT):
        out = out + up[:, s:s + L] * w[s]
    return out


def hyena_filter_spectrum(L, w1, b1, w2, b2, w3, b3, freq, w_out):
    f32 = jnp.float32
    t = jnp.linspace(0.0, 1.0, L, dtype=f32)[:, None]
    w = 2.0 * math.pi * jnp.arange(L, dtype=f32)[:, None] / L
    bands = jnp.linspace(1e-4, HY_BANDS - 1, HY_BANDS, dtype=f32)[None, :]
    z = jnp.concatenate([t, jnp.cos(bands * w), -jnp.sin(bands * w)], axis=-1)
    fr = freq.astype(f32)
    a = jnp.sin(fr * (z @ w1.astype(f32) + b1.astype(f32)))
    a = jnp.sin(fr * (a @ w2.astype(f32) + b2.astype(f32)))
    a = jnp.sin(fr * (a @ w3.astype(f32) + b3.astype(f32)))
    hf = (a @ w_out.astype(f32)).reshape(L, 2, HY_ORDER, D_MODEL)
    deltas = jnp.linspace(math.log(HY_TARGET) / HY_SLOW_DECAY, math.log(HY_TARGET) / HY_FAST_DECAY,
                          D_MODEL, dtype=f32)
    hf = hf * jnp.exp(-t * jnp.abs(deltas))[:, None, None, :]
    fwd, bwd = hf[:, 0], hf[:, 1]
    g = jnp.concatenate([fwd, jnp.zeros((1, HY_ORDER, D_MODEL), f32), bwd[:0:-1]], axis=0)
    g = g / jnp.sum(jnp.abs(g), axis=0, keepdims=True)
    return jnp.fft.rfft(g, axis=0)


def hyena_mixer(h, w_in, b_in, conv_w, conv_b, bias_d, w_o, b_o, filt):
    L = h.shape[1]
    spec = hyena_filter_spectrum(L, *filt)
    u = short_conv(h @ w_in + b_in, conv_w, conv_b)
    x1, x2, v = jnp.split(u, 3, axis=-1)
    z = v
    for o, gate in enumerate((x1, x2)):
        zf = z.astype(jnp.float32)
        conv = jnp.fft.irfft(jnp.fft.rfft(zf, n=2 * L, axis=1) * spec[None, :, o], n=2 * L, axis=1)[:, :L]
        z = gate * (conv + zf * bias_d[o].astype(jnp.float32)).astype(h.dtype)
    return z @ w_o + b_o


def moe_ffn(h, router_w, router_b, w_gu, b_gu, w_down, b_down):
    T, D = h.shape
    logits = (h @ router_w + router_b).astype(jnp.float32)
    top_v, top_i = lax.top_k(logits, TOP_K)
    gates = jax.nn.softmax(top_v, axis=-1)
    n_assign = T * TOP_K
    flat_e = top_i.reshape(-1)
    flat_tok = jnp.repeat(jnp.arange(T, dtype=jnp.int32), TOP_K)
    flat_g = gates.reshape(-1)
    order = jnp.argsort(flat_e)
    sorted_e = flat_e[order]
    counts = jnp.bincount(flat_e, length=N_EXPERTS)
    padded = (counts + MOE_BLOCK - 1) // MOE_BLOCK * MOE_BLOCK
    pad_end = jnp.cumsum(padded)
    pad_start = pad_end - padded
    start = jnp.cumsum(counts) - counts
    dest = pad_start[sorted_e] + jnp.arange(n_assign) - start[sorted_e]
    n_blocks = -(-n_assign // MOE_BLOCK) + N_EXPERTS
    n_rows = n_blocks * MOE_BLOCK
    row_tok = jnp.full((n_rows,), T, jnp.int32).at[dest].set(flat_tok[order])
    row_gate = jnp.zeros((n_rows,), jnp.float32).at[dest].set(flat_g[order])
    block_e = jnp.minimum(jnp.searchsorted(pad_end, jnp.arange(n_blocks) * MOE_BLOCK, side="right"),
                          N_EXPERTS - 1)
    h_pad = jnp.concatenate([h, jnp.zeros((1, D), h.dtype)], axis=0)
    xs = h_pad[row_tok].reshape(n_blocks, MOE_BLOCK, D)

    def expert_block(args):
        xb, e = args
        gu = xb @ w_gu[e] + b_gu[e]
        g = jnp.minimum(gu[..., :D_EXPERT], SWIGLU_LIMIT)
        up = jnp.clip(gu[..., D_EXPERT:], -SWIGLU_LIMIT, SWIGLU_LIMIT)
        act = g * jax.nn.sigmoid(SWIGLU_ALPHA * g) * (up + 1)
        return act @ w_down[e] + b_down[e]

    ys = lax.map(expert_block, (xs, block_e))
    out = jnp.zeros((T + 1, D), jnp.float32).at[row_tok].add(
        ys.reshape(n_rows, D).astype(jnp.float32) * row_gate[:, None])
    return out[:T].astype(h.dtype)


def setup_inputs(seed: int = 0) -> dict:
    key = jax.random.key(seed)
    ks = iter(jax.random.split(key, 48))
    f32 = jnp.float32
    D = D_MODEL
    n_attn = (DEPTH + N_MIXERS - 1) // N_MIXERS
    n_hy = DEPTH // N_MIXERS

    def nrm(shape, s):
        return jax.random.normal(next(ks), shape, f32) * s

    def gain(shape):
        return 1.0 + nrm(shape, 0.05)

    return {
        "x": nrm((BATCH, SEQ, D), 1.0),
        "c": nrm((BATCH, D), 1.0),
        "ctx": nrm((BATCH, CTX_LEN, D), 1.0),
        "c_ctx": nrm((D,), 1.0),
        "ada_w": nrm((DEPTH, D, 6 * D), 0.5 * D ** -0.5),
        "ada_b": nrm((DEPTH, 6 * D), 0.02),
        "norm_mix": gain((DEPTH, D)),
        "norm_ffn": gain((DEPTH, D)),
        "attn_w_qkv": nrm((n_attn, D, QKV_DIM), D ** -0.5),
        "attn_b_qkv": nrm((n_attn, QKV_DIM), 0.02),
        "attn_w_o": nrm((n_attn, Q_DIM, D), Q_DIM ** -0.5),
        "attn_b_o": nrm((n_attn, D), 0.02),
        "attn_sinks": nrm((n_attn, N_HEADS), 0.5),
        "hy_w_in": nrm((n_hy, D, 3 * D), D ** -0.5),
        "hy_b_in": nrm((n_hy, 3 * D), 0.02),
        "hy_conv_w": nrm((n_hy, HY_SHORT, 3 * D), HY_SHORT ** -0.5),
        "hy_conv_b": nrm((n_hy, 3 * D), 0.02),
        "hy_f_w1": nrm((n_hy, HY_EMB, HY_FILTER_ORDER), HY_EMB ** -0.5),
        "hy_f_b1": nrm((n_hy, HY_FILTER_ORDER), 0.1),
        "hy_f_w2": nrm((n_hy, HY_FILTER_ORDER, HY_FILTER_ORDER), HY_FILTER_ORDER ** -0.5),
        "hy_f_b2": nrm((n_hy, HY_FILTER_ORDER), 0.1),
        "hy_f_w3": nrm((n_hy, HY_FILTER_ORDER, HY_FILTER_ORDER), HY_FILTER_ORDER ** -0.5),
        "hy_f_b3": nrm((n_hy, HY_FILTER_ORDER), 0.1),
        "hy_f_freq": gain((n_hy, HY_FILTER_ORDER)),
        "hy_f_wout": nrm((n_hy, HY_FILTER_ORDER, 2 * HY_ORDER * D), HY_FILTER_ORDER ** -0.5),
        "hy_bias_d": nrm((n_hy, HY_ORDER, D), 0.5),
        "hy_w_o": nrm((n_hy, D, D), D ** -0.5),
        "hy_b_o": nrm((n_hy, D), 0.02),
        "moe_router_w": nrm((DEPTH, D, N_EXPERTS), D ** -0.5),
        "moe_router_b": nrm((DEPTH, N_EXPERTS), 0.01),
        "moe_w_gu": nrm((DEPTH, N_EXPERTS, D, 2 * D_EXPERT), D ** -0.5),
        "moe_b_gu": nrm((DEPTH, N_EXPERTS, 2 * D_EXPERT), 0.02),
        "moe_w_down": nrm((DEPTH, N_EXPERTS, D_EXPERT, D), D_EXPERT ** -0.5),
        "moe_b_down": nrm((DEPTH, N_EXPERTS, D), 0.02),
        "final_norm": gain((D,)),
    }


def reference(x, c, ctx, c_ctx, ada_w, ada_b, norm_mix, norm_ffn,
              attn_w_qkv, attn_b_qkv, attn_w_o, attn_b_o, attn_sinks,
              hy_w_in, hy_b_in, hy_conv_w, hy_conv_b, hy_f_w1, hy_f_b1, hy_f_w2, hy_f_b2,
              hy_f_w3, hy_f_b3, hy_f_freq, hy_f_wout, hy_bias_d, hy_w_o, hy_b_o,
              moe_router_w, moe_router_b, moe_w_gu, moe_b_gu, moe_w_down, moe_b_down,
              final_norm):
    B, N, D = x.shape
    s_c = jax.nn.silu(c)
    s_cc = jax.nn.silu(c_ctx)
    for i in range(DEPTH):
        kind = i % N_MIXERS
        j = i // N_MIXERS
        update_ctx = any(l % N_MIXERS == 0 for l in range(i + 1, DEPTH))
        need_ctx = update_ctx or kind == 0
        sh1, sc1, g1, sh2, sc2, g2 = jnp.split((s_c @ ada_w[i] + ada_b[i])[:, None, :], 6, axis=-1)
        h = modulate(x, norm_mix[i], sh1, sc1)
        hc = None
        if need_ctx:
            csh1, csc1, cg1, csh2, csc2, cg2 = jnp.split(s_cc @ ada_w[i] + ada_b[i], 6, axis=-1)
            hc = modulate(ctx, norm_mix[i], csh1, csc1)
        if kind == 0:
            y, yc = attention_mixer(h, hc, attn_w_qkv[j], attn_b_qkv[j], attn_w_o[j], attn_b_o[j],
                                    attn_sinks[j], update_ctx)
        else:
            filt = (hy_f_w1[j], hy_f_b1[j], hy_f_w2[j], hy_f_b2[j], hy_f_w3[j], hy_f_b3[j],
                    hy_f_freq[j], hy_f_wout[j])
            y = hyena_mixer(h, hy_w_in[j], hy_b_in[j], hy_conv_w[j], hy_conv_b[j], hy_bias_d[j],
                            hy_w_o[j], hy_b_o[j], filt)
            yc = None
            if update_ctx:
                yc = hyena_mixer(hc, hy_w_in[j], hy_b_in[j], hy_conv_w[j], hy_conv_b[j], hy_bias_d[j],
                                 hy_w_o[j], hy_b_o[j], filt)
        x = x + g1 * y
        if update_ctx:
            ctx = ctx + cg1 * yc
            tok = jnp.concatenate([modulate(x, norm_ffn[i], sh2, sc2).reshape(-1, D),
                                   modulate(ctx, norm_ffn[i], csh2, csc2).reshape(-1, D)], axis=0)
            f = moe_ffn(tok, moe_router_w[i], moe_router_b[i], moe_w_gu[i], moe_b_gu[i],
                        moe_w_down[i], moe_b_down[i])
            x = x + g2 * f[:B * N].reshape(x.shape)
            ctx = ctx + cg2 * f[B * N:].reshape(ctx.shape)
        else:
            f = moe_ffn(modulate(x, norm_ffn[i], sh2, sc2).reshape(-1, D), moe_router_w[i], moe_router_b[i],
                        moe_w_gu[i], moe_b_gu[i], moe_w_down[i], moe_b_down[i])
            x = x + g2 * f.reshape(x.shape)
    return rmsnorm(x, final_norm)
```

```python
import functools
import math

import jax
import jax.numpy as jnp
from jax import lax
from jax.experimental import pallas as pl
from jax.experimental.pallas import tpu as pltpu

DEPTH = 4
N_MIXERS = 2
GRID_W = 64
N_HEADS = 16
N_KV_HEADS = 4
HEAD_DIM = 64
GROUP = N_HEADS // N_KV_HEADS
WINDOW = 128
ROPE_THETA = 10000.0
HY_ORDER = 2
HY_SHORT = 3
HY_BANDS = 16
HY_TARGET = 1e-2
HY_FAST_DECAY = 0.3
HY_SLOW_DECAY = 1.5
N_EXPERTS = 32
TOP_K = 4
SWIGLU_LIMIT = 7.0
SWIGLU_ALPHA = 1.702
NORM_EPS = 1e-6

LANES = 128
TOKEN_TILE = 512
Q_TILE = 128
MOE_TILE = 512
VMEM_LIMIT = 56 * 1024 * 1024

F32 = jnp.float32
BF16 = jnp.bfloat16


def _dot(a, b):
    return jnp.dot(a, b, preferred_element_type=F32)


def _params(*sem):
    return pltpu.CompilerParams(dimension_semantics=sem, vmem_limit_bytes=VMEM_LIMIT)


def _ada_kernel(c_ref, w_ref, b_ref, o_ref):
    c = c_ref[...]
    s = c * (1.0 / (1.0 + jnp.exp(-c)))
    o_ref[...] = jnp.dot(s, w_ref[...], preferred_element_type=F32,
                         precision=lax.Precision.HIGHEST) + b_ref[...]


def _ada_table(c_all, ada_w, ada_b):
    r, d = c_all.shape
    depth, _, n6 = ada_w.shape
    tn = n6 // 4
    return pl.pallas_call(
        _ada_kernel,
        out_shape=jax.ShapeDtypeStruct((depth, r, n6), F32),
        grid=(depth, n6 // tn),
        in_specs=[pl.BlockSpec((r, d), lambda l, j: (0, 0)),
                  pl.BlockSpec((None, d, tn), lambda l, j: (l, 0, j)),
                  pl.BlockSpec((None, 1, tn), lambda l, j: (l, 0, j))],
        out_specs=pl.BlockSpec((None, r, tn), lambda l, j: (l, 0, j)),
        compiler_params=_params("arbitrary", "arbitrary"),
        name="ada_table",
    )(c_all, ada_w, ada_b.reshape(depth, 1, n6))


def _modulated(x, g, mod_ref, row):
    y = x * lax.rsqrt(jnp.mean(x * x, axis=-1, keepdims=True) + NORM_EPS) * g
    return y * (1.0 + mod_ref[row + 1:row + 2, :]) + mod_ref[row:row + 1, :]


def _mod_index(tm, n_seq, n_batch):
    return lambda i: (jnp.minimum(i * tm // n_seq, n_batch), 0, 0)


def _nm_kernel(x_ref, g_ref, mod_ref, w_ref, b_ref, cos_ref, sin_ref, o_ref, *, tn, rope_cols, q_cols):
    h = _modulated(x_ref[...], g_ref[...], mod_ref, 0).astype(BF16)
    n_out = o_ref.shape[1]
    if rope_cols:
        reps = tn // LANES
        cos = jnp.tile(cos_ref[...], (1, reps))
        sin = jnp.tile(sin_ref[...], (1, reps))
        lane = lax.broadcasted_iota(jnp.int32, (1, tn), 1)
        first_half = (lane & (HEAD_DIM - 1)) < (HEAD_DIM // 2)
    for n0 in range(0, n_out, tn):
        acc = _dot(h, w_ref[:, n0:n0 + tn]) + b_ref[:, n0:n0 + tn]
        if n0 < rope_cols:
            partner = jnp.where(first_half, pltpu.roll(acc, tn - HEAD_DIM // 2, 1),
                                pltpu.roll(acc, HEAD_DIM // 2, 1))
            acc = acc * cos + partner * sin
            if n0 < q_cols:
                acc = acc * (HEAD_DIM ** -0.5)
        o_ref[:, n0:n0 + tn] = acc.astype(o_ref.dtype)


def _norm_matmul(xs, gain, mod, w, bias, *, n_rows, n_seq, n_batch, out_dtype, rope=None):
    d = xs.shape[1]
    n_out = w.shape[1]
    tm, tn = TOKEN_TILE, 512
    if rope is None:
        cos = jnp.zeros((tm, LANES), F32)
        sin = cos
        rope_cols = q_cols = 0
        rope_index = lambda i: (0, 0)
    else:
        cos, sin, rope_cols, q_cols = rope
        seq_tiles = n_seq // tm
        lat_tiles = n_seq * n_batch // tm
        rope_index = lambda i: (jnp.where(i < lat_tiles, i % seq_tiles, seq_tiles), 0)
    kern = functools.partial(_nm_kernel, tn=tn, rope_cols=rope_cols, q_cols=q_cols)
    return pl.pallas_call(
        kern,
        out_shape=jax.ShapeDtypeStruct((n_rows, n_out), out_dtype),
        grid=(n_rows // tm,),
        in_specs=[pl.BlockSpec((tm, d), lambda i: (i, 0)),
                  pl.BlockSpec((1, d), lambda i: (0, 0)),
                  pl.BlockSpec((None, 2, d), _mod_index(tm, n_seq, n_batch)),
                  pl.BlockSpec((d, n_out), lambda i: (0, 0)),
                  pl.BlockSpec((1, n_out), lambda i: (0, 0)),
                  pl.BlockSpec((tm, LANES), rope_index),
                  pl.BlockSpec((tm, LANES), rope_index)],
        out_specs=pl.BlockSpec((tm, n_out), lambda i: (i, 0)),
        compiler_params=_params("arbitrary"),
        name="norm_matmul",
    )(xs, gain.reshape(1, d), mod, w, bias.reshape(1, n_out), cos, sin)


def _res_kernel(a_ref, w_ref, b_ref, x_ref, gate_ref, o_ref):
    y = _dot(a_ref[...].astype(BF16), w_ref[...]) + b_ref[...]
    o_ref[...] = x_ref[...] + gate_ref[...] * y


def _matmul_residual(a, w, bias, xs, gate, *, n_rows, n_seq, n_batch):
    k = a.shape[1]
    d = w.shape[1]
    tm = TOKEN_TILE
    return pl.pallas_call(
        _res_kernel,
        out_shape=jax.ShapeDtypeStruct((n_rows, d), F32),
        grid=(n_rows // tm,),
        in_specs=[pl.BlockSpec((tm, k), lambda i: (i, 0)),
                  pl.BlockSpec((k, d), lambda i: (0, 0)),
                  pl.BlockSpec((1, d), lambda i: (0, 0)),
                  pl.BlockSpec((tm, d), lambda i: (i, 0)),
                  pl.BlockSpec((None, 1, d), _mod_index(tm, n_seq, n_batch))],
        out_specs=pl.BlockSpec((tm, d), lambda i: (i, 0)),
        compiler_params=_params("arbitrary"),
        name="matmul_residual",
    )(a, w, bias.reshape(1, d), xs, gate)


def _attend_pairs(q_ref, k_all, v_all, sink_ref, o_ref, kvh, valid):
    tq = q_ref.shape[0]
    lane = lax.broadcasted_iota(jnp.int32, (tq, LANES), 1)
    low = lane < HEAD_DIM
    row = lax.broadcasted_iota(jnp.int32, (2 * tq, 1), 0)
    for j in range(GROUP // 2):
        c0 = kvh * GROUP * HEAD_DIM + j * LANES
        qp = q_ref[:, c0:c0 + LANES]
        zero = jnp.zeros_like(qp)
        q2 = jnp.concatenate([jnp.where(low, qp, zero), jnp.where(low, zero, qp)], axis=0)
        s = lax.dot_general(q2, k_all, (((1,), (1,)), ((), ())), preferred_element_type=F32)
        if valid is not None:
            n_loc = valid.shape[1]
            s = jnp.concatenate([jnp.where(valid, s[:, :n_loc], -jnp.inf), s[:, n_loc:]], axis=1)
        head = kvh * GROUP + 2 * j
        sink = jnp.where(row < tq, sink_ref[head], sink_ref[head + 1])
        m = jnp.maximum(jnp.max(s, axis=-1, keepdims=True), sink)
        p = jnp.exp(s - m)
        denom = jnp.sum(p, axis=-1, keepdims=True) + jnp.exp(sink - m)
        o2 = _dot(p.astype(BF16), v_all) / denom
        o_ref[:, c0:c0 + LANES] = jnp.where(low, o2[:tq], o2[tq:]).astype(o_ref.dtype)


def _attn_kernel(sink_ref, q_ref, kp_ref, kc_ref, kn_ref, vp_ref, vc_ref, vn_ref, kx_ref, vx_ref, o_ref,
                 *, n_q_blocks):
    qi = pl.program_id(1)
    tq = q_ref.shape[0]

    @pl.when(qi < n_q_blocks)
    def _():
        r = lax.broadcasted_iota(jnp.int32, (tq, 3 * tq), 0)
        c = lax.broadcasted_iota(jnp.int32, (tq, 3 * tq), 1)
        kpos = c - tq + qi * tq
        band = (jnp.abs(c - tq - r) <= WINDOW) & (kpos >= 0) & (kpos < n_q_blocks * tq)
        valid = jnp.concatenate([band, band], axis=0)
        for kvh in range(N_KV_HEADS):
            cs = slice(kvh * LANES, (kvh + 1) * LANES)
            k_all = jnp.concatenate([kp_ref[:, cs], kc_ref[:, cs], kn_ref[:, cs], kx_ref[:, cs]], axis=0)
            v_all = jnp.concatenate([vp_ref[:, cs], vc_ref[:, cs], vn_ref[:, cs], vx_ref[:, cs]], axis=0)
            _attend_pairs(q_ref, k_all, v_all, sink_ref, o_ref, kvh, valid)

    @pl.when(qi >= n_q_blocks)
    def _():
        for kvh in range(N_KV_HEADS):
            cs = slice(kvh * LANES, (kvh + 1) * LANES)
            _attend_pairs(q_ref, kx_ref[:, cs], vx_ref[:, cs], sink_ref, o_ref, kvh, None)


def _attention(qkv, sinks, *, n_seq, n_batch, n_ctx, ctx_out):
    tq = Q_TILE
    nq = n_seq // tq
    ncq = n_ctx // tq if ctx_out else 0
    lat_blocks = n_batch * nq
    q_dim = N_HEADS * HEAD_DIM
    kv_w = N_KV_HEADS * LANES
    k_col = q_dim // kv_w
    v_col = k_col + 1
    ctx_row0 = n_batch * n_seq // n_ctx

    def q_index(b, i, s):
        return (jnp.where(i < nq, b * nq + i, lat_blocks + b * (n_ctx // tq) + (i - nq)), 0)

    def kv_index(off, col):
        def index(b, i, s):
            return (b * nq + jnp.clip(i + off, 0, nq - 1), col)
        return index

    n_rows = n_batch * n_seq + (n_batch * n_ctx if ctx_out else 0)
    grid_spec = pltpu.PrefetchScalarGridSpec(
        num_scalar_prefetch=1,
        grid=(n_batch, nq + ncq),
        in_specs=[pl.BlockSpec((tq, q_dim), q_index),
                  pl.BlockSpec((tq, kv_w), kv_index(-1, k_col)),
                  pl.BlockSpec((tq, kv_w), kv_index(0, k_col)),
                  pl.BlockSpec((tq, kv_w), kv_index(1, k_col)),
                  pl.BlockSpec((tq, kv_w), kv_index(-1, v_col)),
                  pl.BlockSpec((tq, kv_w), kv_index(0, v_col)),
                  pl.BlockSpec((tq, kv_w), kv_index(1, v_col)),
                  pl.BlockSpec((n_ctx, kv_w), lambda b, i, s: (ctx_row0 + b, k_col)),
                  pl.BlockSpec((n_ctx, kv_w), lambda b, i, s: (ctx_row0 + b, v_col))],
        out_specs=pl.BlockSpec((tq, q_dim), q_index),
    )
    return pl.pallas_call(
        functools.partial(_attn_kernel, n_q_blocks=nq),
        out_shape=jax.ShapeDtypeStruct((n_rows, q_dim), BF16),
        grid_spec=grid_spec,
        compiler_params=_params("arbitrary", "arbitrary"),
        name="attention",
    )(sinks, qkv, qkv, qkv, qkv, qkv, qkv, qkv, qkv, qkv)


def _router_kernel(x_ref, g_ref, mod_ref, w_ref, b_ref, tok_ref, logit_ref):
    h = _modulated(x_ref[...], g_ref[...], mod_ref, 0)
    tok_ref[...] = h
    logit_ref[...] = jnp.dot(h, w_ref[...], preferred_element_type=F32,
                             precision=lax.Precision.HIGHEST) + b_ref[...]


def _router(xs, gain, mod, w, bias, *, n_rows, n_seq, n_batch):
    d = xs.shape[1]
    tm = TOKEN_TILE
    return pl.pallas_call(
        _router_kernel,
        out_shape=(jax.ShapeDtypeStruct((n_rows, d), F32), jax.ShapeDtypeStruct((n_rows, LANES), F32)),
        grid=(n_rows // tm,),
        in_specs=[pl.BlockSpec((tm, d), lambda i: (i, 0)),
                  pl.BlockSpec((1, d), lambda i: (0, 0)),
                  pl.BlockSpec((None, 2, d), _mod_index(tm, n_seq, n_batch)),
                  pl.BlockSpec((d, LANES), lambda i: (0, 0)),
                  pl.BlockSpec((1, LANES), lambda i: (0, 0))],
        out_specs=(pl.BlockSpec((tm, d), lambda i: (i, 0)), pl.BlockSpec((tm, LANES), lambda i: (i, 0))),
        compiler_params=_params("arbitrary"),
        name="router",
    )(xs, gain.reshape(1, d), mod, w, bias)


def _expert_kernel(be_ref, nb_ref, x_ref, gate_ref, wgu_ref, bgu_ref, wd_ref, bd_ref, o_ref, wgu_bf, wd_bf):
    i = pl.program_id(0)
    fresh = jnp.logical_or(i == 0, be_ref[i] != be_ref[jnp.maximum(i - 1, 0)])

    @pl.when(jnp.logical_and(fresh, i < nb_ref[0]))
    def _():
        wgu_bf[...] = wgu_ref[...].astype(BF16)
        wd_bf[...] = wd_ref[...].astype(BF16)

    @pl.when(i < nb_ref[0])
    def _():
        d_e = wd_ref.shape[0]
        x = x_ref[...].astype(BF16)
        g = jnp.minimum(_dot(x, wgu_bf[:, :d_e]) + bgu_ref[:, :d_e], SWIGLU_LIMIT)
        up = jnp.clip(_dot(x, wgu_bf[:, d_e:]) + bgu_ref[:, d_e:], -SWIGLU_LIMIT, SWIGLU_LIMIT)
        act = g * (1.0 / (1.0 + jnp.exp(-SWIGLU_ALPHA * g))) * (up + 1.0)
        y = _dot(act.astype(BF16), wd_bf[...]) + bd_ref[...]
        o_ref[...] = y * gate_ref[...]

    @pl.when(i >= nb_ref[0])
    def _():
        o_ref[...] = jnp.zeros_like(o_ref)


def _experts(xs, row_gate, block_e, n_used, w_gu, b_gu, w_down, b_down, layer):
    r, d = xs.shape
    n_e, _, d_gu = w_gu.shape[1:]
    tm = MOE_TILE
    grid_spec = pltpu.PrefetchScalarGridSpec(
        num_scalar_prefetch=2,
        grid=(r // tm,),
        in_specs=[pl.BlockSpec((tm, d), lambda i, be, nb: (i, 0)),
                  pl.BlockSpec((tm, 1), lambda i, be, nb: (i, 0)),
                  pl.BlockSpec((None, None, d, d_gu), lambda i, be, nb: (layer, be[i], 0, 0)),
                  pl.BlockSpec((None, None, 1, d_gu), lambda i, be, nb: (layer, be[i], 0, 0)),
                  pl.BlockSpec((None, None, d_gu // 2, d), lambda i, be, nb: (layer, be[i], 0, 0)),
                  pl.BlockSpec((None, None, 1, d), lambda i, be, nb: (layer, be[i], 0, 0))],
        out_specs=pl.BlockSpec((tm, d), lambda i, be, nb: (i, 0)),
        scratch_shapes=[pltpu.VMEM((d, d_gu), BF16), pltpu.VMEM((d_gu // 2, d), BF16)],
    )
    return pl.pallas_call(
        _expert_kernel,
        out_shape=jax.ShapeDtypeStruct((r, d), F32),
        grid_spec=grid_spec,
        compiler_params=_params("arbitrary"),
        name="experts",
    )(block_e, n_used, xs, row_gate, w_gu, b_gu.reshape(b_gu.shape[0], n_e, 1, d_gu), w_down,
      b_down.reshape(b_down.shape[0], n_e, 1, d))


def _combine_kernel(x_ref, y_ref, gate_ref, g_ref, o_ref, *, final):
    f = y_ref[:, 0, :]
    for k in range(1, TOP_K):
        f = f + y_ref[:, k, :]
    x = x_ref[...] + gate_ref[...] * f
    if final:
        x = x * lax.rsqrt(jnp.mean(x * x, axis=-1, keepdims=True) + NORM_EPS) * g_ref[...]
    o_ref[...] = x


def _combine(xs, y4, gate, final_gain, *, n_rows, n_seq, n_batch, final):
    d = xs.shape[1]
    tm = TOKEN_TILE // 2
    return pl.pallas_call(
        functools.partial(_combine_kernel, final=final),
        out_shape=jax.ShapeDtypeStruct((n_rows, d), F32),
        grid=(n_rows // tm,),
        in_specs=[pl.BlockSpec((tm, d), lambda i: (i, 0)),
                  pl.BlockSpec((tm, TOP_K, d), lambda i: (i, 0, 0)),
                  pl.BlockSpec((None, 1, d), _mod_index(tm, n_seq, n_batch)),
                  pl.BlockSpec((1, d), lambda i: (0, 0))],
        out_specs=pl.BlockSpec((tm, d), lambda i: (i, 0)),
        compiler_params=_params("arbitrary"),
        name="combine",
    )(xs, y4, gate, final_gain.reshape(1, d))


def _moe_layer(xs, layer, mod, norm_g, router_w, router_b, w_gu, b_gu, w_down, b_down, final_gain,
               *, n_rows, n_seq, n_batch, final):
    d = xs.shape[1]
    rw = jnp.zeros((d, LANES), F32).at[:, :N_EXPERTS].set(router_w)
    rb = jnp.zeros((1, LANES), F32).at[0, :N_EXPERTS].set(router_b)
    tok, logits = _router(xs, norm_g, mod[:, 3:5], rw, rb, n_rows=n_rows, n_seq=n_seq, n_batch=n_batch)
    top_v, top_i = lax.top_k(logits[:, :N_EXPERTS], TOP_K)
    gates = jax.nn.softmax(top_v, axis=-1)
    n_assign = n_rows * TOP_K
    flat_e = top_i.reshape(-1)
    order = jnp.argsort(flat_e)
    sorted_e = flat_e[order]
    counts = jnp.bincount(flat_e, length=N_EXPERTS)
    padded = (counts + MOE_TILE - 1) // MOE_TILE * MOE_TILE
    pad_end = jnp.cumsum(padded)
    pad_start = pad_end - padded
    start = jnp.cumsum(counts) - counts
    dest = (pad_start[sorted_e] + jnp.arange(n_assign) - start[sorted_e]).astype(jnp.int32)
    n_blocks = -(-n_assign // MOE_TILE) + N_EXPERTS
    n_slots = n_blocks * MOE_TILE
    row_tok = jnp.full((n_slots,), n_rows, jnp.int32).at[dest].set((order // TOP_K).astype(jnp.int32))
    row_gate = jnp.zeros((n_slots,), F32).at[dest].set(gates.reshape(-1)[order])
    slot_of = jnp.zeros((n_assign,), jnp.int32).at[order].set(dest)
    block_e = jnp.minimum(jnp.searchsorted(pad_end, jnp.arange(n_blocks) * MOE_TILE, side="right"),
                          N_EXPERTS - 1).astype(jnp.int32)
    n_used = (pad_end[-1] // MOE_TILE).astype(jnp.int32).reshape(1)
    tok_pad = jnp.concatenate([tok, jnp.zeros((1, d), F32)], axis=0)
    rows = tok_pad[row_tok]
    ys = _experts(rows, row_gate.reshape(n_slots, 1), block_e, n_used, w_gu, b_gu, w_down, b_down, layer)
    y4 = ys[slot_of].reshape(n_rows, TOP_K, d)
    return _combine(xs, y4, mod[:, 5:6], final_gain, n_rows=n_rows, n_seq=n_seq, n_batch=n_batch, final=final)


def _hyena_spectrum(L, w1, b1, w2, b2, w3, b3, freq, w_out, d):
    t = jnp.linspace(0.0, 1.0, L, dtype=F32)[:, None]
    w = 2.0 * math.pi * jnp.arange(L, dtype=F32)[:, None] / L
    bands = jnp.linspace(1e-4, HY_BANDS - 1, HY_BANDS, dtype=F32)[None, :]
    z = jnp.concatenate([t, jnp.cos(bands * w), -jnp.sin(bands * w)], axis=-1)
    a = jnp.sin(freq * (z @ w1 + b1))
    a = jnp.sin(freq * (a @ w2 + b2))
    a = jnp.sin(freq * (a @ w3 + b3))
    hf = (a @ w_out).reshape(L, 2, HY_ORDER, d)
    deltas = jnp.linspace(math.log(HY_TARGET) / HY_SLOW_DECAY, math.log(HY_TARGET) / HY_FAST_DECAY, d, dtype=F32)
    hf = hf * jnp.exp(-t * jnp.abs(deltas))[:, None, None, :]
    fwd, bwd = hf[:, 0], hf[:, 1]
    g = jnp.concatenate([fwd, jnp.zeros((1, HY_ORDER, d), F32), bwd[:0:-1]], axis=0)
    g = g / jnp.sum(jnp.abs(g), axis=0, keepdims=True)
    return jnp.fft.rfft(g, axis=0)


def _hyena_core(u, conv_w, conv_b, bias_d, filt):
    L = u.shape[1]
    d = u.shape[2] // 3
    spec = _hyena_spectrum(L, *filt, d)
    r = HY_SHORT // 2
    up = jnp.pad(u, ((0, 0), (r, r), (0, 0)))
    out = conv_b
    for s in range(HY_SHORT):
        out = out + up[:, s:s + L] * conv_w[s]
    x1, x2, v = jnp.split(out, 3, axis=-1)
    z = v
    for o, gate in enumerate((x1, x2)):
        conv = jnp.fft.irfft(jnp.fft.rfft(z, n=2 * L, axis=1) * spec[None, :, o], n=2 * L, axis=1)[:, :L]
        z = gate * (conv + z * bias_d[o])
    return z


def _rope_tables(n_seq, tm):
    rows = n_seq // GRID_W
    row = jnp.repeat(jnp.arange(rows), GRID_W).astype(F32)
    col = jnp.tile(jnp.arange(GRID_W), rows).astype(F32)
    half = HEAD_DIM // 2
    n_freq = half // 2
    inv = ROPE_THETA ** (-jnp.arange(n_freq, dtype=F32) / n_freq)
    ang = jnp.concatenate([row[:, None] * inv, col[:, None] * inv], axis=-1)
    cos = jnp.tile(jnp.cos(ang), (1, LANES // half))
    sin = jnp.tile(jnp.concatenate([-jnp.sin(ang), jnp.sin(ang)], axis=-1), (1, LANES // HEAD_DIM))
    cos = jnp.concatenate([cos, jnp.ones((tm, LANES), F32)], axis=0)
    sin = jnp.concatenate([sin, jnp.zeros((tm, LANES), F32)], axis=0)
    return cos, sin


def _doubled(w, n_heads):
    lead = w.shape[:-1]
    w = w.reshape(lead + (n_heads, 1, HEAD_DIM))
    return jnp.broadcast_to(w, lead + (n_heads, 2, HEAD_DIM)).reshape(lead + (n_heads * 2 * HEAD_DIM,))


def kernel(x, c, ctx, c_ctx, ada_w, ada_b, norm_mix, norm_ffn, attn_w_qkv, attn_b_qkv, attn_w_o, attn_b_o, attn_sinks, hy_w_in, hy_b_in, hy_conv_w, hy_conv_b, hy_f_w1, hy_f_b1, hy_f_w2, hy_f_b2, hy_f_w3, hy_f_b3, hy_f_freq, hy_f_wout, hy_bias_d, hy_w_o, hy_b_o, moe_router_w, moe_router_b, moe_w_gu, moe_b_gu, moe_w_down, moe_b_down, final_norm):
    B, N, D = x.shape
    C = ctx.shape[1]
    T, TC = B * N, B * C
    q_dim = N_HEADS * HEAD_DIM
    kv_dim = N_KV_HEADS * HEAD_DIM
    dims = dict(n_seq=N, n_batch=B)

    c_all = jnp.concatenate([c, c_ctx[None, :], jnp.zeros((16 - B - 1, D), F32)], axis=0)
    mod_all = _ada_table(c_all, ada_w, ada_b)[:, :B + 1].reshape(DEPTH, B + 1, 6, D)
    cos, sin = _rope_tables(N, TOKEN_TILE)

    xs = jnp.concatenate([x.reshape(T, D), ctx.reshape(TC, D)], axis=0)
    for i in range(DEPTH):
        kind, j = i % N_MIXERS, i // N_MIXERS
        update_ctx = any(l % N_MIXERS == 0 for l in range(i + 1, DEPTH))
        need_ctx = update_ctx or kind == 0
        mod = mod_all[i]
        n_in = T + TC if need_ctx else T
        n_out = T + TC if update_ctx else T
        if kind == 0:
            wq, wk, wv = (attn_w_qkv[j][:, :q_dim], attn_w_qkv[j][:, q_dim:q_dim + kv_dim],
                          attn_w_qkv[j][:, q_dim + kv_dim:])
            bq, bk, bv = (attn_b_qkv[j][:q_dim], attn_b_qkv[j][q_dim:q_dim + kv_dim],
                          attn_b_qkv[j][q_dim + kv_dim:])
            w = jnp.concatenate([wq, _doubled(wk, N_KV_HEADS), _doubled(wv, N_KV_HEADS)], axis=1).astype(BF16)
            b = jnp.concatenate([bq, _doubled(bk, N_KV_HEADS), _doubled(bv, N_KV_HEADS)])
            qkv = _norm_matmul(xs, norm_mix[i], mod[:, 0:2], w, b, n_rows=n_in, out_dtype=BF16,
                               rope=(cos, sin, q_dim + 2 * kv_dim, q_dim), **dims)
            o = _attention(qkv, attn_sinks[j], n_ctx=C, ctx_out=update_ctx, **dims)
            xs = _matmul_residual(o, attn_w_o[j].astype(BF16), attn_b_o[j], xs, mod[:, 2:3], n_rows=n_out, **dims)
        else:
            u = _norm_matmul(xs, norm_mix[i], mod[:, 0:2], hy_w_in[j].astype(BF16), hy_b_in[j], n_rows=n_in,
                             out_dtype=F32, **dims)
            filt = (hy_f_w1[j], hy_f_b1[j], hy_f_w2[j], hy_f_b2[j], hy_f_w3[j], hy_f_b3[j], hy_f_freq[j],
                    hy_f_wout[j])
            z = _hyena_core(u[:T].reshape(B, N, 3 * D), hy_conv_w[j], hy_conv_b[j], hy_bias_d[j], filt).reshape(T, D)
            if update_ctx:
                zc = _hyena_core(u[T:].reshape(B, C, 3 * D), hy_conv_w[j], hy_conv_b[j], hy_bias_d[j], filt)
                z = jnp.concatenate([z, zc.reshape(TC, D)], axis=0)
            xs = _matmul_residual(z, hy_w_o[j].astype(BF16), hy_b_o[j], xs, mod[:, 2:3], n_rows=n_out, **dims)
        xs = _moe_layer(xs, i, mod, norm_ffn[i], moe_router_w[i], moe_router_b[i], moe_w_gu, moe_b_gu,
                        moe_w_down, moe_b_down, final_norm, n_rows=n_out, final=(i == DEPTH - 1), **dims)
    return xs.reshape(B, N, D)
```

```python
import functools
import math

import jax
import jax.numpy as jnp
import numpy as np
from jax import lax
from jax.experimental import pallas as pl
from jax.experimental.pallas import tpu as pltpu

DEPTH = 4
N_MIXERS = 2
GRID_W = 64
N_HEADS = 16
N_KV_HEADS = 4
HEAD_DIM = 64
GROUP = N_HEADS // N_KV_HEADS
WINDOW = 128
ROPE_THETA = 10000.0
HY_ORDER = 2
HY_SHORT = 3
HY_BANDS = 16
HY_TARGET = 1e-2
HY_FAST_DECAY = 0.3
HY_SLOW_DECAY = 1.5
N_EXPERTS = 32
TOP_K = 4
SWIGLU_LIMIT = 7.0
SWIGLU_ALPHA = 1.702
NORM_EPS = 1e-6

LANES = 128
OCT = 8
MINOR = 128
TOKEN_TILE = 512
Q_TILE = 128
MOE_TILE = 512
VMEM_LIMIT = 56 * 1024 * 1024

F32 = jnp.float32
BF16 = jnp.bfloat16


def _dot(a, b):
    return jnp.dot(a, b, preferred_element_type=F32)


def _params(*sem):
    return pltpu.CompilerParams(dimension_semantics=sem, vmem_limit_bytes=VMEM_LIMIT)


def _ada_kernel(c_ref, w_ref, b_ref, o_ref):
    c = c_ref[...]
    s = c * (1.0 / (1.0 + jnp.exp(-c)))
    o_ref[...] = jnp.dot(s, w_ref[...], preferred_element_type=F32,
                         precision=lax.Precision.HIGHEST) + b_ref[...]


def _ada_table(c_all, ada_w, ada_b):
    r, d = c_all.shape
    depth, _, n6 = ada_w.shape
    tn = n6 // 4
    return pl.pallas_call(
        _ada_kernel,
        out_shape=jax.ShapeDtypeStruct((depth, r, n6), F32),
        grid=(depth, n6 // tn),
        in_specs=[pl.BlockSpec((r, d), lambda l, j: (0, 0)),
                  pl.BlockSpec((None, d, tn), lambda l, j: (l, 0, j)),
                  pl.BlockSpec((None, 1, tn), lambda l, j: (l, 0, j))],
        out_specs=pl.BlockSpec((None, r, tn), lambda l, j: (l, 0, j)),
        compiler_params=_params("arbitrary", "arbitrary"),
        name="ada_table",
    )(c_all, ada_w, ada_b.reshape(depth, 1, n6))


def _modulated(x, g, mod_ref, row):
    y = x * lax.rsqrt(jnp.mean(x * x, axis=-1, keepdims=True) + NORM_EPS) * g
    return y * (1.0 + mod_ref[row + 1:row + 2, :]) + mod_ref[row:row + 1, :]


def _mod_index(tm, n_seq, n_batch):
    return lambda i: (jnp.minimum(i * tm // n_seq, n_batch), 0, 0)


def _nm_kernel(x_ref, g_ref, mod_ref, w_ref, b_ref, cos_ref, sin_ref, o_ref, *, tn, rope_cols, q_cols):
    h = _modulated(x_ref[...], g_ref[...], mod_ref, 0).astype(BF16)
    n_out = o_ref.shape[1]
    if rope_cols:
        reps = tn // LANES
        cos = jnp.tile(cos_ref[...], (1, reps))
        sin = jnp.tile(sin_ref[...], (1, reps))
        lane = lax.broadcasted_iota(jnp.int32, (1, tn), 1)
        first_half = (lane & (HEAD_DIM - 1)) < (HEAD_DIM // 2)
    for n0 in range(0, n_out, tn):
        acc = _dot(h, w_ref[:, n0:n0 + tn]) + b_ref[:, n0:n0 + tn]
        if n0 < rope_cols:
            partner = jnp.where(first_half, pltpu.roll(acc, tn - HEAD_DIM // 2, 1),
                                pltpu.roll(acc, HEAD_DIM // 2, 1))
            acc = acc * cos + partner * sin
            if n0 < q_cols:
                acc = acc * (HEAD_DIM ** -0.5)
        o_ref[:, n0:n0 + tn] = acc.astype(o_ref.dtype)


def _norm_matmul(xs, gain, mod, w, bias, *, n_rows, n_seq, n_batch, out_dtype, rope=None):
    d = xs.shape[1]
    n_out = w.shape[1]
    tm, tn = TOKEN_TILE, 512
    if rope is None:
        cos = jnp.zeros((tm, LANES), F32)
        sin = cos
        rope_cols = q_cols = 0
        rope_index = lambda i: (0, 0)
    else:
        cos, sin, rope_cols, q_cols = rope
        seq_tiles = n_seq // tm
        lat_tiles = n_seq * n_batch // tm
        rope_index = lambda i: (jnp.where(i < lat_tiles, i % seq_tiles, seq_tiles), 0)
    kern = functools.partial(_nm_kernel, tn=tn, rope_cols=rope_cols, q_cols=q_cols)
    return pl.pallas_call(
        kern,
        out_shape=jax.ShapeDtypeStruct((n_rows, n_out), out_dtype),
        grid=(n_rows // tm,),
        in_specs=[pl.BlockSpec((tm, d), lambda i: (i, 0)),
                  pl.BlockSpec((1, d), lambda i: (0, 0)),
                  pl.BlockSpec((None, 2, d), _mod_index(tm, n_seq, n_batch)),
                  pl.BlockSpec((d, n_out), lambda i: (0, 0)),
                  pl.BlockSpec((1, n_out), lambda i: (0, 0)),
                  pl.BlockSpec((tm, LANES), rope_index),
                  pl.BlockSpec((tm, LANES), rope_index)],
        out_specs=pl.BlockSpec((tm, n_out), lambda i: (i, 0)),
        compiler_params=_params("arbitrary"),
        name="norm_matmul",
    )(xs, gain.reshape(1, d), mod, w, bias.reshape(1, n_out), cos, sin)


def _res_kernel(a_ref, w_ref, b_ref, x_ref, gate_ref, o_ref):
    y = _dot(a_ref[...].astype(BF16), w_ref[...]) + b_ref[...]
    o_ref[...] = x_ref[...] + gate_ref[...] * y


def _matmul_residual(a, w, bias, xs, gate, *, n_rows, n_seq, n_batch):
    k = a.shape[1]
    d = w.shape[1]
    tm = TOKEN_TILE
    return pl.pallas_call(
        _res_kernel,
        out_shape=jax.ShapeDtypeStruct((n_rows, d), F32),
        grid=(n_rows // tm,),
        in_specs=[pl.BlockSpec((tm, k), lambda i: (i, 0)),
                  pl.BlockSpec((k, d), lambda i: (0, 0)),
                  pl.BlockSpec((1, d), lambda i: (0, 0)),
                  pl.BlockSpec((tm, d), lambda i: (i, 0)),
                  pl.BlockSpec((None, 1, d), _mod_index(tm, n_seq, n_batch))],
        out_specs=pl.BlockSpec((tm, d), lambda i: (i, 0)),
        compiler_params=_params("arbitrary"),
        name="matmul_residual",
    )(a, w, bias.reshape(1, d), xs, gate)


def _attend_pairs(q_ref, k_all, v_all, sink_ref, o_ref, kvh, valid):
    tq = q_ref.shape[0]
    lane = lax.broadcasted_iota(jnp.int32, (tq, LANES), 1)
    low = lane < HEAD_DIM
    row = lax.broadcasted_iota(jnp.int32, (2 * tq, 1), 0)
    for j in range(GROUP // 2):
        c0 = kvh * GROUP * HEAD_DIM + j * LANES
        qp = q_ref[:, c0:c0 + LANES]
        zero = jnp.zeros_like(qp)
        q2 = jnp.concatenate([jnp.where(low, qp, zero), jnp.where(low, zero, qp)], axis=0)
        s = lax.dot_general(q2, k_all, (((1,), (1,)), ((), ())), preferred_element_type=F32)
        if valid is not None:
            n_loc = valid.shape[1]
            s = jnp.concatenate([jnp.where(valid, s[:, :n_loc], -jnp.inf), s[:, n_loc:]], axis=1)
        head = kvh * GROUP + 2 * j
        sink = jnp.where(row < tq, sink_ref[head], sink_ref[head + 1])
        m = jnp.maximum(jnp.max(s, axis=-1, keepdims=True), sink)
        p = jnp.exp(s - m)
        denom = jnp.sum(p, axis=-1, keepdims=True) + jnp.exp(sink - m)
        o2 = _dot(p.astype(BF16), v_all) / denom
        o_ref[:, c0:c0 + LANES] = jnp.where(low, o2[:tq], o2[tq:]).astype(o_ref.dtype)


def _attn_kernel(sink_ref, q_ref, kp_ref, kc_ref, kn_ref, vp_ref, vc_ref, vn_ref, kx_ref, vx_ref, o_ref,
                 *, n_q_blocks):
    qi = pl.program_id(1)
    tq = q_ref.shape[0]

    @pl.when(qi < n_q_blocks)
    def _():
        r = lax.broadcasted_iota(jnp.int32, (tq, 3 * tq), 0)
        c = lax.broadcasted_iota(jnp.int32, (tq, 3 * tq), 1)
        kpos = c - tq + qi * tq
        band = (jnp.abs(c - tq - r) <= WINDOW) & (kpos >= 0) & (kpos < n_q_blocks * tq)
        valid = jnp.concatenate([band, band], axis=0)
        for kvh in range(N_KV_HEADS):
            cs = slice(kvh * LANES, (kvh + 1) * LANES)
            k_all = jnp.concatenate([kp_ref[:, cs], kc_ref[:, cs], kn_ref[:, cs], kx_ref[:, cs]], axis=0)
            v_all = jnp.concatenate([vp_ref[:, cs], vc_ref[:, cs], vn_ref[:, cs], vx_ref[:, cs]], axis=0)
            _attend_pairs(q_ref, k_all, v_all, sink_ref, o_ref, kvh, valid)

    @pl.when(qi >= n_q_blocks)
    def _():
        for kvh in range(N_KV_HEADS):
            cs = slice(kvh * LANES, (kvh + 1) * LANES)
            _attend_pairs(q_ref, kx_ref[:, cs], vx_ref[:, cs], sink_ref, o_ref, kvh, None)


def _attention(qkv, sinks, *, n_seq, n_batch, n_ctx, ctx_out):
    tq = Q_TILE
    nq = n_seq // tq
    ncq = n_ctx // tq if ctx_out else 0
    lat_blocks = n_batch * nq
    q_dim = N_HEADS * HEAD_DIM
    kv_w = N_KV_HEADS * LANES
    k_col = q_dim // kv_w
    v_col = k_col + 1
    ctx_row0 = n_batch * n_seq // n_ctx

    def q_index(b, i, s):
        return (jnp.where(i < nq, b * nq + i, lat_blocks + b * (n_ctx // tq) + (i - nq)), 0)

    def kv_index(off, col):
        def index(b, i, s):
            return (b * nq + jnp.clip(i + off, 0, nq - 1), col)
        return index

    n_rows = n_batch * n_seq + (n_batch * n_ctx if ctx_out else 0)
    grid_spec = pltpu.PrefetchScalarGridSpec(
        num_scalar_prefetch=1,
        grid=(n_batch, nq + ncq),
        in_specs=[pl.BlockSpec((tq, q_dim), q_index),
                  pl.BlockSpec((tq, kv_w), kv_index(-1, k_col)),
                  pl.BlockSpec((tq, kv_w), kv_index(0, k_col)),
                  pl.BlockSpec((tq, kv_w), kv_index(1, k_col)),
                  pl.BlockSpec((tq, kv_w), kv_index(-1, v_col)),
                  pl.BlockSpec((tq, kv_w), kv_index(0, v_col)),
                  pl.BlockSpec((tq, kv_w), kv_index(1, v_col)),
                  pl.BlockSpec((n_ctx, kv_w), lambda b, i, s: (ctx_row0 + b, k_col)),
                  pl.BlockSpec((n_ctx, kv_w), lambda b, i, s: (ctx_row0 + b, v_col))],
        out_specs=pl.BlockSpec((tq, q_dim), q_index),
    )
    return pl.pallas_call(
        functools.partial(_attn_kernel, n_q_blocks=nq),
        out_shape=jax.ShapeDtypeStruct((n_rows, q_dim), BF16),
        grid_spec=grid_spec,
        compiler_params=_params("arbitrary", "arbitrary"),
        name="attention",
    )(sinks, qkv, qkv, qkv, qkv, qkv, qkv, qkv, qkv, qkv)


def _router_kernel(x_ref, g_ref, mod_ref, w_ref, b_ref, tok_ref, logit_ref):
    h = _modulated(x_ref[...], g_ref[...], mod_ref, 0)
    tok_ref[...] = h
    logit_ref[...] = jnp.dot(h, w_ref[...], preferred_element_type=F32,
                             precision=lax.Precision.HIGHEST) + b_ref[...]


def _router(xs, gain, mod, w, bias, *, n_rows, n_seq, n_batch):
    d = xs.shape[1]
    tm = TOKEN_TILE
    return pl.pallas_call(
        _router_kernel,
        out_shape=(jax.ShapeDtypeStruct((n_rows, d), F32), jax.ShapeDtypeStruct((n_rows, LANES), F32)),
        grid=(n_rows // tm,),
        in_specs=[pl.BlockSpec((tm, d), lambda i: (i, 0)),
                  pl.BlockSpec((1, d), lambda i: (0, 0)),
                  pl.BlockSpec((None, 2, d), _mod_index(tm, n_seq, n_batch)),
                  pl.BlockSpec((d, LANES), lambda i: (0, 0)),
                  pl.BlockSpec((1, LANES), lambda i: (0, 0))],
        out_specs=(pl.BlockSpec((tm, d), lambda i: (i, 0)), pl.BlockSpec((tm, LANES), lambda i: (i, 0))),
        compiler_params=_params("arbitrary"),
        name="router",
    )(xs, gain.reshape(1, d), mod, w, bias)


def _expert_kernel(be_ref, nb_ref, x_ref, gate_ref, wgu_ref, bgu_ref, wd_ref, bd_ref, o_ref, wgu_bf, wd_bf):
    i = pl.program_id(0)
    fresh = jnp.logical_or(i == 0, be_ref[i] != be_ref[jnp.maximum(i - 1, 0)])

    @pl.when(jnp.logical_and(fresh, i < nb_ref[0]))
    def _():
        wgu_bf[...] = wgu_ref[...].astype(BF16)
        wd_bf[...] = wd_ref[...].astype(BF16)

    @pl.when(i < nb_ref[0])
    def _():
        d_e = wd_ref.shape[0]
        x = x_ref[...].astype(BF16)
        g = jnp.minimum(_dot(x, wgu_bf[:, :d_e]) + bgu_ref[:, :d_e], SWIGLU_LIMIT)
        up = jnp.clip(_dot(x, wgu_bf[:, d_e:]) + bgu_ref[:, d_e:], -SWIGLU_LIMIT, SWIGLU_LIMIT)
        act = g * (1.0 / (1.0 + jnp.exp(-SWIGLU_ALPHA * g))) * (up + 1.0)
        y = _dot(act.astype(BF16), wd_bf[...]) + bd_ref[...]
        o_ref[...] = y * gate_ref[...]

    @pl.when(i >= nb_ref[0])
    def _():
        o_ref[...] = jnp.zeros_like(o_ref)


def _experts(xs, row_gate, block_e, n_used, w_gu, b_gu, w_down, b_down, layer):
    r, d = xs.shape
    n_e, _, d_gu = w_gu.shape[1:]
    tm = MOE_TILE
    grid_spec = pltpu.PrefetchScalarGridSpec(
        num_scalar_prefetch=2,
        grid=(r // tm,),
        in_specs=[pl.BlockSpec((tm, d), lambda i, be, nb: (i, 0)),
                  pl.BlockSpec((tm, 1), lambda i, be, nb: (i, 0)),
                  pl.BlockSpec((None, None, d, d_gu), lambda i, be, nb: (layer, be[i], 0, 0)),
                  pl.BlockSpec((None, None, 1, d_gu), lambda i, be, nb: (layer, be[i], 0, 0)),
                  pl.BlockSpec((None, None, d_gu // 2, d), lambda i, be, nb: (layer, be[i], 0, 0)),
                  pl.BlockSpec((None, None, 1, d), lambda i, be, nb: (layer, be[i], 0, 0))],
        out_specs=pl.BlockSpec((tm, d), lambda i, be, nb: (i, 0)),
        scratch_shapes=[pltpu.VMEM((d, d_gu), BF16), pltpu.VMEM((d_gu // 2, d), BF16)],
    )
    return pl.pallas_call(
        _expert_kernel,
        out_shape=jax.ShapeDtypeStruct((r, d), F32),
        grid_spec=grid_spec,
        compiler_params=_params("arbitrary"),
        name="experts",
    )(block_e, n_used, xs, row_gate, w_gu, b_gu.reshape(b_gu.shape[0], n_e, 1, d_gu), w_down,
      b_down.reshape(b_down.shape[0], n_e, 1, d))


def _combine_kernel(x_ref, y0_ref, y1_ref, y2_ref, y3_ref, gate_ref, g_ref, o_ref, *, final):
    f = (y0_ref[...] + y1_ref[...]) + (y2_ref[...] + y3_ref[...])
    x = x_ref[...] + gate_ref[...] * f
    if final:
        x = x * lax.rsqrt(jnp.mean(x * x, axis=-1, keepdims=True) + NORM_EPS) * g_ref[...]
    o_ref[...] = x


def _combine(xs, ys, gate, final_gain, *, n_rows, n_seq, n_batch, final):
    d = xs.shape[1]
    tm = TOKEN_TILE
    row_spec = pl.BlockSpec((tm, d), lambda i: (i, 0))
    return pl.pallas_call(
        functools.partial(_combine_kernel, final=final),
        out_shape=jax.ShapeDtypeStruct((n_rows, d), F32),
        grid=(n_rows // tm,),
        in_specs=[row_spec] * (1 + TOP_K) + [pl.BlockSpec((None, 1, d), _mod_index(tm, n_seq, n_batch)),
                                             pl.BlockSpec((1, d), lambda i: (0, 0))],
        out_specs=row_spec,
        compiler_params=_params("arbitrary"),
        name="combine",
    )(xs, *ys, gate, final_gain.reshape(1, d))


def _moe_layer(xs, layer, mod, norm_g, router_w, router_b, w_gu, b_gu, w_down, b_down, final_gain,
               *, n_rows, n_seq, n_batch, final):
    d = xs.shape[1]
    rw = jnp.zeros((d, LANES), F32).at[:, :N_EXPERTS].set(router_w)
    rb = jnp.zeros((1, LANES), F32).at[0, :N_EXPERTS].set(router_b)
    tok, logits = _router(xs, norm_g, mod[:, 3:5], rw, rb, n_rows=n_rows, n_seq=n_seq, n_batch=n_batch)
    top_v, top_i = lax.top_k(logits[:, :N_EXPERTS], TOP_K)
    gates = jax.nn.softmax(top_v, axis=-1).reshape(-1)
    n_assign = n_rows * TOP_K
    flat_e = top_i.reshape(-1)
    order = jnp.argsort(flat_e).astype(jnp.int32)
    rank = jnp.argsort(order).astype(jnp.int32)
    counts = jnp.sum(flat_e[:, None] == jnp.arange(N_EXPERTS)[None, :], axis=0).astype(jnp.int32)
    padded = (counts + MOE_TILE - 1) // MOE_TILE * MOE_TILE
    pad_end = jnp.cumsum(padded)
    pad_start = pad_end - padded
    start = jnp.cumsum(counts) - counts
    n_blocks = -(-n_assign // MOE_TILE) + N_EXPERTS
    n_slots = n_blocks * MOE_TILE
    block_e = jnp.minimum(jnp.sum(pad_end[None, :] <= (jnp.arange(n_blocks) * MOE_TILE)[:, None], axis=1),
                          N_EXPERTS - 1).astype(jnp.int32)
    n_used = (pad_end[-1] // MOE_TILE).astype(jnp.int32).reshape(1)
    slot_e = jnp.repeat(block_e, MOE_TILE)
    within = jnp.arange(n_slots, dtype=jnp.int32) - pad_start[slot_e]
    live = within < counts[slot_e]
    src = order[jnp.minimum(start[slot_e] + within, n_assign - 1)]
    row_gate = jnp.where(live, gates[src], 0.0)
    slot_of = (pad_start[flat_e] + rank - start[flat_e]).reshape(n_rows, TOP_K)
    ys = _experts(tok[src // TOP_K], row_gate.reshape(n_slots, 1), block_e, n_used, w_gu, b_gu, w_down, b_down,
                  layer)
    parts = [ys[slot_of[:, k]] for k in range(TOP_K)]
    return _combine(xs, parts, mod[:, 5:6], final_gain, n_rows=n_rows, n_seq=n_seq, n_batch=n_batch, final=final)


def _hyena_spectrum(L, w1, b1, w2, b2, w3, b3, freq, w_out, d):
    t = jnp.linspace(0.0, 1.0, L, dtype=F32)[:, None]
    w = 2.0 * math.pi * jnp.arange(L, dtype=F32)[:, None] / L
    bands = jnp.linspace(1e-4, HY_BANDS - 1, HY_BANDS, dtype=F32)[None, :]
    z = jnp.concatenate([t, jnp.cos(bands * w), -jnp.sin(bands * w)], axis=-1)
    a = jnp.sin(freq * (z @ w1 + b1))
    a = jnp.sin(freq * (a @ w2 + b2))
    a = jnp.sin(freq * (a @ w3 + b3))
    hf = (a @ w_out).reshape(L, 2, HY_ORDER, d)
    deltas = jnp.linspace(math.log(HY_TARGET) / HY_SLOW_DECAY, math.log(HY_TARGET) / HY_FAST_DECAY, d, dtype=F32)
    hf = hf * jnp.exp(-t * jnp.abs(deltas))[:, None, None, :]
    fwd, bwd = hf[:, 0], hf[:, 1]
    g = jnp.concatenate([fwd, jnp.zeros((1, HY_ORDER, d), F32), bwd[:0:-1]], axis=0)
    g = g / jnp.sum(jnp.abs(g), axis=0, keepdims=True)
    n1 = 2 * L // MINOR
    gh = jnp.fft.fft(g, axis=0).reshape(MINOR, n1, HY_ORDER, d)[:, :n1 // 2 + 1]
    gh = jnp.transpose(gh, (2, 1, 0, 3))
    return jnp.stack([gh.real, gh.imag], axis=2).astype(BF16)


def _dft_constants(L):
    n = 2 * L
    n1 = n // MINOR
    n_a, n_k = n1 // 2, n1 // 2 + 1
    k = np.arange(n_k)[:, None]
    a = np.arange(n_a)[None, :]
    th = 2.0 * np.pi * k * a / n1
    eye = np.eye(OCT)
    kgf = np.concatenate([np.kron(np.cos(th), eye), np.kron(-np.sin(th), eye)], axis=0)
    w = np.where((k == 0) | (k == n_a), 1.0, 2.0) / n
    kgi = np.concatenate([np.kron((np.cos(th) * w).T, eye), np.kron((-np.sin(th) * w).T, eye)], axis=1)
    b = np.arange(MINOR)
    ph = 2.0 * np.pi * np.outer(b, b) / MINOR
    fr, fi = np.cos(ph), -np.sin(ph)
    fm = np.block([[fr, -fi], [fi, fr]])
    fmi = np.block([[fr, fi], [-fi, fr]])
    bb = OCT * np.arange(MINOR // OCT)[None, None, :] + np.arange(OCT)[None, :, None]
    tw = 2.0 * np.pi * np.arange(n_k)[:, None, None] * bb / n
    pad = ((0, 0), (0, LANES - MINOR // OCT))
    twc = np.pad(np.cos(tw).reshape(n_k * OCT, -1), pad)
    tws = np.pad(np.sin(tw).reshape(n_k * OCT, -1), pad)
    return (jnp.asarray(kgf, BF16), jnp.asarray(kgi, BF16), jnp.asarray(fm, BF16), jnp.asarray(fmi, BF16),
            jnp.asarray(twc, F32), jnp.asarray(tws, F32))


def _hyena_kernel(z_ref, gate_ref, spec_ref, d_ref, kgf_ref, kgi_ref, fm_ref, fmi_ref, twc_ref, tws_ref, *rest):
    o_ref, a_scr = rest[-2], rest[-1]
    n_k = a_scr.shape[0]
    n_a = n_k - 1
    n_bo = MINOR // OCT
    dt = z_ref.shape[2]

    for bo in range(n_bo):
        xb = z_ref[pl.ds(bo, n_a, stride=n_bo)].reshape(n_a * OCT, dt).astype(BF16)
        p = _dot(kgf_ref[...], xb)
        ar, ai = p[:n_k * OCT], p[n_k * OCT:]
        c, s = twc_ref[:, bo:bo + 1], tws_ref[:, bo:bo + 1]
        a_scr[:, 0, bo] = (ar * c + ai * s).reshape(n_k, OCT, dt)
        a_scr[:, 1, bo] = (ai * c - ar * s).reshape(n_k, OCT, dt)

    def per_major_freq(k, carry):
        x = _dot(fm_ref[...], a_scr[k].reshape(2 * MINOR, dt).astype(BF16))
        xr, xi = x[:MINOR], x[MINOR:]
        gr, gi = spec_ref[k, 0].astype(F32), spec_ref[k, 1].astype(F32)
        y = jnp.concatenate([xr * gr - xi * gi, xr * gi + xi * gr], axis=0).astype(BF16)
        a_scr[k] = _dot(fmi_ref[...], y).reshape(2, n_bo, OCT, dt)
        return carry

    lax.fori_loop(0, n_k, per_major_freq, 0)

    for bo in range(n_bo):
        cr = a_scr[:, 0, bo].reshape(n_k * OCT, dt)
        ci = a_scr[:, 1, bo].reshape(n_k * OCT, dt)
        c, s = twc_ref[:, bo:bo + 1], tws_ref[:, bo:bo + 1]
        dst = jnp.concatenate([cr * c - ci * s, cr * s + ci * c], axis=0).astype(BF16)
        conv = _dot(kgi_ref[...], dst)
        rows = pl.ds(bo, n_a, stride=n_bo)
        zin = z_ref[rows].reshape(n_a * OCT, dt)
        out = gate_ref[rows].reshape(n_a * OCT, dt) * (conv + zin * d_ref[...])
        o_ref[rows] = out.reshape(n_a, OCT, dt)


def _hyena_conv(zin, zin_col0, gate, gate_col0, spec, bias_d, consts, *, seq_len, n_seqs, row0, out_rows, prev_out,
                order=0):
    d = spec.shape[-1]
    dt = LANES
    n_k = spec.shape[1]
    seq_blk = seq_len // OCT
    blk0 = row0 // seq_len
    as_oct = lambda arr: arr.reshape(arr.shape[0] // OCT, OCT, arr.shape[1])
    const_spec = lambda arr: pl.BlockSpec(arr.shape, lambda j, b: (0, 0))
    in_specs = [pl.BlockSpec((seq_blk, OCT, dt), lambda j, b: (blk0 + b, 0, zin_col0 // dt + j)),
                pl.BlockSpec((seq_blk, OCT, dt), lambda j, b: (blk0 + b, 0, gate_col0 // dt + j)),
                pl.BlockSpec((None, n_k, 2, MINOR, dt), lambda j, b: (order, 0, 0, 0, j)),
                pl.BlockSpec((None, 1, dt), lambda j, b: (order, 0, j))] + [const_spec(cst) for cst in consts]
    args = [as_oct(zin), as_oct(gate), spec, bias_d.reshape(bias_d.shape[0], 1, d), *consts]
    aliases = {}
    if prev_out is not None:
        in_specs.append(pl.BlockSpec(memory_space=pl.ANY))
        aliases = {len(args): 0}
        args.append(as_oct(prev_out))
    out = pl.pallas_call(
        _hyena_kernel,
        out_shape=jax.ShapeDtypeStruct((out_rows // OCT, OCT, d), F32),
        grid=(d // dt, n_seqs),
        in_specs=in_specs,
        out_specs=pl.BlockSpec((seq_blk, OCT, dt), lambda j, b: (blk0 + b, 0, j)),
        scratch_shapes=[pltpu.VMEM((n_k, 2, MINOR // OCT, OCT, dt), F32)],
        input_output_aliases=aliases,
        compiler_params=_params("arbitrary", "arbitrary"),
        name="hyena_conv",
    )(*args)
    return out.reshape(out_rows, d)


def _hyena_mixer(uc, bias_d, filt, *, n_seq, n_batch, n_ctx, with_ctx):
    d = uc.shape[1] // 3
    rows = uc.shape[0]
    groups = [(n_seq, n_batch, 0)] + ([(n_ctx, n_batch, n_seq * n_batch)] if with_ctx else [])
    specs = [_hyena_spectrum(L, *filt, d) for L, _, _ in groups]
    consts = [_dft_constants(L) for L, _, _ in groups]
    z = None
    for o in range(HY_ORDER):
        zin, zin_col0 = (uc, 2 * d) if o == 0 else (z, 0)
        out = None
        for (L, n_seqs, row0), spec, cst in zip(groups, specs, consts):
            out = _hyena_conv(zin, zin_col0, uc, o * d, spec, bias_d, cst, seq_len=L, n_seqs=n_seqs, row0=row0,
                              out_rows=rows, prev_out=out, order=o)
        z = out
    return z


def _inproj_kernel(x_ref, xp_ref, xn_ref, g_ref, mod_ref, w_ref, b_ref, cw_ref, cb_ref, o_ref,
                   *, tn, n_seq, n_ctx, lat_tiles):
    i = pl.program_id(0)
    tm = x_ref.shape[0]
    x = jnp.concatenate([xp_ref[...], x_ref[...], xn_ref[...]], axis=0)
    h = _modulated(x, g_ref[...], mod_ref, 0).astype(BF16)
    row = lax.broadcasted_iota(jnp.int32, (tm, 1), 0)
    pos = jnp.where(i < lat_tiles, (i * tm) % n_seq + row, row & (n_ctx - 1))
    last = jnp.where(i < lat_tiles, n_seq - 1, n_ctx - 1)
    has_prev = pos != 0
    has_next = pos != last
    for n0 in range(0, o_ref.shape[1], tn):
        cols = slice(n0, n0 + tn)
        u = _dot(h, w_ref[:, cols]) + b_ref[:, cols]
        prev = jnp.where(has_prev, u[OCT - 1:OCT - 1 + tm], 0.0)
        nxt = jnp.where(has_next, u[OCT + 1:OCT + 1 + tm], 0.0)
        o_ref[:, cols] = (cb_ref[:, cols] + prev * cw_ref[0:1, cols] + u[OCT:OCT + tm] * cw_ref[1:2, cols]
                          + nxt * cw_ref[2:3, cols])


def _inproj(xs, gain, mod, w, bias, conv_w, conv_b, *, n_rows, n_seq, n_batch, n_ctx):
    d = xs.shape[1]
    n_out = w.shape[1]
    tm, tn = TOKEN_TILE, 512
    assert n_seq % tm == 0 and tm % n_ctx == 0 and n_ctx & (n_ctx - 1) == 0
    halo = tm // OCT
    last_halo = xs.shape[0] // OCT - 1
    kern = functools.partial(_inproj_kernel, tn=tn, n_seq=n_seq, n_ctx=n_ctx, lat_tiles=n_seq * n_batch // tm)
    return pl.pallas_call(
        kern,
        out_shape=jax.ShapeDtypeStruct((n_rows, n_out), F32),
        grid=(n_rows // tm,),
        in_specs=[pl.BlockSpec((tm, d), lambda i: (i, 0)),
                  pl.BlockSpec((OCT, d), lambda i: (jnp.maximum(i * halo - 1, 0), 0)),
                  pl.BlockSpec((OCT, d), lambda i: (jnp.minimum((i + 1) * halo, last_halo), 0)),
                  pl.BlockSpec((1, d), lambda i: (0, 0)),
                  pl.BlockSpec((None, 2, d), _mod_index(tm, n_seq, n_batch)),
                  pl.BlockSpec((d, n_out), lambda i: (0, 0)),
                  pl.BlockSpec((1, n_out), lambda i: (0, 0)),
                  pl.BlockSpec((HY_SHORT, n_out), lambda i: (0, 0)),
                  pl.BlockSpec((1, n_out), lambda i: (0, 0))],
        out_specs=pl.BlockSpec((tm, n_out), lambda i: (i, 0)),
        compiler_params=_params("arbitrary"),
        name="hyena_inproj",
    )(xs, xs, xs, gain.reshape(1, d), mod, w, bias.reshape(1, n_out), conv_w, conv_b.reshape(1, n_out))


def _rope_tables(n_seq, tm):
    rows = n_seq // GRID_W
    row = jnp.repeat(jnp.arange(rows), GRID_W).astype(F32)
    col = jnp.tile(jnp.arange(GRID_W), rows).astype(F32)
    half = HEAD_DIM // 2
    n_freq = half // 2
    inv = ROPE_THETA ** (-jnp.arange(n_freq, dtype=F32) / n_freq)
    ang = jnp.concatenate([row[:, None] * inv, col[:, None] * inv], axis=-1)
    cos = jnp.tile(jnp.cos(ang), (1, LANES // half))
    sin = jnp.tile(jnp.concatenate([-jnp.sin(ang), jnp.sin(ang)], axis=-1), (1, LANES // HEAD_DIM))
    cos = jnp.concatenate([cos, jnp.ones((tm, LANES), F32)], axis=0)
    sin = jnp.concatenate([sin, jnp.zeros((tm, LANES), F32)], axis=0)
    return cos, sin


def _doubled(w, n_heads):
    lead = w.shape[:-1]
    w = w.reshape(lead + (n_heads, 1, HEAD_DIM))
    return jnp.broadcast_to(w, lead + (n_heads, 2, HEAD_DIM)).reshape(lead + (n_heads * 2 * HEAD_DIM,))


def kernel(x, c, ctx, c_ctx, ada_w, ada_b, norm_mix, norm_ffn, attn_w_qkv, attn_b_qkv, attn_w_o, attn_b_o, attn_sinks, hy_w_in, hy_b_in, hy_conv_w, hy_conv_b, hy_f_w1, hy_f_b1, hy_f_w2, hy_f_b2, hy_f_w3, hy_f_b3, hy_f_freq, hy_f_wout, hy_bias_d, hy_w_o, hy_b_o, moe_router_w, moe_router_b, moe_w_gu, moe_b_gu, moe_w_down, moe_b_down, final_norm):
    B, N, D = x.shape
    C = ctx.shape[1]
    T, TC = B * N, B * C
    q_dim = N_HEADS * HEAD_DIM
    kv_dim = N_KV_HEADS * HEAD_DIM
    dims = dict(n_seq=N, n_batch=B)

    c_all = jnp.concatenate([c, c_ctx[None, :], jnp.zeros((16 - B - 1, D), F32)], axis=0)
    mod_all = _ada_table(c_all, ada_w, ada_b)[:, :B + 1].reshape(DEPTH, B + 1, 6, D)
    cos, sin = _rope_tables(N, TOKEN_TILE)

    xs = jnp.concatenate([x.reshape(T, D), ctx.reshape(TC, D)], axis=0)
    for i in range(DEPTH):
        kind, j = i % N_MIXERS, i // N_MIXERS
        update_ctx = any(l % N_MIXERS == 0 for l in range(i + 1, DEPTH))
        need_ctx = update_ctx or kind == 0
        mod = mod_all[i]
        n_in = T + TC if need_ctx else T
        n_out = T + TC if update_ctx else T
        if kind == 0:
            wq, wk, wv = (attn_w_qkv[j][:, :q_dim], attn_w_qkv[j][:, q_dim:q_dim + kv_dim],
                          attn_w_qkv[j][:, q_dim + kv_dim:])
            bq, bk, bv = (attn_b_qkv[j][:q_dim], attn_b_qkv[j][q_dim:q_dim + kv_dim],
                          attn_b_qkv[j][q_dim + kv_dim:])
            w = jnp.concatenate([wq, _doubled(wk, N_KV_HEADS), _doubled(wv, N_KV_HEADS)], axis=1).astype(BF16)
            b = jnp.concatenate([bq, _doubled(bk, N_KV_HEADS), _doubled(bv, N_KV_HEADS)])
            qkv = _norm_matmul(xs, norm_mix[i], mod[:, 0:2], w, b, n_rows=n_in, out_dtype=BF16,
                               rope=(cos, sin, q_dim + 2 * kv_dim, q_dim), **dims)
            o = _attention(qkv, attn_sinks[j], n_ctx=C, ctx_out=update_ctx, **dims)
            xs = _matmul_residual(o, attn_w_o[j].astype(BF16), attn_b_o[j], xs, mod[:, 2:3], n_rows=n_out, **dims)
        else:
            uc = _inproj(xs, norm_mix[i], mod[:, 0:2], hy_w_in[j].astype(BF16), hy_b_in[j], hy_conv_w[j],
                         hy_conv_b[j], n_rows=n_in, n_ctx=C, **dims)
            filt = (hy_f_w1[j], hy_f_b1[j], hy_f_w2[j], hy_f_b2[j], hy_f_w3[j], hy_f_b3[j], hy_f_freq[j],
                    hy_f_wout[j])
            z = _hyena_mixer(uc, hy_bias_d[j], filt, n_ctx=C, with_ctx=update_ctx, **dims)
            xs = _matmul_residual(z, hy_w_o[j].astype(BF16), hy_b_o[j], xs, mod[:, 2:3], n_rows=n_out, **dims)
        xs = _moe_layer(xs, i, mod, norm_ffn[i], moe_router_w[i], moe_router_b[i], moe_w_gu, moe_b_gu,
                        moe_w_down, moe_b_down, final_norm, n_rows=n_out, final=(i == DEPTH - 1), **dims)
    return xs.reshape(B, N, D)
```

```python
import functools
import math

import jax
import jax.numpy as jnp
import numpy as np
from jax import lax
from jax.experimental import pallas as pl
from jax.experimental.pallas import tpu as pltpu

DEPTH = 4
N_MIXERS = 2
GRID_W = 64
N_HEADS = 16
N_KV_HEADS = 4
HEAD_DIM = 64
GROUP = N_HEADS // N_KV_HEADS
WINDOW = 128
ROPE_THETA = 10000.0
HY_ORDER = 2
HY_SHORT = 3
HY_BANDS = 16
HY_TARGET = 1e-2
HY_FAST_DECAY = 0.3
HY_SLOW_DECAY = 1.5
N_EXPERTS = 32
TOP_K = 4
SWIGLU_LIMIT = 7.0
SWIGLU_ALPHA = 1.702
NORM_EPS = 1e-6

LANES = 128
OCT = 8
MINOR = 128
TOKEN_TILE = 512
Q_TILE = 128
MOE_TILE = 512
VMEM_LIMIT = 56 * 1024 * 1024

F32 = jnp.float32
BF16 = jnp.bfloat16


def _dot(a, b):
    return jnp.dot(a, b, preferred_element_type=F32)


def _params(*sem):
    return pltpu.CompilerParams(dimension_semantics=sem, vmem_limit_bytes=VMEM_LIMIT)


def _ada_kernel(c_ref, w_ref, b_ref, o_ref):
    c = c_ref[...]
    s = c * (1.0 / (1.0 + jnp.exp(-c)))
    o_ref[...] = jnp.dot(s, w_ref[...], preferred_element_type=F32,
                         precision=lax.Precision.HIGHEST) + b_ref[...]


def _ada_table(c_all, ada_w, ada_b):
    r, d = c_all.shape
    depth, _, n6 = ada_w.shape
    tn = n6 // 4
    return pl.pallas_call(
        _ada_kernel,
        out_shape=jax.ShapeDtypeStruct((depth, r, n6), F32),
        grid=(depth, n6 // tn),
        in_specs=[pl.BlockSpec((r, d), lambda l, j: (0, 0)),
                  pl.BlockSpec((None, d, tn), lambda l, j: (l, 0, j)),
                  pl.BlockSpec((None, 1, tn), lambda l, j: (l, 0, j))],
        out_specs=pl.BlockSpec((None, r, tn), lambda l, j: (l, 0, j)),
        compiler_params=_params("arbitrary", "arbitrary"),
        name="ada_table",
    )(c_all, ada_w, ada_b.reshape(depth, 1, n6))


def _modulated(x, g, mod_ref, row):
    y = x * lax.rsqrt(jnp.mean(x * x, axis=-1, keepdims=True) + NORM_EPS) * g
    return y * (1.0 + mod_ref[row + 1:row + 2, :]) + mod_ref[row:row + 1, :]


def _mod_index(tm, n_seq, n_batch):
    return lambda i: (jnp.minimum(i * tm // n_seq, n_batch), 0, 0)


def _nm_kernel(x_ref, g_ref, mod_ref, w_ref, b_ref, cos_ref, sin_ref, o_ref, *, tn, rope_cols, q_cols):
    h = _modulated(x_ref[...], g_ref[...], mod_ref, 0).astype(BF16)
    n_out = o_ref.shape[1]
    if rope_cols:
        reps = tn // LANES
        cos = jnp.tile(cos_ref[...], (1, reps))
        sin = jnp.tile(sin_ref[...], (1, reps))
        lane = lax.broadcasted_iota(jnp.int32, (1, tn), 1)
        first_half = (lane & (HEAD_DIM - 1)) < (HEAD_DIM // 2)
    for n0 in range(0, n_out, tn):
        acc = _dot(h, w_ref[:, n0:n0 + tn]) + b_ref[:, n0:n0 + tn]
        if n0 < rope_cols:
            partner = jnp.where(first_half, pltpu.roll(acc, tn - HEAD_DIM // 2, 1),
                                pltpu.roll(acc, HEAD_DIM // 2, 1))
            acc = acc * cos + partner * sin
            if n0 < q_cols:
                acc = acc * (HEAD_DIM ** -0.5)
        o_ref[:, n0:n0 + tn] = acc.astype(o_ref.dtype)


def _norm_matmul(xs, gain, mod, w, bias, *, n_rows, n_seq, n_batch, out_dtype, rope=None):
    d = xs.shape[1]
    n_out = w.shape[1]
    tm, tn = TOKEN_TILE, 512
    if rope is None:
        cos = jnp.zeros((tm, LANES), F32)
        sin = cos
        rope_cols = q_cols = 0
        rope_index = lambda i: (0, 0)
    else:
        cos, sin, rope_cols, q_cols = rope
        seq_tiles = n_seq // tm
        lat_tiles = n_seq * n_batch // tm
        rope_index = lambda i: (jnp.where(i < lat_tiles, i % seq_tiles, seq_tiles), 0)
    kern = functools.partial(_nm_kernel, tn=tn, rope_cols=rope_cols, q_cols=q_cols)
    return pl.pallas_call(
        kern,
        out_shape=jax.ShapeDtypeStruct((n_rows, n_out), out_dtype),
        grid=(n_rows // tm,),
        in_specs=[pl.BlockSpec((tm, d), lambda i: (i, 0)),
                  pl.BlockSpec((1, d), lambda i: (0, 0)),
                  pl.BlockSpec((None, 2, d), _mod_index(tm, n_seq, n_batch)),
                  pl.BlockSpec((d, n_out), lambda i: (0, 0)),
                  pl.BlockSpec((1, n_out), lambda i: (0, 0)),
                  pl.BlockSpec((tm, LANES), rope_index),
                  pl.BlockSpec((tm, LANES), rope_index)],
        out_specs=pl.BlockSpec((tm, n_out), lambda i: (i, 0)),
        compiler_params=_params("arbitrary"),
        name="norm_matmul",
    )(xs, gain.reshape(1, d), mod, w, bias.reshape(1, n_out), cos, sin)


def _res_kernel(a_ref, w_ref, b_ref, x_ref, gate_ref, o_ref):
    y = _dot(a_ref[...].astype(BF16), w_ref[...]) + b_ref[...]
    o_ref[...] = x_ref[...] + gate_ref[...] * y


def _matmul_residual(a, w, bias, xs, gate, *, n_rows, n_seq, n_batch):
    k = a.shape[1]
    d = w.shape[1]
    tm = TOKEN_TILE
    return pl.pallas_call(
        _res_kernel,
        out_shape=jax.ShapeDtypeStruct((n_rows, d), F32),
        grid=(n_rows // tm,),
        in_specs=[pl.BlockSpec((tm, k), lambda i: (i, 0)),
                  pl.BlockSpec((k, d), lambda i: (0, 0)),
                  pl.BlockSpec((1, d), lambda i: (0, 0)),
                  pl.BlockSpec((tm, d), lambda i: (i, 0)),
                  pl.BlockSpec((None, 1, d), _mod_index(tm, n_seq, n_batch))],
        out_specs=pl.BlockSpec((tm, d), lambda i: (i, 0)),
        compiler_params=_params("arbitrary"),
        name="matmul_residual",
    )(a, w, bias.reshape(1, d), xs, gate)


def _attend_pairs(q_ref, k_all, v_all, sink_ref, o_ref, kvh, valid):
    tq = q_ref.shape[0]
    lane = lax.broadcasted_iota(jnp.int32, (tq, LANES), 1)
    low = lane < HEAD_DIM
    row = lax.broadcasted_iota(jnp.int32, (2 * tq, 1), 0)
    for j in range(GROUP // 2):
        c0 = kvh * GROUP * HEAD_DIM + j * LANES
        qp = q_ref[:, c0:c0 + LANES]
        zero = jnp.zeros_like(qp)
        q2 = jnp.concatenate([jnp.where(low, qp, zero), jnp.where(low, zero, qp)], axis=0)
        s = lax.dot_general(q2, k_all, (((1,), (1,)), ((), ())), preferred_element_type=F32)
        if valid is not None:
            n_loc = valid.shape[1]
            s = jnp.concatenate([jnp.where(valid, s[:, :n_loc], -jnp.inf), s[:, n_loc:]], axis=1)
        head = kvh * GROUP + 2 * j
        sink = jnp.where(row < tq, sink_ref[head], sink_ref[head + 1])
        m = jnp.maximum(jnp.max(s, axis=-1, keepdims=True), sink)
        p = jnp.exp(s - m)
        denom = jnp.sum(p, axis=-1, keepdims=True) + jnp.exp(sink - m)
        o2 = _dot(p.astype(BF16), v_all) / denom
        o_ref[:, c0:c0 + LANES] = jnp.where(low, o2[:tq], o2[tq:]).astype(o_ref.dtype)


def _attn_kernel(sink_ref, q_ref, kp_ref, kc_ref, kn_ref, vp_ref, vc_ref, vn_ref, kx_ref, vx_ref, o_ref,
                 *, n_q_blocks):
    qi = pl.program_id(1)
    tq = q_ref.shape[0]

    @pl.when(qi < n_q_blocks)
    def _():
        r = lax.broadcasted_iota(jnp.int32, (tq, 3 * tq), 0)
        c = lax.broadcasted_iota(jnp.int32, (tq, 3 * tq), 1)
        kpos = c - tq + qi * tq
        band = (jnp.abs(c - tq - r) <= WINDOW) & (kpos >= 0) & (kpos < n_q_blocks * tq)
        valid = jnp.concatenate([band, band], axis=0)
        for kvh in range(N_KV_HEADS):
            cs = slice(kvh * LANES, (kvh + 1) * LANES)
            k_all = jnp.concatenate([kp_ref[:, cs], kc_ref[:, cs], kn_ref[:, cs], kx_ref[:, cs]], axis=0)
            v_all = jnp.concatenate([vp_ref[:, cs], vc_ref[:, cs], vn_ref[:, cs], vx_ref[:, cs]], axis=0)
            _attend_pairs(q_ref, k_all, v_all, sink_ref, o_ref, kvh, valid)

    @pl.when(qi >= n_q_blocks)
    def _():
        for kvh in range(N_KV_HEADS):
            cs = slice(kvh * LANES, (kvh + 1) * LANES)
            _attend_pairs(q_ref, kx_ref[:, cs], vx_ref[:, cs], sink_ref, o_ref, kvh, None)


def _attention(qkv, sinks, *, n_seq, n_batch, n_ctx, ctx_out):
    tq = Q_TILE
    nq = n_seq // tq
    ncq = n_ctx // tq if ctx_out else 0
    lat_blocks = n_batch * nq
    q_dim = N_HEADS * HEAD_DIM
    kv_w = N_KV_HEADS * LANES
    k_col = q_dim // kv_w
    v_col = k_col + 1
    ctx_row0 = n_batch * n_seq // n_ctx

    def q_index(b, i, s):
        return (jnp.where(i < nq, b * nq + i, lat_blocks + b * (n_ctx // tq) + (i - nq)), 0)

    def kv_index(off, col):
        def index(b, i, s):
            return (b * nq + jnp.clip(i + off, 0, nq - 1), col)
        return index

    n_rows = n_batch * n_seq + (n_batch * n_ctx if ctx_out else 0)
    grid_spec = pltpu.PrefetchScalarGridSpec(
        num_scalar_prefetch=1,
        grid=(n_batch, nq + ncq),
        in_specs=[pl.BlockSpec((tq, q_dim), q_index),
                  pl.BlockSpec((tq, kv_w), kv_index(-1, k_col)),
                  pl.BlockSpec((tq, kv_w), kv_index(0, k_col)),
                  pl.BlockSpec((tq, kv_w), kv_index(1, k_col)),
                  pl.BlockSpec((tq, kv_w), kv_index(-1, v_col)),
                  pl.BlockSpec((tq, kv_w), kv_index(0, v_col)),
                  pl.BlockSpec((tq, kv_w), kv_index(1, v_col)),
                  pl.BlockSpec((n_ctx, kv_w), lambda b, i, s: (ctx_row0 + b, k_col)),
                  pl.BlockSpec((n_ctx, kv_w), lambda b, i, s: (ctx_row0 + b, v_col))],
        out_specs=pl.BlockSpec((tq, q_dim), q_index),
    )
    return pl.pallas_call(
        functools.partial(_attn_kernel, n_q_blocks=nq),
        out_shape=jax.ShapeDtypeStruct((n_rows, q_dim), BF16),
        grid_spec=grid_spec,
        compiler_params=_params("arbitrary", "arbitrary"),
        name="attention",
    )(sinks, qkv, qkv, qkv, qkv, qkv, qkv, qkv, qkv, qkv)


def _router_kernel(x_ref, g_ref, mod_ref, w_ref, b_ref, tri_ref, tok_ref, route_ref, count_ref, seen):
    @pl.when(pl.program_id(0) == 0)
    def _():
        seen[...] = jnp.zeros_like(seen)

    h = _modulated(x_ref[...], g_ref[...], mod_ref, 0)
    tok_ref[...] = h
    logits = jnp.dot(h, w_ref[...], preferred_element_type=F32, precision=lax.Precision.HIGHEST) + b_ref[...]
    lane = lax.broadcasted_iota(jnp.int32, logits.shape, 1)
    rest = jnp.where(lane < N_EXPERTS, logits, -jnp.inf)
    chosen = jnp.zeros(logits.shape, F32)
    top_v, top_i = [], []
    for _ in range(TOP_K):
        best = jnp.max(rest, axis=-1, keepdims=True)
        idx = jnp.min(jnp.where(rest == best, lane, LANES), axis=-1, keepdims=True)
        hit = lane == idx
        top_v.append(best)
        top_i.append(idx)
        rest = jnp.where(hit, -jnp.inf, rest)
        chosen = jnp.where(hit, 1.0, chosen)
    weights = [jnp.exp(v - top_v[0]) for v in top_v]
    denom = (weights[0] + weights[1]) + (weights[2] + weights[3])
    earlier = _dot(tri_ref[...], chosen.astype(BF16)) + seen[...]
    seen[...] = seen[...] + jnp.sum(chosen, axis=0, keepdims=True)
    route = jnp.zeros(logits.shape, F32)
    for k in range(TOP_K):
        rank = jnp.sum(jnp.where(lane == top_i[k], earlier, 0.0), axis=-1, keepdims=True)
        route = jnp.where(lane == k, top_i[k].astype(F32), route)
        route = jnp.where(lane == TOP_K + k, weights[k] / denom, route)
        route = jnp.where(lane == 2 * TOP_K + k, rank, route)
    route_ref[...] = route
    count_ref[...] = jnp.broadcast_to(seen[...], count_ref.shape)


def _router(xs, gain, mod, w, bias, *, n_rows, n_seq, n_batch):
    d = xs.shape[1]
    tm = TOKEN_TILE
    tri = jnp.asarray(np.tril(np.ones((tm, tm)), -1), BF16)
    return pl.pallas_call(
        _router_kernel,
        out_shape=(jax.ShapeDtypeStruct((n_rows, d), F32), jax.ShapeDtypeStruct((n_rows, LANES), F32),
                   jax.ShapeDtypeStruct((OCT, LANES), F32)),
        grid=(n_rows // tm,),
        in_specs=[pl.BlockSpec((tm, d), lambda i: (i, 0)),
                  pl.BlockSpec((1, d), lambda i: (0, 0)),
                  pl.BlockSpec((None, 2, d), _mod_index(tm, n_seq, n_batch)),
                  pl.BlockSpec((d, LANES), lambda i: (0, 0)),
                  pl.BlockSpec((1, LANES), lambda i: (0, 0)),
                  pl.BlockSpec((tm, tm), lambda i: (0, 0))],
        out_specs=(pl.BlockSpec((tm, d), lambda i: (i, 0)), pl.BlockSpec((tm, LANES), lambda i: (i, 0)),
                   pl.BlockSpec((OCT, LANES), lambda i: (0, 0))),
        scratch_shapes=[pltpu.VMEM((1, LANES), F32)],
        compiler_params=_params("arbitrary"),
        name="router",
    )(xs, gain.reshape(1, d), mod, w, bias, tri)


def _expert_kernel(be_ref, nb_ref, x_ref, wgu_ref, bgu_ref, wd_ref, bd_ref, o_ref, wgu_bf, wd_bf):
    i = pl.program_id(0)
    fresh = jnp.logical_or(i == 0, be_ref[i] != be_ref[jnp.maximum(i - 1, 0)])

    @pl.when(jnp.logical_and(fresh, i < nb_ref[0]))
    def _():
        wgu_bf[...] = wgu_ref[...].astype(BF16)
        wd_bf[...] = wd_ref[...].astype(BF16)

    @pl.when(i < nb_ref[0])
    def _():
        d_e = wd_ref.shape[0]
        x = x_ref[...].astype(BF16)
        g = jnp.minimum(_dot(x, wgu_bf[:, :d_e]) + bgu_ref[:, :d_e], SWIGLU_LIMIT)
        up = jnp.clip(_dot(x, wgu_bf[:, d_e:]) + bgu_ref[:, d_e:], -SWIGLU_LIMIT, SWIGLU_LIMIT)
        act = g * (1.0 / (1.0 + jnp.exp(-SWIGLU_ALPHA * g))) * (up + 1.0)
        o_ref[...] = _dot(act.astype(BF16), wd_bf[...]) + bd_ref[...]

    @pl.when(i >= nb_ref[0])
    def _():
        o_ref[...] = jnp.zeros_like(o_ref)


def _experts(xs, block_e, n_used, w_gu, b_gu, w_down, b_down, layer):
    r, d = xs.shape
    n_e, _, d_gu = w_gu.shape[1:]
    tm = MOE_TILE
    grid_spec = pltpu.PrefetchScalarGridSpec(
        num_scalar_prefetch=2,
        grid=(r // tm,),
        in_specs=[pl.BlockSpec((tm, d), lambda i, be, nb: (i, 0)),
                  pl.BlockSpec((None, None, d, d_gu), lambda i, be, nb: (layer, be[i], 0, 0)),
                  pl.BlockSpec((None, None, 1, d_gu), lambda i, be, nb: (layer, be[i], 0, 0)),
                  pl.BlockSpec((None, None, d_gu // 2, d), lambda i, be, nb: (layer, be[i], 0, 0)),
                  pl.BlockSpec((None, None, 1, d), lambda i, be, nb: (layer, be[i], 0, 0))],
        out_specs=pl.BlockSpec((tm, d), lambda i, be, nb: (i, 0)),
        scratch_shapes=[pltpu.VMEM((d, d_gu), BF16), pltpu.VMEM((d_gu // 2, d), BF16)],
    )
    return pl.pallas_call(
        _expert_kernel,
        out_shape=jax.ShapeDtypeStruct((r, d), F32),
        grid_spec=grid_spec,
        compiler_params=_params("arbitrary"),
        name="experts",
    )(block_e, n_used, xs, w_gu, b_gu.reshape(b_gu.shape[0], n_e, 1, d_gu), w_down,
      b_down.reshape(b_down.shape[0], n_e, 1, d))


def _combine_kernel(x_ref, y0_ref, y1_ref, y2_ref, y3_ref, route_ref, gate_ref, g_ref, o_ref, *, final):
    w = [route_ref[:, TOP_K + k:TOP_K + k + 1] for k in range(TOP_K)]
    f = (w[0] * y0_ref[...] + w[1] * y1_ref[...]) + (w[2] * y2_ref[...] + w[3] * y3_ref[...])
    x = x_ref[...] + gate_ref[...] * f
    if final:
        x = x * lax.rsqrt(jnp.mean(x * x, axis=-1, keepdims=True) + NORM_EPS) * g_ref[...]
    o_ref[...] = x


def _combine(xs, ys, route, gate, final_gain, *, n_rows, n_seq, n_batch, final):
    d = xs.shape[1]
    tm = TOKEN_TILE
    row_spec = pl.BlockSpec((tm, d), lambda i: (i, 0))
    return pl.pallas_call(
        functools.partial(_combine_kernel, final=final),
        out_shape=jax.ShapeDtypeStruct((n_rows, d), F32),
        grid=(n_rows // tm,),
        in_specs=[row_spec] * (1 + TOP_K) + [pl.BlockSpec((tm, LANES), lambda i: (i, 0)),
                                             pl.BlockSpec((None, 1, d), _mod_index(tm, n_seq, n_batch)),
                                             pl.BlockSpec((1, d), lambda i: (0, 0))],
        out_specs=row_spec,
        compiler_params=_params("arbitrary"),
        name="combine",
    )(xs, *ys, route, gate, final_gain.reshape(1, d))


def _moe_layer(xs, layer, mod, norm_g, router_w, router_b, w_gu, b_gu, w_down, b_down, final_gain,
               *, n_rows, n_seq, n_batch, final):
    d = xs.shape[1]
    rw = jnp.zeros((d, LANES), F32).at[:, :N_EXPERTS].set(router_w)
    rb = jnp.zeros((1, LANES), F32).at[0, :N_EXPERTS].set(router_b)
    tok, route, count = _router(xs, norm_g, mod[:, 3:5], rw, rb, n_rows=n_rows, n_seq=n_seq, n_batch=n_batch)
    top_i = route[:, :TOP_K].astype(jnp.int32)
    rank = route[:, 2 * TOP_K:3 * TOP_K].astype(jnp.int32)
    counts = count[0, :N_EXPERTS].astype(jnp.int32)
    n_assign = n_rows * TOP_K
    padded = (counts + MOE_TILE - 1) // MOE_TILE * MOE_TILE
    pad_end = jnp.cumsum(padded)
    pad_start = pad_end - padded
    start = jnp.cumsum(counts) - counts
    n_blocks = -(-n_assign // MOE_TILE) + N_EXPERTS
    n_slots = n_blocks * MOE_TILE
    block_e = jnp.minimum(jnp.sum(pad_end[None, :] <= (jnp.arange(n_blocks) * MOE_TILE)[:, None], axis=1),
                          N_EXPERTS - 1).astype(jnp.int32)
    n_used = (pad_end[-1] // MOE_TILE).astype(jnp.int32).reshape(1)
    slot_of = pad_start[top_i] + rank
    by_slot = jnp.argsort(slot_of.reshape(-1)).astype(jnp.int32)
    slot_e = jnp.repeat(block_e, MOE_TILE)
    within = jnp.arange(n_slots, dtype=jnp.int32) - pad_start[slot_e]
    src = by_slot[jnp.clip(start[slot_e] + within, 0, n_assign - 1)]
    ys = _experts(tok[src // TOP_K], block_e, n_used, w_gu, b_gu, w_down, b_down, layer)
    parts = [ys[slot_of[:, k]] for k in range(TOP_K)]
    return _combine(xs, parts, route, mod[:, 5:6], final_gain, n_rows=n_rows, n_seq=n_seq, n_batch=n_batch,
                    final=final)


def _hyena_spectrum(L, w1, b1, w2, b2, w3, b3, freq, w_out, d):
    t = jnp.linspace(0.0, 1.0, L, dtype=F32)[:, None]
    w = 2.0 * math.pi * jnp.arange(L, dtype=F32)[:, None] / L
    bands = jnp.linspace(1e-4, HY_BANDS - 1, HY_BANDS, dtype=F32)[None, :]
    z = jnp.concatenate([t, jnp.cos(bands * w), -jnp.sin(bands * w)], axis=-1)
    a = jnp.sin(freq * (z @ w1 + b1))
    a = jnp.sin(freq * (a @ w2 + b2))
    a = jnp.sin(freq * (a @ w3 + b3))
    hf = (a @ w_out).reshape(L, 2, HY_ORDER, d)
    deltas = jnp.linspace(math.log(HY_TARGET) / HY_SLOW_DECAY, math.log(HY_TARGET) / HY_FAST_DECAY, d, dtype=F32)
    hf = hf * jnp.exp(-t * jnp.abs(deltas))[:, None, None, :]
    fwd, bwd = hf[:, 0], hf[:, 1]
    g = jnp.concatenate([fwd, jnp.zeros((1, HY_ORDER, d), F32), bwd[:0:-1]], axis=0)
    g = g / jnp.sum(jnp.abs(g), axis=0, keepdims=True)
    n1 = 2 * L // MINOR
    gh = jnp.fft.fft(g, axis=0).reshape(MINOR, n1, HY_ORDER, d)[:, :n1 // 2 + 1]
    gh = jnp.transpose(gh, (2, 1, 0, 3))
    gh = jnp.stack([gh.real, gh.imag], axis=2).astype(BF16)
    return jnp.pad(gh, ((0, 0), (0, gh.shape[1] % 2), (0, 0), (0, 0), (0, 0)))


def _dft_constants(L):
    n = 2 * L
    n1 = n // MINOR
    n_a, n_k = n1 // 2, n1 // 2 + 1
    k = np.arange(n_k)[:, None]
    a = np.arange(n_a)[None, :]
    th = 2.0 * np.pi * k * a / n1
    eye = np.eye(OCT)
    kgf = np.concatenate([np.kron(np.cos(th), eye), np.kron(-np.sin(th), eye)], axis=0)
    w = np.where((k == 0) | (k == n_a), 1.0, 2.0) / n
    kgi = np.concatenate([np.kron((np.cos(th) * w).T, eye), np.kron((-np.sin(th) * w).T, eye)], axis=1)
    b = np.arange(MINOR)
    ph = 2.0 * np.pi * np.outer(b, b) / MINOR
    fr, fi = np.cos(ph), -np.sin(ph)
    fm = np.block([[fr, -fi], [fi, fr]])
    fmi = np.block([[fr, fi], [-fi, fr]])
    bb = OCT * np.arange(MINOR // OCT)[None, None, :] + np.arange(OCT)[None, :, None]
    tw = 2.0 * np.pi * np.arange(n_k)[:, None, None] * bb / n
    pad = ((0, 0), (0, LANES - MINOR // OCT))
    twc = np.pad(np.cos(tw).reshape(n_k * OCT, -1), pad)
    tws = np.pad(np.sin(tw).reshape(n_k * OCT, -1), pad)
    return (jnp.asarray(kgf, BF16), jnp.asarray(kgi, BF16), jnp.asarray(fm, BF16), jnp.asarray(fmi, BF16),
            jnp.asarray(twc, F32), jnp.asarray(tws, F32))


def _hyena_kernel(z_ref, gate_ref, spec_ref, d_ref, kgf_ref, kgi_ref, fm_ref, fmi_ref, twc_ref, tws_ref, *rest):
    o_ref, a_scr = rest[-2], rest[-1]
    n_k = kgf_ref.shape[0] // (2 * OCT)
    n_a = n_k - 1
    n_bo = MINOR // OCT
    dt = z_ref.shape[2]
    kr = n_k * OCT
    side_by_side = lambda u, v: jnp.concatenate([u, v], axis=1)

    if a_scr.shape[0] > n_k:
        a_scr[n_k] = jnp.zeros(a_scr.shape[1:], F32)

    for bo in range(0, n_bo, 2):
        x2 = side_by_side(*[z_ref[pl.ds(bo + h, n_a, stride=n_bo)].reshape(n_a * OCT, dt) for h in range(2)])
        p = _dot(kgf_ref[...], x2.astype(BF16))
        for h in range(2):
            ar, ai = p[:kr, h * dt:(h + 1) * dt], p[kr:, h * dt:(h + 1) * dt]
            c, s = twc_ref[:, bo + h:bo + h + 1], tws_ref[:, bo + h:bo + h + 1]
            a_scr[0:n_k, 0, bo + h] = (ar * c + ai * s).reshape(n_k, OCT, dt)
            a_scr[0:n_k, 1, bo + h] = (ai * c - ar * s).reshape(n_k, OCT, dt)

    n_pairs = a_scr.shape[0] // 2
    unroll = max(u for u in range(1, 12) if n_pairs % u == 0)

    def minor_stage(step, carry):
        for j in range(unroll):
            k = 2 * (step * unroll + j)
            b2 = side_by_side(a_scr[k].reshape(2 * MINOR, dt), a_scr[k + 1].reshape(2 * MINOR, dt))
            x = _dot(fm_ref[...], b2.astype(BF16))
            xr, xi = x[:MINOR], x[MINOR:]
            gr = side_by_side(spec_ref[k, 0], spec_ref[k + 1, 0]).astype(F32)
            gi = side_by_side(spec_ref[k, 1], spec_ref[k + 1, 1]).astype(F32)
            y = jnp.concatenate([xr * gr - xi * gi, xr * gi + xi * gr], axis=0).astype(BF16)
            c2 = _dot(fmi_ref[...], y)
            a_scr[k] = c2[:, :dt].reshape(2, n_bo, OCT, dt)
            a_scr[k + 1] = c2[:, dt:].reshape(2, n_bo, OCT, dt)
        return carry

    lax.fori_loop(0, n_pairs // unroll, minor_stage, 0)

    for bo in range(0, n_bo, 2):
        halves = []
        for h in range(2):
            cr = a_scr[0:n_k, 0, bo + h].reshape(kr, dt)
            ci = a_scr[0:n_k, 1, bo + h].reshape(kr, dt)
            c, s = twc_ref[:, bo + h:bo + h + 1], tws_ref[:, bo + h:bo + h + 1]
            halves.append(jnp.concatenate([cr * c - ci * s, cr * s + ci * c], axis=0))
        conv = _dot(kgi_ref[...], side_by_side(*halves).astype(BF16))
        for h in range(2):
            rows = pl.ds(bo + h, n_a, stride=n_bo)
            zin = z_ref[rows].reshape(n_a * OCT, dt)
            out = gate_ref[rows].reshape(n_a * OCT, dt) * (conv[:, h * dt:(h + 1) * dt] + zin * d_ref[...])
            o_ref[rows] = out.reshape(n_a, OCT, dt)


def _hyena_conv(zin, zin_col0, gate, gate_col0, spec, bias_d, consts, *, seq_len, n_seqs, row0, out_rows, prev_out,
                order=0):
    d = spec.shape[-1]
    dt = LANES
    n_k = spec.shape[1]
    seq_blk = seq_len // OCT
    blk0 = row0 // seq_len
    as_oct = lambda arr: arr.reshape(arr.shape[0] // OCT, OCT, arr.shape[1])
    const_spec = lambda arr: pl.BlockSpec(arr.shape, lambda j, b: (0, 0))
    in_specs = [pl.BlockSpec((seq_blk, OCT, dt), lambda j, b: (blk0 + b, 0, zin_col0 // dt + j)),
                pl.BlockSpec((seq_blk, OCT, dt), lambda j, b: (blk0 + b, 0, gate_col0 // dt + j)),
                pl.BlockSpec((None, n_k, 2, MINOR, dt), lambda j, b: (order, 0, 0, 0, j)),
                pl.BlockSpec((None, 1, dt), lambda j, b: (order, 0, j))] + [const_spec(cst) for cst in consts]
    args = [as_oct(zin), as_oct(gate), spec, bias_d.reshape(bias_d.shape[0], 1, d), *consts]
    aliases = {}
    if prev_out is not None:
        in_specs.append(pl.BlockSpec(memory_space=pl.ANY))
        aliases = {len(args): 0}
        args.append(as_oct(prev_out))
    out = pl.pallas_call(
        _hyena_kernel,
        out_shape=jax.ShapeDtypeStruct((out_rows // OCT, OCT, d), F32),
        grid=(d // dt, n_seqs),
        in_specs=in_specs,
        out_specs=pl.BlockSpec((seq_blk, OCT, dt), lambda j, b: (blk0 + b, 0, j)),
        scratch_shapes=[pltpu.VMEM((n_k, 2, MINOR // OCT, OCT, dt), F32)],
        input_output_aliases=aliases,
        compiler_params=_params("arbitrary", "arbitrary"),
        name="hyena_conv",
    )(*args)
    return out.reshape(out_rows, d)


def _hyena_mixer(uc, bias_d, filt, *, n_seq, n_batch, n_ctx, with_ctx):
    d = uc.shape[1] // 3
    rows = uc.shape[0]
    groups = [(n_seq, n_batch, 0)] + ([(n_ctx, n_batch, n_seq * n_batch)] if with_ctx else [])
    specs = [_hyena_spectrum(L, *filt, d) for L, _, _ in groups]
    consts = [_dft_constants(L) for L, _, _ in groups]
    z = None
    for o in range(HY_ORDER):
        zin, zin_col0 = (uc, 2 * d) if o == 0 else (z, 0)
        out = None
        for (L, n_seqs, row0), spec, cst in zip(groups, specs, consts):
            out = _hyena_conv(zin, zin_col0, uc, o * d, spec, bias_d, cst, seq_len=L, n_seqs=n_seqs, row0=row0,
                              out_rows=rows, prev_out=out, order=o)
        z = out
    return z


def _inproj_kernel(x_ref, xp_ref, xn_ref, g_ref, mod_ref, w_ref, b_ref, cw_ref, cb_ref, o_ref,
                   *, tn, n_seq, n_ctx, lat_tiles):
    i = pl.program_id(0)
    tm = x_ref.shape[0]
    x = jnp.concatenate([xp_ref[...], x_ref[...], xn_ref[...]], axis=0)
    h = _modulated(x, g_ref[...], mod_ref, 0).astype(BF16)
    row = lax.broadcasted_iota(jnp.int32, (tm, 1), 0)
    pos = jnp.where(i < lat_tiles, (i * tm) % n_seq + row, row & (n_ctx - 1))
    last = jnp.where(i < lat_tiles, n_seq - 1, n_ctx - 1)
    has_prev = pos != 0
    has_next = pos != last
    for n0 in range(0, o_ref.shape[1], tn):
        cols = slice(n0, n0 + tn)
        u = _dot(h, w_ref[:, cols]) + b_ref[:, cols]
        prev = jnp.where(has_prev, u[OCT - 1:OCT - 1 + tm], 0.0)
        nxt = jnp.where(has_next, u[OCT + 1:OCT + 1 + tm], 0.0)
        o_ref[:, cols] = (cb_ref[:, cols] + prev * cw_ref[0:1, cols] + u[OCT:OCT + tm] * cw_ref[1:2, cols]
                          + nxt * cw_ref[2:3, cols])


def _inproj(xs, gain, mod, w, bias, conv_w, conv_b, *, n_rows, n_seq, n_batch, n_ctx):
    d = xs.shape[1]
    n_out = w.shape[1]
    tm, tn = TOKEN_TILE, 512
    assert n_seq % tm == 0 and tm % n_ctx == 0 and n_ctx & (n_ctx - 1) == 0
    halo = tm // OCT
    last_halo = xs.shape[0] // OCT - 1
    kern = functools.partial(_inproj_kernel, tn=tn, n_seq=n_seq, n_ctx=n_ctx, lat_tiles=n_seq * n_batch // tm)
    return pl.pallas_call(
        kern,
        out_shape=jax.ShapeDtypeStruct((n_rows, n_out), F32),
        grid=(n_rows // tm,),
        in_specs=[pl.BlockSpec((tm, d), lambda i: (i, 0)),
                  pl.BlockSpec((OCT, d), lambda i: (jnp.maximum(i * halo - 1, 0), 0)),
                  pl.BlockSpec((OCT, d), lambda i: (jnp.minimum((i + 1) * halo, last_halo), 0)),
                  pl.BlockSpec((1, d), lambda i: (0, 0)),
                  pl.BlockSpec((None, 2, d), _mod_index(tm, n_seq, n_batch)),
                  pl.BlockSpec((d, n_out), lambda i: (0, 0)),
                  pl.BlockSpec((1, n_out), lambda i: (0, 0)),
                  pl.BlockSpec((HY_SHORT, n_out), lambda i: (0, 0)),
                  pl.BlockSpec((1, n_out), lambda i: (0, 0))],
        out_specs=pl.BlockSpec((tm, n_out), lambda i: (i, 0)),
        compiler_params=_params("arbitrary"),
        name="hyena_inproj",
    )(xs, xs, xs, gain.reshape(1, d), mod, w, bias.reshape(1, n_out), conv_w, conv_b.reshape(1, n_out))


def _rope_tables(n_seq, tm):
    rows = n_seq // GRID_W
    row = jnp.repeat(jnp.arange(rows), GRID_W).astype(F32)
    col = jnp.tile(jnp.arange(GRID_W), rows).astype(F32)
    half = HEAD_DIM // 2
    n_freq = half // 2
    inv = ROPE_THETA ** (-jnp.arange(n_freq, dtype=F32) / n_freq)
    ang = jnp.concatenate([row[:, None] * inv, col[:, None] * inv], axis=-1)
    cos = jnp.tile(jnp.cos(ang), (1, LANES // half))
    sin = jnp.tile(jnp.concatenate([-jnp.sin(ang), jnp.sin(ang)], axis=-1), (1, LANES // HEAD_DIM))
    cos = jnp.concatenate([cos, jnp.ones((tm, LANES), F32)], axis=0)
    sin = jnp.concatenate([sin, jnp.zeros((tm, LANES), F32)], axis=0)
    return cos, sin


def _doubled(w, n_heads):
    lead = w.shape[:-1]
    w = w.reshape(lead + (n_heads, 1, HEAD_DIM))
    return jnp.broadcast_to(w, lead + (n_heads, 2, HEAD_DIM)).reshape(lead + (n_heads * 2 * HEAD_DIM,))


def kernel(x, c, ctx, c_ctx, ada_w, ada_b, norm_mix, norm_ffn, attn_w_qkv, attn_b_qkv, attn_w_o, attn_b_o, attn_sinks, hy_w_in, hy_b_in, hy_conv_w, hy_conv_b, hy_f_w1, hy_f_b1, hy_f_w2, hy_f_b2, hy_f_w3, hy_f_b3, hy_f_freq, hy_f_wout, hy_bias_d, hy_w_o, hy_b_o, moe_router_w, moe_router_b, moe_w_gu, moe_b_gu, moe_w_down, moe_b_down, final_norm):
    B, N, D = x.shape
    C = ctx.shape[1]
    T, TC = B * N, B * C
    q_dim = N_HEADS * HEAD_DIM
    kv_dim = N_KV_HEADS * HEAD_DIM
    dims = dict(n_seq=N, n_batch=B)

    c_all = jnp.concatenate([c, c_ctx[None, :], jnp.zeros((16 - B - 1, D), F32)], axis=0)
    mod_all = _ada_table(c_all, ada_w, ada_b)[:, :B + 1].reshape(DEPTH, B + 1, 6, D)
    cos, sin = _rope_tables(N, TOKEN_TILE)

    xs = jnp.concatenate([x.reshape(T, D), ctx.reshape(TC, D)], axis=0)
    for i in range(DEPTH):
        kind, j = i % N_MIXERS, i // N_MIXERS
        update_ctx = any(l % N_MIXERS == 0 for l in range(i + 1, DEPTH))
        need_ctx = update_ctx or kind == 0
        mod = mod_all[i]
        n_in = T + TC if need_ctx else T
        n_out = T + TC if update_ctx else T
        if kind == 0:
            wq, wk, wv = (attn_w_qkv[j][:, :q_dim], attn_w_qkv[j][:, q_dim:q_dim + kv_dim],
                          attn_w_qkv[j][:, q_dim + kv_dim:])
            bq, bk, bv = (attn_b_qkv[j][:q_dim], attn_b_qkv[j][q_dim:q_dim + kv_dim],
                          attn_b_qkv[j][q_dim + kv_dim:])
            w = jnp.concatenate([wq, _doubled(wk, N_KV_HEADS), _doubled(wv, N_KV_HEADS)], axis=1).astype(BF16)
            b = jnp.concatenate([bq, _doubled(bk, N_KV_HEADS), _doubled(bv, N_KV_HEADS)])
            qkv = _norm_matmul(xs, norm_mix[i], mod[:, 0:2], w, b, n_rows=n_in, out_dtype=BF16,
                               rope=(cos, sin, q_dim + 2 * kv_dim, q_dim), **dims)
            o = _attention(qkv, attn_sinks[j], n_ctx=C, ctx_out=update_ctx, **dims)
            xs = _matmul_residual(o, attn_w_o[j].astype(BF16), attn_b_o[j], xs, mod[:, 2:3], n_rows=n_out, **dims)
        else:
            uc = _inproj(xs, norm_mix[i], mod[:, 0:2], hy_w_in[j].astype(BF16), hy_b_in[j], hy_conv_w[j],
                         hy_conv_b[j], n_rows=n_in, n_ctx=C, **dims)
            filt = (hy_f_w1[j], hy_f_b1[j], hy_f_w2[j], hy_f_b2[j], hy_f_w3[j], hy_f_b3[j], hy_f_freq[j],
                    hy_f_wout[j])
            z = _hyena_mixer(uc, hy_bias_d[j], filt, n_ctx=C, with_ctx=update_ctx, **dims)
            xs = _matmul_residual(z, hy_w_o[j].astype(BF16), hy_b_o[j], xs, mod[:, 2:3], n_rows=n_out, **dims)
        xs = _moe_layer(xs, i, mod, norm_ffn[i], moe_router_w[i], moe_router_b[i], moe_w_gu, moe_b_gu,
                        moe_w_down, moe_b_down, final_norm, n_rows=n_out, final=(i == DEPTH - 1), **dims)
    return xs.reshape(B, N, D)
```

```python
import functools
import math

import jax
import jax.numpy as jnp
import numpy as np
from jax import lax
from jax.experimental import pallas as pl
from jax.experimental.pallas import tpu as pltpu

DEPTH = 4
N_MIXERS = 2
GRID_W = 64
N_HEADS = 16
N_KV_HEADS = 4
HEAD_DIM = 64
GROUP = N_HEADS // N_KV_HEADS
WINDOW = 128
ROPE_THETA = 10000.0
HY_ORDER = 2
HY_SHORT = 3
HY_BANDS = 16
HY_TARGET = 1e-2
HY_FAST_DECAY = 0.3
HY_SLOW_DECAY = 1.5
N_EXPERTS = 32
TOP_K = 4
SWIGLU_LIMIT = 7.0
SWIGLU_ALPHA = 1.702
NORM_EPS = 1e-6

LANES = 128
OCT = 8
MINOR = 128
TOKEN_TILE = 512
Q_TILE = 128
MOE_TILE = 512
VMEM_LIMIT = 56 * 1024 * 1024

F32 = jnp.float32
BF16 = jnp.bfloat16


def _dot(a, b):
    return jnp.dot(a, b, preferred_element_type=F32)


def _params(*sem):
    return pltpu.CompilerParams(dimension_semantics=sem, vmem_limit_bytes=VMEM_LIMIT)


def _ada_kernel(c_ref, w_ref, b_ref, o_ref):
    c = c_ref[...]
    s = c * (1.0 / (1.0 + jnp.exp(-c)))
    o_ref[...] = jnp.dot(s, w_ref[...], preferred_element_type=F32,
                         precision=lax.Precision.HIGHEST) + b_ref[...]


def _ada_table(c_all, ada_w, ada_b):
    r, d = c_all.shape
    depth, _, n6 = ada_w.shape
    tn = n6 // 4
    return pl.pallas_call(
        _ada_kernel,
        out_shape=jax.ShapeDtypeStruct((depth, r, n6), F32),
        grid=(depth, n6 // tn),
        in_specs=[pl.BlockSpec((r, d), lambda l, j: (0, 0)),
                  pl.BlockSpec((None, d, tn), lambda l, j: (l, 0, j)),
                  pl.BlockSpec((None, 1, tn), lambda l, j: (l, 0, j))],
        out_specs=pl.BlockSpec((None, r, tn), lambda l, j: (l, 0, j)),
        compiler_params=_params("arbitrary", "arbitrary"),
        name="ada_table",
    )(c_all, ada_w, ada_b.reshape(depth, 1, n6))


def _modulated(x, g, mod_ref, row):
    y = x * lax.rsqrt(jnp.mean(x * x, axis=-1, keepdims=True) + NORM_EPS) * g
    return y * (1.0 + mod_ref[row + 1:row + 2, :]) + mod_ref[row:row + 1, :]


def _mod_index(tm, n_seq, n_batch):
    return lambda i: (jnp.minimum(i * tm // n_seq, n_batch), 0, 0)


def _nm_kernel(x_ref, g_ref, mod_ref, w_ref, b_ref, cos_ref, sin_ref, o_ref, *, tn, rope_cols, q_cols):
    h = _modulated(x_ref[...], g_ref[...], mod_ref, 0).astype(BF16)
    n_out = o_ref.shape[1]
    if rope_cols:
        reps = tn // LANES
        cos = jnp.tile(cos_ref[...], (1, reps))
        sin = jnp.tile(sin_ref[...], (1, reps))
        lane = lax.broadcasted_iota(jnp.int32, (1, tn), 1)
        first_half = (lane & (HEAD_DIM - 1)) < (HEAD_DIM // 2)
    for n0 in range(0, n_out, tn):
        acc = _dot(h, w_ref[:, n0:n0 + tn]) + b_ref[:, n0:n0 + tn]
        if n0 < rope_cols:
            partner = jnp.where(first_half, pltpu.roll(acc, tn - HEAD_DIM // 2, 1),
                                pltpu.roll(acc, HEAD_DIM // 2, 1))
            acc = acc * cos + partner * sin
            if n0 < q_cols:
                acc = acc * (HEAD_DIM ** -0.5)
        o_ref[:, n0:n0 + tn] = acc.astype(o_ref.dtype)


def _norm_matmul(xs, gain, mod, w, bias, *, n_rows, n_seq, n_batch, out_dtype, rope=None):
    d = xs.shape[1]
    n_out = w.shape[1]
    tm, tn = TOKEN_TILE, 512
    if rope is None:
        cos = jnp.zeros((tm, LANES), F32)
        sin = cos
        rope_cols = q_cols = 0
        rope_index = lambda i: (0, 0)
    else:
        cos, sin, rope_cols, q_cols = rope
        seq_tiles = n_seq // tm
        lat_tiles = n_seq * n_batch // tm
        rope_index = lambda i: (jnp.where(i < lat_tiles, i % seq_tiles, seq_tiles), 0)
    kern = functools.partial(_nm_kernel, tn=tn, rope_cols=rope_cols, q_cols=q_cols)
    return pl.pallas_call(
        kern,
        out_shape=jax.ShapeDtypeStruct((n_rows, n_out), out_dtype),
        grid=(n_rows // tm,),
        in_specs=[pl.BlockSpec((tm, d), lambda i: (i, 0)),
                  pl.BlockSpec((1, d), lambda i: (0, 0)),
                  pl.BlockSpec((None, 2, d), _mod_index(tm, n_seq, n_batch)),
                  pl.BlockSpec((d, n_out), lambda i: (0, 0)),
                  pl.BlockSpec((1, n_out), lambda i: (0, 0)),
                  pl.BlockSpec((tm, LANES), rope_index),
                  pl.BlockSpec((tm, LANES), rope_index)],
        out_specs=pl.BlockSpec((tm, n_out), lambda i: (i, 0)),
        compiler_params=_params("arbitrary"),
        name="norm_matmul",
    )(xs, gain.reshape(1, d), mod, w, bias.reshape(1, n_out), cos, sin)


def _res_kernel(a_ref, w_ref, b_ref, x_ref, gate_ref, o_ref):
    y = _dot(a_ref[...].astype(BF16), w_ref[...]) + b_ref[...]
    o_ref[...] = x_ref[...] + gate_ref[...] * y


def _matmul_residual(a, w, bias, xs, gate, *, n_rows, n_seq, n_batch):
    k = a.shape[1]
    d = w.shape[1]
    tm = TOKEN_TILE
    return pl.pallas_call(
        _res_kernel,
        out_shape=jax.ShapeDtypeStruct((n_rows, d), F32),
        grid=(n_rows // tm,),
        in_specs=[pl.BlockSpec((tm, k), lambda i: (i, 0)),
                  pl.BlockSpec((k, d), lambda i: (0, 0)),
                  pl.BlockSpec((1, d), lambda i: (0, 0)),
                  pl.BlockSpec((tm, d), lambda i: (i, 0)),
                  pl.BlockSpec((None, 1, d), _mod_index(tm, n_seq, n_batch))],
        out_specs=pl.BlockSpec((tm, d), lambda i: (i, 0)),
        compiler_params=_params("arbitrary"),
        name="matmul_residual",
    )(a, w, bias.reshape(1, d), xs, gate)


def _attend_pairs(q_ref, k_all, v_all, sink_ref, o_ref, kvh, valid):
    tq = q_ref.shape[0]
    lane = lax.broadcasted_iota(jnp.int32, (tq, LANES), 1)
    low = lane < HEAD_DIM
    row = lax.broadcasted_iota(jnp.int32, (2 * tq, 1), 0)
    for j in range(GROUP // 2):
        c0 = kvh * GROUP * HEAD_DIM + j * LANES
        qp = q_ref[:, c0:c0 + LANES]
        zero = jnp.zeros_like(qp)
        q2 = jnp.concatenate([jnp.where(low, qp, zero), jnp.where(low, zero, qp)], axis=0)
        s = lax.dot_general(q2, k_all, (((1,), (1,)), ((), ())), preferred_element_type=F32)
        if valid is not None:
            n_loc = valid.shape[1]
            s = jnp.concatenate([jnp.where(valid, s[:, :n_loc], -jnp.inf), s[:, n_loc:]], axis=1)
        head = kvh * GROUP + 2 * j
        sink = jnp.where(row < tq, sink_ref[head], sink_ref[head + 1])
        m = jnp.maximum(jnp.max(s, axis=-1, keepdims=True), sink)
        p = jnp.exp(s - m)
        denom = jnp.sum(p, axis=-1, keepdims=True) + jnp.exp(sink - m)
        o2 = _dot(p.astype(BF16), v_all) / denom
        o_ref[:, c0:c0 + LANES] = jnp.where(low, o2[:tq], o2[tq:]).astype(o_ref.dtype)


def _attn_kernel(sink_ref, q_ref, kp_ref, kc_ref, kn_ref, vp_ref, vc_ref, vn_ref, kx_ref, vx_ref, o_ref,
                 *, n_q_blocks):
    qi = pl.program_id(1)
    tq = q_ref.shape[0]

    @pl.when(qi < n_q_blocks)
    def _():
        r = lax.broadcasted_iota(jnp.int32, (tq, 3 * tq), 0)
        c = lax.broadcasted_iota(jnp.int32, (tq, 3 * tq), 1)
        kpos = c - tq + qi * tq
        band = (jnp.abs(c - tq - r) <= WINDOW) & (kpos >= 0) & (kpos < n_q_blocks * tq)
        valid = jnp.concatenate([band, band], axis=0)
        for kvh in range(N_KV_HEADS):
            cs = slice(kvh * LANES, (kvh + 1) * LANES)
            k_all = jnp.concatenate([kp_ref[:, cs], kc_ref[:, cs], kn_ref[:, cs], kx_ref[:, cs]], axis=0)
            v_all = jnp.concatenate([vp_ref[:, cs], vc_ref[:, cs], vn_ref[:, cs], vx_ref[:, cs]], axis=0)
            _attend_pairs(q_ref, k_all, v_all, sink_ref, o_ref, kvh, valid)

    @pl.when(qi >= n_q_blocks)
    def _():
        for kvh in range(N_KV_HEADS):
            cs = slice(kvh * LANES, (kvh + 1) * LANES)
            _attend_pairs(q_ref, kx_ref[:, cs], vx_ref[:, cs], sink_ref, o_ref, kvh, None)


def _attention(qkv, sinks, *, n_seq, n_batch, n_ctx, ctx_out):
    tq = Q_TILE
    nq = n_seq // tq
    ncq = n_ctx // tq if ctx_out else 0
    lat_blocks = n_batch * nq
    q_dim = N_HEADS * HEAD_DIM
    kv_w = N_KV_HEADS * LANES
    k_col = q_dim // kv_w
    v_col = k_col + 1
    ctx_row0 = n_batch * n_seq // n_ctx

    def q_index(b, i, s):
        return (jnp.where(i < nq, b * nq + i, lat_blocks + b * (n_ctx // tq) + (i - nq)), 0)

    def kv_index(off, col):
        def index(b, i, s):
            return (b * nq + jnp.clip(i + off, 0, nq - 1), col)
        return index

    n_rows = n_batch * n_seq + (n_batch * n_ctx if ctx_out else 0)
    grid_spec = pltpu.PrefetchScalarGridSpec(
        num_scalar_prefetch=1,
        grid=(n_batch, nq + ncq),
        in_specs=[pl.BlockSpec((tq, q_dim), q_index),
                  pl.BlockSpec((tq, kv_w), kv_index(-1, k_col)),
                  pl.BlockSpec((tq, kv_w), kv_index(0, k_col)),
                  pl.BlockSpec((tq, kv_w), kv_index(1, k_col)),
                  pl.BlockSpec((tq, kv_w), kv_index(-1, v_col)),
                  pl.BlockSpec((tq, kv_w), kv_index(0, v_col)),
                  pl.BlockSpec((tq, kv_w), kv_index(1, v_col)),
                  pl.BlockSpec((n_ctx, kv_w), lambda b, i, s: (ctx_row0 + b, k_col)),
                  pl.BlockSpec((n_ctx, kv_w), lambda b, i, s: (ctx_row0 + b, v_col))],
        out_specs=pl.BlockSpec((tq, q_dim), q_index),
    )
    return pl.pallas_call(
        functools.partial(_attn_kernel, n_q_blocks=nq),
        out_shape=jax.ShapeDtypeStruct((n_rows, q_dim), BF16),
        grid_spec=grid_spec,
        compiler_params=_params("arbitrary", "arbitrary"),
        name="attention",
    )(sinks, qkv, qkv, qkv, qkv, qkv, qkv, qkv, qkv, qkv)


def _router_kernel(x_ref, g_ref, mod_ref, w_ref, b_ref, tri_ref, tok_ref, route_ref, count_ref, seen):
    @pl.when(pl.program_id(0) == 0)
    def _():
        seen[...] = jnp.zeros_like(seen)

    h = _modulated(x_ref[...], g_ref[...], mod_ref, 0)
    tok_ref[...] = h.astype(tok_ref.dtype)
    logits = jnp.dot(h, w_ref[...], preferred_element_type=F32, precision=lax.Precision.HIGHEST) + b_ref[...]
    lane = lax.broadcasted_iota(jnp.int32, logits.shape, 1)
    rest = jnp.where(lane < N_EXPERTS, logits, -jnp.inf)
    chosen = jnp.zeros(logits.shape, F32)
    top_v, top_i = [], []
    for _ in range(TOP_K):
        best = jnp.max(rest, axis=-1, keepdims=True)
        idx = jnp.min(jnp.where(rest == best, lane, LANES), axis=-1, keepdims=True)
        hit = lane == idx
        top_v.append(best)
        top_i.append(idx)
        rest = jnp.where(hit, -jnp.inf, rest)
        chosen = jnp.where(hit, 1.0, chosen)
    weights = [jnp.exp(v - top_v[0]) for v in top_v]
    denom = (weights[0] + weights[1]) + (weights[2] + weights[3])
    earlier = _dot(tri_ref[...], chosen.astype(BF16)) + seen[...]
    seen[...] = seen[...] + jnp.sum(chosen, axis=0, keepdims=True)
    route = jnp.zeros(logits.shape, F32)
    for k in range(TOP_K):
        rank = jnp.sum(jnp.where(lane == top_i[k], earlier, 0.0), axis=-1, keepdims=True)
        route = jnp.where(lane == k, top_i[k].astype(F32), route)
        route = jnp.where(lane == TOP_K + k, weights[k] / denom, route)
        route = jnp.where(lane == 2 * TOP_K + k, rank, route)
    route_ref[...] = route
    count_ref[...] = jnp.broadcast_to(seen[...], count_ref.shape)


def _router(xs, gain, mod, w, bias, *, n_rows, n_seq, n_batch):
    d = xs.shape[1]
    tm = TOKEN_TILE
    tri = jnp.asarray(np.tril(np.ones((tm, tm)), -1), BF16)
    return pl.pallas_call(
        _router_kernel,
        out_shape=(jax.ShapeDtypeStruct((n_rows, d), BF16), jax.ShapeDtypeStruct((n_rows, LANES), F32),
                   jax.ShapeDtypeStruct((OCT, LANES), F32)),
        grid=(n_rows // tm,),
        in_specs=[pl.BlockSpec((tm, d), lambda i: (i, 0)),
                  pl.BlockSpec((1, d), lambda i: (0, 0)),
                  pl.BlockSpec((None, 2, d), _mod_index(tm, n_seq, n_batch)),
                  pl.BlockSpec((d, LANES), lambda i: (0, 0)),
                  pl.BlockSpec((1, LANES), lambda i: (0, 0)),
                  pl.BlockSpec((tm, tm), lambda i: (0, 0))],
        out_specs=(pl.BlockSpec((tm, d), lambda i: (i, 0)), pl.BlockSpec((tm, LANES), lambda i: (i, 0)),
                   pl.BlockSpec((OCT, LANES), lambda i: (0, 0))),
        scratch_shapes=[pltpu.VMEM((1, LANES), F32)],
        compiler_params=_params("arbitrary"),
        name="router",
    )(xs, gain.reshape(1, d), mod, w, bias, tri)


def _expert_kernel(be_ref, nb_ref, x_ref, wgu_ref, bgu_ref, wd_ref, bd_ref, o_ref, wgu_bf, wd_bf):
    i = pl.program_id(0)
    fresh = jnp.logical_or(i == 0, be_ref[i] != be_ref[jnp.maximum(i - 1, 0)])

    @pl.when(jnp.logical_and(fresh, i < nb_ref[0]))
    def _():
        wgu_bf[...] = wgu_ref[...].astype(BF16)
        wd_bf[...] = wd_ref[...].astype(BF16)

    @pl.when(i < nb_ref[0])
    def _():
        d_e = wd_ref.shape[0]
        x = x_ref[...]
        g = jnp.minimum(_dot(x, wgu_bf[:, :d_e]) + bgu_ref[:, :d_e], SWIGLU_LIMIT)
        up = jnp.clip(_dot(x, wgu_bf[:, d_e:]) + bgu_ref[:, d_e:], -SWIGLU_LIMIT, SWIGLU_LIMIT)
        act = g * (1.0 / (1.0 + jnp.exp(-SWIGLU_ALPHA * g))) * (up + 1.0)
        o_ref[...] = (_dot(act.astype(BF16), wd_bf[...]) + bd_ref[...]).astype(o_ref.dtype)

    @pl.when(i >= nb_ref[0])
    def _():
        o_ref[...] = jnp.zeros_like(o_ref)


def _experts(xs, block_e, n_used, w_gu, b_gu, w_down, b_down, layer):
    r, d = xs.shape
    n_e, _, d_gu = w_gu.shape[1:]
    tm = MOE_TILE
    grid_spec = pltpu.PrefetchScalarGridSpec(
        num_scalar_prefetch=2,
        grid=(r // tm,),
        in_specs=[pl.BlockSpec((tm, d), lambda i, be, nb: (i, 0)),
                  pl.BlockSpec((None, None, d, d_gu), lambda i, be, nb: (layer, be[i], 0, 0)),
                  pl.BlockSpec((None, None, 1, d_gu), lambda i, be, nb: (layer, be[i], 0, 0)),
                  pl.BlockSpec((None, None, d_gu // 2, d), lambda i, be, nb: (layer, be[i], 0, 0)),
                  pl.BlockSpec((None, None, 1, d), lambda i, be, nb: (layer, be[i], 0, 0))],
        out_specs=pl.BlockSpec((tm, d), lambda i, be, nb: (i, 0)),
        scratch_shapes=[pltpu.VMEM((d, d_gu), BF16), pltpu.VMEM((d_gu // 2, d), BF16)],
    )
    return pl.pallas_call(
        _expert_kernel,
        out_shape=jax.ShapeDtypeStruct((r, d), BF16),
        grid_spec=grid_spec,
        compiler_params=_params("arbitrary"),
        name="experts",
    )(block_e, n_used, xs, w_gu, b_gu.reshape(b_gu.shape[0], n_e, 1, d_gu), w_down,
      b_down.reshape(b_down.shape[0], n_e, 1, d))


def _combine_kernel(x_ref, y0_ref, y1_ref, y2_ref, y3_ref, route_ref, gate_ref, g_ref, o_ref, *, final):
    w = [route_ref[:, TOP_K + k:TOP_K + k + 1] for k in range(TOP_K)]
    y = [r[...].astype(F32) for r in (y0_ref, y1_ref, y2_ref, y3_ref)]
    f = (w[0] * y[0] + w[1] * y[1]) + (w[2] * y[2] + w[3] * y[3])
    x = x_ref[...] + gate_ref[...] * f
    if final:
        x = x * lax.rsqrt(jnp.mean(x * x, axis=-1, keepdims=True) + NORM_EPS) * g_ref[...]
    o_ref[...] = x


def _combine(xs, ys, route, gate, final_gain, *, n_rows, n_seq, n_batch, final):
    d = xs.shape[1]
    tm = TOKEN_TILE
    row_spec = pl.BlockSpec((tm, d), lambda i: (i, 0))
    return pl.pallas_call(
        functools.partial(_combine_kernel, final=final),
        out_shape=jax.ShapeDtypeStruct((n_rows, d), F32),
        grid=(n_rows // tm,),
        in_specs=[row_spec] * (1 + TOP_K) + [pl.BlockSpec((tm, LANES), lambda i: (i, 0)),
                                             pl.BlockSpec((None, 1, d), _mod_index(tm, n_seq, n_batch)),
                                             pl.BlockSpec((1, d), lambda i: (0, 0))],
        out_specs=row_spec,
        compiler_params=_params("arbitrary"),
        name="combine",
    )(xs, *ys, route, gate, final_gain.reshape(1, d))


def _moe_layer(xs, layer, mod, norm_g, router_w, router_b, w_gu, b_gu, w_down, b_down, final_gain,
               *, n_rows, n_seq, n_batch, final):
    d = xs.shape[1]
    rw = jnp.zeros((d, LANES), F32).at[:, :N_EXPERTS].set(router_w)
    rb = jnp.zeros((1, LANES), F32).at[0, :N_EXPERTS].set(router_b)
    tok, route, count = _router(xs, norm_g, mod[:, 3:5], rw, rb, n_rows=n_rows, n_seq=n_seq, n_batch=n_batch)
    top_i = route[:, :TOP_K].astype(jnp.int32)
    rank = route[:, 2 * TOP_K:3 * TOP_K].astype(jnp.int32)
    counts = count[0, :N_EXPERTS].astype(jnp.int32)
    n_assign = n_rows * TOP_K
    padded = (counts + MOE_TILE - 1) // MOE_TILE * MOE_TILE
    pad_end = jnp.cumsum(padded)
    pad_start = pad_end - padded
    start = jnp.cumsum(counts) - counts
    n_blocks = -(-n_assign // MOE_TILE) + N_EXPERTS
    n_slots = n_blocks * MOE_TILE
    block_e = jnp.minimum(jnp.sum(pad_end[None, :] <= (jnp.arange(n_blocks) * MOE_TILE)[:, None], axis=1),
                          N_EXPERTS - 1).astype(jnp.int32)
    n_used = (pad_end[-1] // MOE_TILE).astype(jnp.int32).reshape(1)
    slot_of = pad_start[top_i] + rank
    by_slot = jnp.argsort(slot_of.reshape(-1)).astype(jnp.int32)
    slot_e = jnp.repeat(block_e, MOE_TILE)
    within = jnp.arange(n_slots, dtype=jnp.int32) - pad_start[slot_e]
    src = by_slot[jnp.clip(start[slot_e] + within, 0, n_assign - 1)]
    ys = _experts(tok[src // TOP_K], block_e, n_used, w_gu, b_gu, w_down, b_down, layer)
    parts = [ys[slot_of[:, k]] for k in range(TOP_K)]
    return _combine(xs, parts, route, mod[:, 5:6], final_gain, n_rows=n_rows, n_seq=n_seq, n_batch=n_batch,
                    final=final)


def _hyena_spectrum(L, consts, w1, b1, w2, b2, w3, b3, freq, w_out, d):
    t = jnp.linspace(0.0, 1.0, L, dtype=F32)[:, None]
    w = 2.0 * math.pi * jnp.arange(L, dtype=F32)[:, None] / L
    bands = jnp.linspace(1e-4, HY_BANDS - 1, HY_BANDS, dtype=F32)[None, :]
    z = jnp.concatenate([t, jnp.cos(bands * w), -jnp.sin(bands * w)], axis=-1)
    a = jnp.sin(freq * (z @ w1 + b1))
    a = jnp.sin(freq * (a @ w2 + b2))
    a = jnp.sin(freq * (a @ w3 + b3))
    hf = (a @ w_out).reshape(L, 2, HY_ORDER, d)
    deltas = jnp.linspace(math.log(HY_TARGET) / HY_SLOW_DECAY, math.log(HY_TARGET) / HY_FAST_DECAY, d, dtype=F32)
    hf = hf * jnp.exp(-t * jnp.abs(deltas))[:, None, None, :]
    return _filter_spectrum(hf.reshape(L // OCT, OCT, 2 * HY_ORDER * d), consts, d)


def _filter_kernel(f_ref, b_ref, kgf_ref, fm_ref, twc_ref, tws_ref, spec_ref, a_scr):
    n_k = kgf_ref.shape[0] // (2 * OCT)
    n_a = n_k - 1
    n_bo = MINOR // OCT
    dt = f_ref.shape[2]
    kr = n_k * OCT
    total = jnp.zeros((1, 2 * dt), F32)
    for bo in range(n_bo):
        rows = pl.ds(bo, n_a, stride=n_bo)
        x2 = jnp.concatenate([f_ref[rows].reshape(n_a * OCT, dt), b_ref[rows].reshape(n_a * OCT, dt)], axis=1)
        total = total + jnp.sum(jnp.abs(x2), axis=0, keepdims=True)
        p = _dot(kgf_ref[...], x2.astype(BF16))
        ar, ai = p[:kr], p[kr:]
        c, s = twc_ref[:, bo:bo + 1], tws_ref[:, bo:bo + 1]
        a_scr[:, 0, bo] = (ar * c + ai * s).reshape(n_k, OCT, 2 * dt)
        a_scr[:, 1, bo] = (ai * c - ar * s).reshape(n_k, OCT, 2 * dt)
    b0 = b_ref[0, 0:1, :]
    inv = 1.0 / (total[:, :dt] + total[:, dt:] - jnp.abs(b0))
    unroll = max(u for u in range(1, 12) if n_k % u == 0)

    def minor_stage(step, carry):
        for j in range(unroll):
            k = step * unroll + j
            x = _dot(fm_ref[...], a_scr[k].reshape(2 * MINOR, 2 * dt).astype(BF16))
            spec_ref[k, 0] = ((x[:MINOR, :dt] + x[:MINOR, dt:] - b0) * inv).astype(spec_ref.dtype)
            spec_ref[k, 1] = ((x[MINOR:, :dt] - x[MINOR:, dt:]) * inv).astype(spec_ref.dtype)
        return carry

    lax.fori_loop(0, n_k // unroll, minor_stage, 0)
    for k in range(n_k, spec_ref.shape[0]):
        spec_ref[k] = jnp.zeros(spec_ref.shape[1:], spec_ref.dtype)


def _filter_spectrum(hf, consts, d):
    kgf, _, fm, _, twc, tws = consts
    dt = LANES
    seq_blk = hf.shape[0]
    n_k = kgf.shape[0] // (2 * OCT)
    n_kp = n_k + n_k % 2
    tiles = d // dt
    const_spec = lambda arr: pl.BlockSpec(arr.shape, lambda o, j: (0, 0))
    return pl.pallas_call(
        _filter_kernel,
        out_shape=jax.ShapeDtypeStruct((HY_ORDER, n_kp, 2, MINOR, d), BF16),
        grid=(HY_ORDER, tiles),
        in_specs=[pl.BlockSpec((seq_blk, OCT, dt), lambda o, j: (0, 0, o * tiles + j)),
                  pl.BlockSpec((seq_blk, OCT, dt), lambda o, j: (0, 0, (HY_ORDER + o) * tiles + j)),
                  const_spec(kgf), const_spec(fm), const_spec(twc), const_spec(tws)],
        out_specs=pl.BlockSpec((None, n_kp, 2, MINOR, dt), lambda o, j: (o, 0, 0, 0, j)),
        scratch_shapes=[pltpu.VMEM((n_k, 2, MINOR // OCT, OCT, 2 * dt), F32)],
        compiler_params=_params("arbitrary", "arbitrary"),
        name="hyena_filter",
    )(hf, hf, kgf, fm, twc, tws)


def _dft_constants(L):
    n = 2 * L
    n1 = n // MINOR
    n_a, n_k = n1 // 2, n1 // 2 + 1
    k = np.arange(n_k)[:, None]
    a = np.arange(n_a)[None, :]
    th = 2.0 * np.pi * k * a / n1
    eye = np.eye(OCT)
    kgf = np.concatenate([np.kron(np.cos(th), eye), np.kron(-np.sin(th), eye)], axis=0)
    w = np.where((k == 0) | (k == n_a), 1.0, 2.0) / n
    kgi = np.concatenate([np.kron((np.cos(th) * w).T, eye), np.kron((-np.sin(th) * w).T, eye)], axis=1)
    b = np.arange(MINOR)
    ph = 2.0 * np.pi * np.outer(b, b) / MINOR
    fr, fi = np.cos(ph), -np.sin(ph)
    fm = np.block([[fr, -fi], [fi, fr]])
    fmi = np.block([[fr, fi], [-fi, fr]])
    bb = OCT * np.arange(MINOR // OCT)[None, None, :] + np.arange(OCT)[None, :, None]
    tw = 2.0 * np.pi * np.arange(n_k)[:, None, None] * bb / n
    pad = ((0, 0), (0, LANES - MINOR // OCT))
    twc = np.pad(np.cos(tw).reshape(n_k * OCT, -1), pad)
    tws = np.pad(np.sin(tw).reshape(n_k * OCT, -1), pad)
    return (jnp.asarray(kgf, BF16), jnp.asarray(kgi, BF16), jnp.asarray(fm, BF16), jnp.asarray(fmi, BF16),
            jnp.asarray(twc, F32), jnp.asarray(tws, F32))


def _hyena_kernel(z_ref, gate_ref, spec_ref, d_ref, kgf_ref, kgi_ref, fm_ref, fmi_ref, twc_ref, tws_ref, *rest):
    o_ref, a_scr = rest[-2], rest[-1]
    n_k = kgf_ref.shape[0] // (2 * OCT)
    n_a = n_k - 1
    n_bo = MINOR // OCT
    dt = z_ref.shape[2]
    kr = n_k * OCT
    side_by_side = lambda u, v: jnp.concatenate([u, v], axis=1)

    if a_scr.shape[0] > n_k:
        a_scr[n_k] = jnp.zeros(a_scr.shape[1:], F32)

    for bo in range(0, n_bo, 2):
        x2 = side_by_side(*[z_ref[pl.ds(bo + h, n_a, stride=n_bo)].reshape(n_a * OCT, dt) for h in range(2)])
        p = _dot(kgf_ref[...], x2.astype(BF16))
        for h in range(2):
            ar, ai = p[:kr, h * dt:(h + 1) * dt], p[kr:, h * dt:(h + 1) * dt]
            c, s = twc_ref[:, bo + h:bo + h + 1], tws_ref[:, bo + h:bo + h + 1]
            a_scr[0:n_k, 0, bo + h] = (ar * c + ai * s).reshape(n_k, OCT, dt)
            a_scr[0:n_k, 1, bo + h] = (ai * c - ar * s).reshape(n_k, OCT, dt)

    n_pairs = a_scr.shape[0] // 2
    unroll = max(u for u in range(1, 12) if n_pairs % u == 0)

    def minor_stage(step, carry):
        for j in range(unroll):
            k = 2 * (step * unroll + j)
            b2 = side_by_side(a_scr[k].reshape(2 * MINOR, dt), a_scr[k + 1].reshape(2 * MINOR, dt))
            x = _dot(fm_ref[...], b2.astype(BF16))
            xr, xi = x[:MINOR], x[MINOR:]
            gr = side_by_side(spec_ref[k, 0], spec_ref[k + 1, 0]).astype(F32)
            gi = side_by_side(spec_ref[k, 1], spec_ref[k + 1, 1]).astype(F32)
            y = jnp.concatenate([xr * gr - xi * gi, xr * gi + xi * gr], axis=0).astype(BF16)
            c2 = _dot(fmi_ref[...], y)
            a_scr[k] = c2[:, :dt].reshape(2, n_bo, OCT, dt)
            a_scr[k + 1] = c2[:, dt:].reshape(2, n_bo, OCT, dt)
        return carry

    lax.fori_loop(0, n_pairs // unroll, minor_stage, 0)

    for bo in range(0, n_bo, 2):
        halves = []
        for h in range(2):
            cr = a_scr[0:n_k, 0, bo + h].reshape(kr, dt)
            ci = a_scr[0:n_k, 1, bo + h].reshape(kr, dt)
            c, s = twc_ref[:, bo + h:bo + h + 1], tws_ref[:, bo + h:bo + h + 1]
            halves.append(jnp.concatenate([cr * c - ci * s, cr * s + ci * c], axis=0))
        conv = _dot(kgi_ref[...], side_by_side(*halves).astype(BF16))
        for h in range(2):
            rows = pl.ds(bo + h, n_a, stride=n_bo)
            zin = z_ref[rows].reshape(n_a * OCT, dt)
            out = gate_ref[rows].reshape(n_a * OCT, dt) * (conv[:, h * dt:(h + 1) * dt] + zin * d_ref[...])
            o_ref[rows] = out.reshape(n_a, OCT, dt)


def _hyena_conv(zin, zin_col0, gate, gate_col0, spec, bias_d, consts, *, seq_len, n_seqs, row0, out_rows, prev_out,
                order=0):
    d = spec.shape[-1]
    dt = LANES
    n_k = spec.shape[1]
    seq_blk = seq_len // OCT
    blk0 = row0 // seq_len
    as_oct = lambda arr: arr.reshape(arr.shape[0] // OCT, OCT, arr.shape[1])
    const_spec = lambda arr: pl.BlockSpec(arr.shape, lambda j, b: (0, 0))
    in_specs = [pl.BlockSpec((seq_blk, OCT, dt), lambda j, b: (blk0 + b, 0, zin_col0 // dt + j)),
                pl.BlockSpec((seq_blk, OCT, dt), lambda j, b: (blk0 + b, 0, gate_col0 // dt + j)),
                pl.BlockSpec((None, n_k, 2, MINOR, dt), lambda j, b: (order, 0, 0, 0, j)),
                pl.BlockSpec((None, 1, dt), lambda j, b: (order, 0, j))] + [const_spec(cst) for cst in consts]
    args = [as_oct(zin), as_oct(gate), spec, bias_d.reshape(bias_d.shape[0], 1, d), *consts]
    aliases = {}
    if prev_out is not None:
        in_specs.append(pl.BlockSpec(memory_space=pl.ANY))
        aliases = {len(args): 0}
        args.append(as_oct(prev_out))
    out = pl.pallas_call(
        _hyena_kernel,
        out_shape=jax.ShapeDtypeStruct((out_rows // OCT, OCT, d), F32),
        grid=(d // dt, n_seqs),
        in_specs=in_specs,
        out_specs=pl.BlockSpec((seq_blk, OCT, dt), lambda j, b: (blk0 + b, 0, j)),
        scratch_shapes=[pltpu.VMEM((n_k, 2, MINOR // OCT, OCT, dt), F32)],
        input_output_aliases=aliases,
        compiler_params=_params("arbitrary", "arbitrary"),
        name="hyena_conv",
    )(*args)
    return out.reshape(out_rows, d)


def _hyena_mixer(uc, bias_d, filt, *, n_seq, n_batch, n_ctx, with_ctx):
    d = uc.shape[1] // 3
    rows = uc.shape[0]
    groups = [(n_seq, n_batch, 0)] + ([(n_ctx, n_batch, n_seq * n_batch)] if with_ctx else [])
    consts = [_dft_constants(L) for L, _, _ in groups]
    specs = [_hyena_spectrum(L, cst, *filt, d) for (L, _, _), cst in zip(groups, consts)]
    z = None
    for o in range(HY_ORDER):
        zin, zin_col0 = (uc, 2 * d) if o == 0 else (z, 0)
        out = None
        for (L, n_seqs, row0), spec, cst in zip(groups, specs, consts):
            out = _hyena_conv(zin, zin_col0, uc, o * d, spec, bias_d, cst, seq_len=L, n_seqs=n_seqs, row0=row0,
                              out_rows=rows, prev_out=out, order=o)
        z = out
    return z


def _inproj_kernel(x_ref, xp_ref, xn_ref, g_ref, mod_ref, w_ref, b_ref, cw_ref, cb_ref, o_ref,
                   *, tn, n_seq, n_ctx, lat_tiles):
    i = pl.program_id(0)
    tm = x_ref.shape[0]
    x = jnp.concatenate([xp_ref[...], x_ref[...], xn_ref[...]], axis=0)
    h = _modulated(x, g_ref[...], mod_ref, 0).astype(BF16)
    row = lax.broadcasted_iota(jnp.int32, (tm, 1), 0)
    pos = jnp.where(i < lat_tiles, (i * tm) % n_seq + row, row & (n_ctx - 1))
    last = jnp.where(i < lat_tiles, n_seq - 1, n_ctx - 1)
    has_prev = pos != 0
    has_next = pos != last
    for n0 in range(0, o_ref.shape[1], tn):
        cols = slice(n0, n0 + tn)
        u = _dot(h, w_ref[:, cols]) + b_ref[:, cols]
        prev = jnp.where(has_prev, u[OCT - 1:OCT - 1 + tm], 0.0)
        nxt = jnp.where(has_next, u[OCT + 1:OCT + 1 + tm], 0.0)
        o_ref[:, cols] = (cb_ref[:, cols] + prev * cw_ref[0:1, cols] + u[OCT:OCT + tm] * cw_ref[1:2, cols]
                          + nxt * cw_ref[2:3, cols])


def _inproj(xs, gain, mod, w, bias, conv_w, conv_b, *, n_rows, n_seq, n_batch, n_ctx):
    d = xs.shape[1]
    n_out = w.shape[1]
    tm, tn = TOKEN_TILE, 512
    assert n_seq % tm == 0 and tm % n_ctx == 0 and n_ctx & (n_ctx - 1) == 0
    halo = tm // OCT
    last_halo = xs.shape[0] // OCT - 1
    kern = functools.partial(_inproj_kernel, tn=tn, n_seq=n_seq, n_ctx=n_ctx, lat_tiles=n_seq * n_batch // tm)
    return pl.pallas_call(
        kern,
        out_shape=jax.ShapeDtypeStruct((n_rows, n_out), F32),
        grid=(n_rows // tm,),
        in_specs=[pl.BlockSpec((tm, d), lambda i: (i, 0)),
                  pl.BlockSpec((OCT, d), lambda i: (jnp.maximum(i * halo - 1, 0), 0)),
                  pl.BlockSpec((OCT, d), lambda i: (jnp.minimum((i + 1) * halo, last_halo), 0)),
                  pl.BlockSpec((1, d), lambda i: (0, 0)),
                  pl.BlockSpec((None, 2, d), _mod_index(tm, n_seq, n_batch)),
                  pl.BlockSpec((d, n_out), lambda i: (0, 0)),
                  pl.BlockSpec((1, n_out), lambda i: (0, 0)),
                  pl.BlockSpec((HY_SHORT, n_out), lambda i: (0, 0)),
                  pl.BlockSpec((1, n_out), lambda i: (0, 0))],
        out_specs=pl.BlockSpec((tm, n_out), lambda i: (i, 0)),
        compiler_params=_params("arbitrary"),
        name="hyena_inproj",
    )(xs, xs, xs, gain.reshape(1, d), mod, w, bias.reshape(1, n_out), conv_w, conv_b.reshape(1, n_out))


def _rope_tables(n_seq, tm):
    rows = n_seq // GRID_W
    row = jnp.repeat(jnp.arange(rows), GRID_W).astype(F32)
    col = jnp.tile(jnp.arange(GRID_W), rows).astype(F32)
    half = HEAD_DIM // 2
    n_freq = half // 2
    inv = ROPE_THETA ** (-jnp.arange(n_freq, dtype=F32) / n_freq)
    ang = jnp.concatenate([row[:, None] * inv, col[:, None] * inv], axis=-1)
    cos = jnp.tile(jnp.cos(ang), (1, LANES // half))
    sin = jnp.tile(jnp.concatenate([-jnp.sin(ang), jnp.sin(ang)], axis=-1), (1, LANES // HEAD_DIM))
    cos = jnp.concatenate([cos, jnp.ones((tm, LANES), F32)], axis=0)
    sin = jnp.concatenate([sin, jnp.zeros((tm, LANES), F32)], axis=0)
    return cos, sin


def _doubled(w, n_heads):
    lead = w.shape[:-1]
    w = w.reshape(lead + (n_heads, 1, HEAD_DIM))
    return jnp.broadcast_to(w, lead + (n_heads, 2, HEAD_DIM)).reshape(lead + (n_heads * 2 * HEAD_DIM,))


def kernel(x, c, ctx, c_ctx, ada_w, ada_b, norm_mix, norm_ffn, attn_w_qkv, attn_b_qkv, attn_w_o, attn_b_o, attn_sinks, hy_w_in, hy_b_in, hy_conv_w, hy_conv_b, hy_f_w1, hy_f_b1, hy_f_w2, hy_f_b2, hy_f_w3, hy_f_b3, hy_f_freq, hy_f_wout, hy_bias_d, hy_w_o, hy_b_o, moe_router_w, moe_router_b, moe_w_gu, moe_b_gu, moe_w_down, moe_b_down, final_norm):
    B, N, D = x.shape
    C = ctx.shape[1]
    T, TC = B * N, B * C
    q_dim = N_HEADS * HEAD_DIM
    kv_dim = N_KV_HEADS * HEAD_DIM
    dims = dict(n_seq=N, n_batch=B)

    c_all = jnp.concatenate([c, c_ctx[None, :], jnp.zeros((16 - B - 1, D), F32)], axis=0)
    mod_all = _ada_table(c_all, ada_w, ada_b)[:, :B + 1].reshape(DEPTH, B + 1, 6, D)
    cos, sin = _rope_tables(N, TOKEN_TILE)

    xs = jnp.concatenate([x.reshape(T, D), ctx.reshape(TC, D)], axis=0)
    for i in range(DEPTH):
        kind, j = i % N_MIXERS, i // N_MIXERS
        update_ctx = any(l % N_MIXERS == 0 for l in range(i + 1, DEPTH))
        need_ctx = update_ctx or kind == 0
        mod = mod_all[i]
        n_in = T + TC if need_ctx else T
        n_out = T + TC if update_ctx else T
        if kind == 0:
            wq, wk, wv = (attn_w_qkv[j][:, :q_dim], attn_w_qkv[j][:, q_dim:q_dim + kv_dim],
                          attn_w_qkv[j][:, q_dim + kv_dim:])
            bq, bk, bv = (attn_b_qkv[j][:q_dim], attn_b_qkv[j][q_dim:q_dim + kv_dim],
                          attn_b_qkv[j][q_dim + kv_dim:])
            w = jnp.concatenate([wq, _doubled(wk, N_KV_HEADS), _doubled(wv, N_KV_HEADS)], axis=1).astype(BF16)
            b = jnp.concatenate([bq, _doubled(bk, N_KV_HEADS), _doubled(bv, N_KV_HEADS)])
            qkv = _norm_matmul(xs, norm_mix[i], mod[:, 0:2], w, b, n_rows=n_in, out_dtype=BF16,
                               rope=(cos, sin, q_dim + 2 * kv_dim, q_dim), **dims)
            o = _attention(qkv, attn_sinks[j], n_ctx=C, ctx_out=update_ctx, **dims)
            xs = _matmul_residual(o, attn_w_o[j].astype(BF16), attn_b_o[j], xs, mod[:, 2:3], n_rows=n_out, **dims)
        else:
            uc = _inproj(xs, norm_mix[i], mod[:, 0:2], hy_w_in[j].astype(BF16), hy_b_in[j], hy_conv_w[j],
                         hy_conv_b[j], n_rows=n_in, n_ctx=C, **dims)
            filt = (hy_f_w1[j], hy_f_b1[j], hy_f_w2[j], hy_f_b2[j], hy_f_w3[j], hy_f_b3[j], hy_f_freq[j],
                    hy_f_wout[j])
            z = _hyena_mixer(uc, hy_bias_d[j], filt, n_ctx=C, with_ctx=update_ctx, **dims)
            xs = _matmul_residual(z, hy_w_o[j].astype(BF16), hy_b_o[j], xs, mod[:, 2:3], n_rows=n_out, **dims)
        xs = _moe_layer(xs, i, mod, norm_ffn[i], moe_router_w[i], moe_router_b[i], moe_w_gu, moe_b_gu,
                        moe_w_down, moe_b_down, final_norm, n_rows=n_out, final=(i == DEPTH - 1), **dims)
    return xs.reshape(B, N, D)
```

```python
import functools
import math

import jax
import jax.numpy as jnp
import numpy as np
from jax import lax
from jax.experimental import pallas as pl
from jax.experimental.pallas import tpu as pltpu

DEPTH = 4
N_MIXERS = 2
GRID_W = 64
N_HEADS = 16
N_KV_HEADS = 4
HEAD_DIM = 64
GROUP = N_HEADS // N_KV_HEADS
WINDOW = 128
ROPE_THETA = 10000.0
HY_ORDER = 2
HY_SHORT = 3
HY_BANDS = 16
HY_TARGET = 1e-2
HY_FAST_DECAY = 0.3
HY_SLOW_DECAY = 1.5
N_EXPERTS = 32
TOP_K = 4
SWIGLU_LIMIT = 7.0
SWIGLU_ALPHA = 1.702
NORM_EPS = 1e-6

LANES = 128
OCT = 8
MINOR = 128
TOKEN_TILE = 512
Q_TILE = 128
MOE_TILE = 512
VMEM_LIMIT = 56 * 1024 * 1024

F32 = jnp.float32
BF16 = jnp.bfloat16


def _dot(a, b):
    return jnp.dot(a, b, preferred_element_type=F32)


def _params(*sem):
    return pltpu.CompilerParams(dimension_semantics=sem, vmem_limit_bytes=VMEM_LIMIT)


def _ada_kernel(c_ref, w_ref, b_ref, o_ref):
    c = c_ref[...]
    s = c * (1.0 / (1.0 + jnp.exp(-c)))
    o_ref[...] = jnp.dot(s, w_ref[...], preferred_element_type=F32,
                         precision=lax.Precision.HIGHEST) + b_ref[...]


def _ada_table(c_all, ada_w, ada_b):
    r, d = c_all.shape
    depth, _, n6 = ada_w.shape
    tn = n6 // 4
    return pl.pallas_call(
        _ada_kernel,
        out_shape=jax.ShapeDtypeStruct((depth, r, n6), F32),
        grid=(depth, n6 // tn),
        in_specs=[pl.BlockSpec((r, d), lambda l, j: (0, 0)),
                  pl.BlockSpec((None, d, tn), lambda l, j: (l, 0, j)),
                  pl.BlockSpec((None, 1, tn), lambda l, j: (l, 0, j))],
        out_specs=pl.BlockSpec((None, r, tn), lambda l, j: (l, 0, j)),
        compiler_params=_params("arbitrary", "arbitrary"),
        name="ada_table",
    )(c_all, ada_w, ada_b.reshape(depth, 1, n6))


def _modulated(x, g, mod_ref, row):
    y = x * lax.rsqrt(jnp.mean(x * x, axis=-1, keepdims=True) + NORM_EPS) * g
    return y * (1.0 + mod_ref[row + 1:row + 2, :]) + mod_ref[row:row + 1, :]


def _mod_index(tm, n_seq, n_batch):
    return lambda i: (jnp.minimum(i * tm // n_seq, n_batch), 0, 0)


def _nm_kernel(x_ref, g_ref, mod_ref, w_ref, b_ref, cos_ref, sin_ref, o_ref, *, tn, rope_cols, q_cols):
    h = _modulated(x_ref[...], g_ref[...], mod_ref, 0).astype(BF16)
    n_out = o_ref.shape[1]
    if rope_cols:
        reps = tn // LANES
        cos = jnp.tile(cos_ref[...], (1, reps))
        sin = jnp.tile(sin_ref[...], (1, reps))
        lane = lax.broadcasted_iota(jnp.int32, (1, tn), 1)
        first_half = (lane & (HEAD_DIM - 1)) < (HEAD_DIM // 2)
    for n0 in range(0, n_out, tn):
        acc = _dot(h, w_ref[:, n0:n0 + tn]) + b_ref[:, n0:n0 + tn]
        if n0 < rope_cols:
            partner = jnp.where(first_half, pltpu.roll(acc, tn - HEAD_DIM // 2, 1),
                                pltpu.roll(acc, HEAD_DIM // 2, 1))
            acc = acc * cos + partner * sin
            if n0 < q_cols:
                acc = acc * (HEAD_DIM ** -0.5)
        o_ref[:, n0:n0 + tn] = acc.astype(o_ref.dtype)


def _norm_matmul(xs, gain, mod, w, bias, *, n_rows, n_seq, n_batch, out_dtype, rope=None):
    d = xs.shape[1]
    n_out = w.shape[1]
    tm, tn = TOKEN_TILE, 512
    if rope is None:
        cos = jnp.zeros((tm, LANES), F32)
        sin = cos
        rope_cols = q_cols = 0
        rope_index = lambda i: (0, 0)
    else:
        cos, sin, rope_cols, q_cols = rope
        seq_tiles = n_seq // tm
        lat_tiles = n_seq * n_batch // tm
        rope_index = lambda i: (jnp.where(i < lat_tiles, i % seq_tiles, seq_tiles), 0)
    kern = functools.partial(_nm_kernel, tn=tn, rope_cols=rope_cols, q_cols=q_cols)
    return pl.pallas_call(
        kern,
        out_shape=jax.ShapeDtypeStruct((n_rows, n_out), out_dtype),
        grid=(n_rows // tm,),
        in_specs=[pl.BlockSpec((tm, d), lambda i: (i, 0)),
                  pl.BlockSpec((1, d), lambda i: (0, 0)),
                  pl.BlockSpec((None, 2, d), _mod_index(tm, n_seq, n_batch)),
                  pl.BlockSpec((d, n_out), lambda i: (0, 0)),
                  pl.BlockSpec((1, n_out), lambda i: (0, 0)),
                  pl.BlockSpec((tm, LANES), rope_index),
                  pl.BlockSpec((tm, LANES), rope_index)],
        out_specs=pl.BlockSpec((tm, n_out), lambda i: (i, 0)),
        compiler_params=_params("arbitrary"),
        name="norm_matmul",
    )(xs, gain.reshape(1, d), mod, w, bias.reshape(1, n_out), cos, sin)


def _res_kernel(a_ref, w_ref, b_ref, x_ref, gate_ref, o_ref):
    y = _dot(a_ref[...].astype(BF16), w_ref[...]) + b_ref[...]
    o_ref[...] = x_ref[...] + gate_ref[...] * y


def _matmul_residual(a, w, bias, xs, gate, *, n_rows, n_seq, n_batch):
    k = a.shape[1]
    d = w.shape[1]
    tm = TOKEN_TILE
    return pl.pallas_call(
        _res_kernel,
        out_shape=jax.ShapeDtypeStruct((n_rows, d), F32),
        grid=(n_rows // tm,),
        in_specs=[pl.BlockSpec((tm, k), lambda i: (i, 0)),
                  pl.BlockSpec((k, d), lambda i: (0, 0)),
                  pl.BlockSpec((1, d), lambda i: (0, 0)),
                  pl.BlockSpec((tm, d), lambda i: (i, 0)),
                  pl.BlockSpec((None, 1, d), _mod_index(tm, n_seq, n_batch))],
        out_specs=pl.BlockSpec((tm, d), lambda i: (i, 0)),
        compiler_params=_params("arbitrary"),
        name="matmul_residual",
    )(a, w, bias.reshape(1, d), xs, gate)


def _attend_pairs(q_ref, k_all, v_all, sink_ref, o_ref, kvh, valid):
    tq = q_ref.shape[0]
    lane = lax.broadcasted_iota(jnp.int32, (tq, LANES), 1)
    low = lane < HEAD_DIM
    row = lax.broadcasted_iota(jnp.int32, (2 * tq, 1), 0)
    for j in range(GROUP // 2):
        c0 = kvh * GROUP * HEAD_DIM + j * LANES
        qp = q_ref[:, c0:c0 + LANES]
        zero = jnp.zeros_like(qp)
        q2 = jnp.concatenate([jnp.where(low, qp, zero), jnp.where(low, zero, qp)], axis=0)
        s = lax.dot_general(q2, k_all, (((1,), (1,)), ((), ())), preferred_element_type=F32)
        if valid is not None:
            n_loc = valid.shape[1]
            s = jnp.concatenate([jnp.where(valid, s[:, :n_loc], -jnp.inf), s[:, n_loc:]], axis=1)
        head = kvh * GROUP + 2 * j
        sink = jnp.where(row < tq, sink_ref[head], sink_ref[head + 1])
        m = jnp.maximum(jnp.max(s, axis=-1, keepdims=True), sink)
        p = jnp.exp(s - m)
        denom = jnp.sum(p, axis=-1, keepdims=True) + jnp.exp(sink - m)
        o2 = _dot(p.astype(BF16), v_all) / denom
        o_ref[:, c0:c0 + LANES] = jnp.where(low, o2[:tq], o2[tq:]).astype(o_ref.dtype)


def _attn_kernel(sink_ref, q_ref, kp_ref, kc_ref, kn_ref, vp_ref, vc_ref, vn_ref, kx_ref, vx_ref, o_ref,
                 *, n_q_blocks):
    qi = pl.program_id(1)
    tq = q_ref.shape[0]

    @pl.when(qi < n_q_blocks)
    def _():
        r = lax.broadcasted_iota(jnp.int32, (tq, 3 * tq), 0)
        c = lax.broadcasted_iota(jnp.int32, (tq, 3 * tq), 1)
        kpos = c - tq + qi * tq
        band = (jnp.abs(c - tq - r) <= WINDOW) & (kpos >= 0) & (kpos < n_q_blocks * tq)
        valid = jnp.concatenate([band, band], axis=0)
        for kvh in range(N_KV_HEADS):
            cs = slice(kvh * LANES, (kvh + 1) * LANES)
            k_all = jnp.concatenate([kp_ref[:, cs], kc_ref[:, cs], kn_ref[:, cs], kx_ref[:, cs]], axis=0)
            v_all = jnp.concatenate([vp_ref[:, cs], vc_ref[:, cs], vn_ref[:, cs], vx_ref[:, cs]], axis=0)
            _attend_pairs(q_ref, k_all, v_all, sink_ref, o_ref, kvh, valid)

    @pl.when(qi >= n_q_blocks)
    def _():
        for kvh in range(N_KV_HEADS):
            cs = slice(kvh * LANES, (kvh + 1) * LANES)
            _attend_pairs(q_ref, kx_ref[:, cs], vx_ref[:, cs], sink_ref, o_ref, kvh, None)


def _attention(qkv, sinks, *, n_seq, n_batch, n_ctx, ctx_out):
    tq = Q_TILE
    nq = n_seq // tq
    ncq = n_ctx // tq if ctx_out else 0
    lat_blocks = n_batch * nq
    q_dim = N_HEADS * HEAD_DIM
    kv_w = N_KV_HEADS * LANES
    k_col = q_dim // kv_w
    v_col = k_col + 1
    ctx_row0 = n_batch * n_seq // n_ctx

    def q_index(b, i, s):
        return (jnp.where(i < nq, b * nq + i, lat_blocks + b * (n_ctx // tq) + (i - nq)), 0)

    def kv_index(off, col):
        def index(b, i, s):
            return (b * nq + jnp.clip(i + off, 0, nq - 1), col)
        return index

    n_rows = n_batch * n_seq + (n_batch * n_ctx if ctx_out else 0)
    grid_spec = pltpu.PrefetchScalarGridSpec(
        num_scalar_prefetch=1,
        grid=(n_batch, nq + ncq),
        in_specs=[pl.BlockSpec((tq, q_dim), q_index),
                  pl.BlockSpec((tq, kv_w), kv_index(-1, k_col)),
                  pl.BlockSpec((tq, kv_w), kv_index(0, k_col)),
                  pl.BlockSpec((tq, kv_w), kv_index(1, k_col)),
                  pl.BlockSpec((tq, kv_w), kv_index(-1, v_col)),
                  pl.BlockSpec((tq, kv_w), kv_index(0, v_col)),
                  pl.BlockSpec((tq, kv_w), kv_index(1, v_col)),
                  pl.BlockSpec((n_ctx, kv_w), lambda b, i, s: (ctx_row0 + b, k_col)),
                  pl.BlockSpec((n_ctx, kv_w), lambda b, i, s: (ctx_row0 + b, v_col))],
        out_specs=pl.BlockSpec((tq, q_dim), q_index),
    )
    return pl.pallas_call(
        functools.partial(_attn_kernel, n_q_blocks=nq),
        out_shape=jax.ShapeDtypeStruct((n_rows, q_dim), BF16),
        grid_spec=grid_spec,
        compiler_params=_params("arbitrary", "arbitrary"),
        name="attention",
    )(sinks, qkv, qkv, qkv, qkv, qkv, qkv, qkv, qkv, qkv)


def _pack_pairs(x):
    half = x.shape[1] // 2
    bits = lax.bitcast_convert_type(x.astype(BF16).astype(F32), jnp.int32)
    return lax.shift_right_logical(bits[:, :half], 16) | (bits[:, half:] & -65536)


def _unpack_pairs(u):
    return (lax.bitcast_convert_type(lax.shift_left(u, 16), F32),
            lax.bitcast_convert_type(u & -65536, F32))


def _router_kernel(x_ref, g_ref, mod_ref, w_ref, b_ref, tri_ref, tok_ref, route_ref, count_ref, seen):
    @pl.when(pl.program_id(0) == 0)
    def _():
        seen[...] = jnp.zeros_like(seen)

    h = _modulated(x_ref[...], g_ref[...], mod_ref, 0)
    tok_ref[...] = _pack_pairs(h)
    logits = jnp.dot(h, w_ref[...], preferred_element_type=F32, precision=lax.Precision.HIGHEST) + b_ref[...]
    lane = lax.broadcasted_iota(jnp.int32, logits.shape, 1)
    rest = jnp.where(lane < N_EXPERTS, logits, -jnp.inf)
    chosen = jnp.zeros(logits.shape, F32)
    top_v, top_i = [], []
    for _ in range(TOP_K):
        best = jnp.max(rest, axis=-1, keepdims=True)
        idx = jnp.min(jnp.where(rest == best, lane, LANES), axis=-1, keepdims=True)
        hit = lane == idx
        top_v.append(best)
        top_i.append(idx)
        rest = jnp.where(hit, -jnp.inf, rest)
        chosen = jnp.where(hit, 1.0, chosen)
    weights = [jnp.exp(v - top_v[0]) for v in top_v]
    denom = (weights[0] + weights[1]) + (weights[2] + weights[3])
    earlier = _dot(tri_ref[...], chosen.astype(BF16)) + seen[...]
    seen[...] = seen[...] + jnp.sum(chosen, axis=0, keepdims=True)
    route = jnp.zeros(logits.shape, F32)
    for k in range(TOP_K):
        rank = jnp.sum(jnp.where(lane == top_i[k], earlier, 0.0), axis=-1, keepdims=True)
        route = jnp.where(lane == k, top_i[k].astype(F32), route)
        route = jnp.where(lane == TOP_K + k, weights[k] / denom, route)
        route = jnp.where(lane == 2 * TOP_K + k, rank, route)
    route_ref[...] = route
    count_ref[...] = jnp.broadcast_to(seen[...], count_ref.shape)


def _router(xs, gain, mod, w, bias, *, n_rows, n_seq, n_batch):
    d = xs.shape[1]
    tm = TOKEN_TILE
    tri = jnp.asarray(np.tril(np.ones((tm, tm)), -1), BF16)
    return pl.pallas_call(
        _router_kernel,
        out_shape=(jax.ShapeDtypeStruct((n_rows, d // 2), jnp.int32), jax.ShapeDtypeStruct((n_rows, LANES), F32),
                   jax.ShapeDtypeStruct((OCT, LANES), F32)),
        grid=(n_rows // tm,),
        in_specs=[pl.BlockSpec((tm, d), lambda i: (i, 0)),
                  pl.BlockSpec((1, d), lambda i: (0, 0)),
                  pl.BlockSpec((None, 2, d), _mod_index(tm, n_seq, n_batch)),
                  pl.BlockSpec((d, LANES), lambda i: (0, 0)),
                  pl.BlockSpec((1, LANES), lambda i: (0, 0)),
                  pl.BlockSpec((tm, tm), lambda i: (0, 0))],
        out_specs=(pl.BlockSpec((tm, d // 2), lambda i: (i, 0)), pl.BlockSpec((tm, LANES), lambda i: (i, 0)),
                   pl.BlockSpec((OCT, LANES), lambda i: (0, 0))),
        scratch_shapes=[pltpu.VMEM((1, LANES), F32)],
        compiler_params=_params("arbitrary"),
        name="router",
    )(xs, gain.reshape(1, d), mod, w, bias, tri)


def _expert_kernel(be_ref, nb_ref, x_ref, wgu_ref, bgu_ref, wd_ref, bd_ref, o_ref, wgu_bf, wd_bf):
    i = pl.program_id(0)
    fresh = jnp.logical_or(i == 0, be_ref[i] != be_ref[jnp.maximum(i - 1, 0)])

    @pl.when(jnp.logical_and(fresh, i < nb_ref[0]))
    def _():
        wgu_bf[...] = wgu_ref[...].astype(BF16)
        wd_bf[...] = wd_ref[...].astype(BF16)

    @pl.when(i < nb_ref[0])
    def _():
        d_e = wd_ref.shape[0]
        x = jnp.concatenate(_unpack_pairs(x_ref[...]), axis=1).astype(BF16)
        g = jnp.minimum(_dot(x, wgu_bf[:, :d_e]) + bgu_ref[:, :d_e], SWIGLU_LIMIT)
        up = jnp.clip(_dot(x, wgu_bf[:, d_e:]) + bgu_ref[:, d_e:], -SWIGLU_LIMIT, SWIGLU_LIMIT)
        act = g * (1.0 / (1.0 + jnp.exp(-SWIGLU_ALPHA * g))) * (up + 1.0)
        o_ref[...] = _pack_pairs(_dot(act.astype(BF16), wd_bf[...]) + bd_ref[...])

    @pl.when(i >= nb_ref[0])
    def _():
        o_ref[...] = jnp.zeros_like(o_ref)


def _experts(xs, block_e, n_used, w_gu, b_gu, w_down, b_down, layer):
    r = xs.shape[0]
    n_e, d, d_gu = w_gu.shape[1:]
    tm = MOE_TILE
    grid_spec = pltpu.PrefetchScalarGridSpec(
        num_scalar_prefetch=2,
        grid=(r // tm,),
        in_specs=[pl.BlockSpec((tm, d // 2), lambda i, be, nb: (i, 0)),
                  pl.BlockSpec((None, None, d, d_gu), lambda i, be, nb: (layer, be[i], 0, 0)),
                  pl.BlockSpec((None, None, 1, d_gu), lambda i, be, nb: (layer, be[i], 0, 0)),
                  pl.BlockSpec((None, None, d_gu // 2, d), lambda i, be, nb: (layer, be[i], 0, 0)),
                  pl.BlockSpec((None, None, 1, d), lambda i, be, nb: (layer, be[i], 0, 0))],
        out_specs=pl.BlockSpec((tm, d // 2), lambda i, be, nb: (i, 0)),
        scratch_shapes=[pltpu.VMEM((d, d_gu), BF16), pltpu.VMEM((d_gu // 2, d), BF16)],
    )
    return pl.pallas_call(
        _expert_kernel,
        out_shape=jax.ShapeDtypeStruct((r, d // 2), jnp.int32),
        grid_spec=grid_spec,
        compiler_params=_params("arbitrary"),
        name="experts",
    )(block_e, n_used, xs, w_gu, b_gu.reshape(b_gu.shape[0], n_e, 1, d_gu), w_down,
      b_down.reshape(b_down.shape[0], n_e, 1, d))


def _combine_kernel(x_ref, y0_ref, y1_ref, y2_ref, y3_ref, route_ref, gate_ref, g_ref, o_ref, *, final):
    w = [route_ref[:, TOP_K + k:TOP_K + k + 1] for k in range(TOP_K)]
    y = [_unpack_pairs(r[...]) for r in (y0_ref, y1_ref, y2_ref, y3_ref)]
    f = jnp.concatenate([(w[0] * y[0][h] + w[1] * y[1][h]) + (w[2] * y[2][h] + w[3] * y[3][h]) for h in range(2)],
                        axis=1)
    x = x_ref[...] + gate_ref[...] * f
    if final:
        x = x * lax.rsqrt(jnp.mean(x * x, axis=-1, keepdims=True) + NORM_EPS) * g_ref[...]
    o_ref[...] = x


def _combine(xs, ys, route, gate, final_gain, *, n_rows, n_seq, n_batch, final):
    d = xs.shape[1]
    tm = TOKEN_TILE
    row_spec = pl.BlockSpec((tm, d), lambda i: (i, 0))
    return pl.pallas_call(
        functools.partial(_combine_kernel, final=final),
        out_shape=jax.ShapeDtypeStruct((n_rows, d), F32),
        grid=(n_rows // tm,),
        in_specs=[row_spec] + [pl.BlockSpec((tm, d // 2), lambda i: (i, 0))] * TOP_K
                 + [pl.BlockSpec((tm, LANES), lambda i: (i, 0)),
                                             pl.BlockSpec((None, 1, d), _mod_index(tm, n_seq, n_batch)),
                                             pl.BlockSpec((1, d), lambda i: (0, 0))],
        out_specs=row_spec,
        compiler_params=_params("arbitrary"),
        name="combine",
    )(xs, *ys, route, gate, final_gain.reshape(1, d))


def _moe_layer(xs, layer, mod, norm_g, router_w, router_b, w_gu, b_gu, w_down, b_down, final_gain,
               *, n_rows, n_seq, n_batch, final):
    d = xs.shape[1]
    rw = jnp.zeros((d, LANES), F32).at[:, :N_EXPERTS].set(router_w)
    rb = jnp.zeros((1, LANES), F32).at[0, :N_EXPERTS].set(router_b)
    tok, route, count = _router(xs, norm_g, mod[:, 3:5], rw, rb, n_rows=n_rows, n_seq=n_seq, n_batch=n_batch)
    top_i = route[:, :TOP_K].astype(jnp.int32)
    rank = route[:, 2 * TOP_K:3 * TOP_K].astype(jnp.int32)
    counts = count[0, :N_EXPERTS].astype(jnp.int32)
    n_assign = n_rows * TOP_K
    padded = (counts + MOE_TILE - 1) // MOE_TILE * MOE_TILE
    pad_end = jnp.cumsum(padded)
    pad_start = pad_end - padded
    start = jnp.cumsum(counts) - counts
    n_blocks = -(-n_assign // MOE_TILE) + N_EXPERTS
    n_slots = n_blocks * MOE_TILE
    block_e = jnp.minimum(jnp.sum(pad_end[None, :] <= (jnp.arange(n_blocks) * MOE_TILE)[:, None], axis=1),
                          N_EXPERTS - 1).astype(jnp.int32)
    n_used = (pad_end[-1] // MOE_TILE).astype(jnp.int32).reshape(1)
    slot_of = pad_start[top_i] + rank
    by_slot = jnp.argsort(slot_of.reshape(-1)).astype(jnp.int32)
    slot_e = jnp.repeat(block_e, MOE_TILE)
    within = jnp.arange(n_slots, dtype=jnp.int32) - pad_start[slot_e]
    src = by_slot[jnp.clip(start[slot_e] + within, 0, n_assign - 1)]
    ys = _experts(tok[src // TOP_K], block_e, n_used, w_gu, b_gu, w_down, b_down, layer)
    parts = [ys[slot_of[:, k]] for k in range(TOP_K)]
    return _combine(xs, parts, route, mod[:, 5:6], final_gain, n_rows=n_rows, n_seq=n_seq, n_batch=n_batch,
                    final=final)


def _hyena_spectrum(L, consts, w1, b1, w2, b2, w3, b3, freq, w_out, d):
    t = jnp.linspace(0.0, 1.0, L, dtype=F32)[:, None]
    w = 2.0 * math.pi * jnp.arange(L, dtype=F32)[:, None] / L
    bands = jnp.linspace(1e-4, HY_BANDS - 1, HY_BANDS, dtype=F32)[None, :]
    z = jnp.concatenate([t, jnp.cos(bands * w), -jnp.sin(bands * w)], axis=-1)
    a = jnp.sin(freq * (z @ w1 + b1))
    a = jnp.sin(freq * (a @ w2 + b2))
    a = jnp.sin(freq * (a @ w3 + b3))
    hf = (a @ w_out).reshape(L, 2, HY_ORDER, d)
    deltas = jnp.linspace(math.log(HY_TARGET) / HY_SLOW_DECAY, math.log(HY_TARGET) / HY_FAST_DECAY, d, dtype=F32)
    hf = hf * jnp.exp(-t * jnp.abs(deltas))[:, None, None, :]
    return _filter_spectrum(hf.reshape(L // OCT, OCT, 2 * HY_ORDER * d), consts, d)


def _filter_kernel(f_ref, b_ref, kgf_ref, fm_ref, twc_ref, tws_ref, spec_ref, a_scr):
    n_k = kgf_ref.shape[0] // (2 * OCT)
    n_a = n_k - 1
    n_bo = MINOR // OCT
    dt = f_ref.shape[2]
    kc = _major_chunk(n_k)
    cr_rows = kc * OCT
    total = jnp.zeros((1, 2 * dt), F32)
    for bo in range(n_bo):
        rows = pl.ds(bo, n_a, stride=n_bo)
        x2 = jnp.concatenate([f_ref[rows].reshape(n_a * OCT, dt), b_ref[rows].reshape(n_a * OCT, dt)], axis=1)
        total = total + jnp.sum(jnp.abs(x2), axis=0, keepdims=True)
        p = _dot(kgf_ref[...], x2.astype(BF16))
        for ch in range(n_k // kc):
            r0 = 2 * ch * cr_rows
            ks = slice(ch * kc, (ch + 1) * kc)
            tw_rows = slice(ch * cr_rows, (ch + 1) * cr_rows)
            ar, ai = p[r0:r0 + cr_rows], p[r0 + cr_rows:r0 + 2 * cr_rows]
            c, s = twc_ref[tw_rows, bo:bo + 1], tws_ref[tw_rows, bo:bo + 1]
            a_scr[ks, 0, bo] = (ar * c + ai * s).reshape(kc, OCT, 2 * dt)
            a_scr[ks, 1, bo] = (ai * c - ar * s).reshape(kc, OCT, 2 * dt)
    b0 = b_ref[0, 0:1, :]
    inv = 1.0 / (total[:, :dt] + total[:, dt:] - jnp.abs(b0))
    unroll = max(u for u in range(1, 12) if n_k % u == 0)

    def minor_stage(step, carry):
        for j in range(unroll):
            k = step * unroll + j
            x = _dot(fm_ref[...], a_scr[k].reshape(2 * MINOR, 2 * dt).astype(BF16))
            spec_ref[k, 0] = ((x[:MINOR, :dt] + x[:MINOR, dt:] - b0) * inv).astype(spec_ref.dtype)
            spec_ref[k, 1] = ((x[MINOR:, :dt] - x[MINOR:, dt:]) * inv).astype(spec_ref.dtype)
        return carry

    lax.fori_loop(0, n_k // unroll, minor_stage, 0)
    for k in range(n_k, spec_ref.shape[0]):
        spec_ref[k] = jnp.zeros(spec_ref.shape[1:], spec_ref.dtype)


def _filter_spectrum(hf, consts, d):
    kgf, _, fm, _, twc, tws = consts
    dt = LANES
    seq_blk = hf.shape[0]
    n_k = kgf.shape[0] // (2 * OCT)
    n_kp = n_k + n_k % 2
    tiles = d // dt
    const_spec = lambda arr: pl.BlockSpec(arr.shape, lambda o, j: (0, 0))
    return pl.pallas_call(
        _filter_kernel,
        out_shape=jax.ShapeDtypeStruct((HY_ORDER, n_kp, 2, MINOR, d), BF16),
        grid=(HY_ORDER, tiles),
        in_specs=[pl.BlockSpec((seq_blk, OCT, dt), lambda o, j: (0, 0, o * tiles + j)),
                  pl.BlockSpec((seq_blk, OCT, dt), lambda o, j: (0, 0, (HY_ORDER + o) * tiles + j)),
                  const_spec(kgf), const_spec(fm), const_spec(twc), const_spec(tws)],
        out_specs=pl.BlockSpec((None, n_kp, 2, MINOR, dt), lambda o, j: (o, 0, 0, 0, j)),
        scratch_shapes=[pltpu.VMEM((n_k, 2, MINOR // OCT, OCT, 2 * dt), F32)],
        compiler_params=_params("arbitrary", "arbitrary"),
        name="hyena_filter",
    )(hf, hf, kgf, fm, twc, tws)


def _major_chunk(n_k):
    return max(c for c in range(1, 14) if n_k % c == 0)


def _dft_constants(L):
    n = 2 * L
    n1 = n // MINOR
    n_a, n_k = n1 // 2, n1 // 2 + 1
    k = np.arange(n_k)[:, None]
    a = np.arange(n_a)[None, :]
    th = 2.0 * np.pi * k * a / n1
    eye = np.eye(OCT)
    kgf = np.concatenate([np.kron(np.cos(th), eye), np.kron(-np.sin(th), eye)], axis=0)
    w = np.where((k == 0) | (k == n_a), 1.0, 2.0) / n
    kgi = np.concatenate([np.kron((np.cos(th) * w).T, eye), np.kron((-np.sin(th) * w).T, eye)], axis=1)
    b = np.arange(MINOR)
    ph = 2.0 * np.pi * np.outer(b, b) / MINOR
    fr, fi = np.cos(ph), -np.sin(ph)
    fm = np.block([[fr, -fi], [fi, fr]])
    fmi = np.block([[fr, fi], [-fi, fr]])
    bb = OCT * np.arange(MINOR // OCT)[None, None, :] + np.arange(OCT)[None, :, None]
    tw = 2.0 * np.pi * np.arange(n_k)[:, None, None] * bb / n
    pad = ((0, 0), (0, LANES - MINOR // OCT))
    twc = np.pad(np.cos(tw).reshape(n_k * OCT, -1), pad)
    tws = np.pad(np.sin(tw).reshape(n_k * OCT, -1), pad)
    kc = _major_chunk(n_k)
    kgf = kgf.reshape(2, n_k // kc, kc * OCT, -1).transpose(1, 0, 2, 3).reshape(2 * n_k * OCT, -1)
    kgi = kgi.reshape(-1, 2, n_k // kc, kc * OCT).transpose(0, 2, 1, 3).reshape(-1, 2 * n_k * OCT)
    return (jnp.asarray(kgf, BF16), jnp.asarray(kgi, BF16), jnp.asarray(fm, BF16), jnp.asarray(fmi, BF16),
            jnp.asarray(twc, F32), jnp.asarray(tws, F32))


def _hyena_kernel(z_ref, gate_ref, spec_ref, d_ref, kgf_ref, kgi_ref, fm_ref, fmi_ref, twc_ref, tws_ref, *rest):
    o_ref, a_scr = rest[-2], rest[-1]
    n_k = kgf_ref.shape[0] // (2 * OCT)
    n_a = n_k - 1
    n_bo = MINOR // OCT
    dt = z_ref.shape[2]
    kc = _major_chunk(n_k)
    cr_rows = kc * OCT
    side_by_side = lambda u, v: jnp.concatenate([u, v], axis=1)

    if a_scr.shape[0] > n_k:
        a_scr[n_k] = jnp.zeros(a_scr.shape[1:], F32)

    for bo in range(0, n_bo, 2):
        x2 = side_by_side(*[z_ref[pl.ds(bo + h, n_a, stride=n_bo)].reshape(n_a * OCT, dt) for h in range(2)])
        p = _dot(kgf_ref[...], x2.astype(BF16))
        for ch in range(n_k // kc):
            r0 = 2 * ch * cr_rows
            ks = slice(ch * kc, (ch + 1) * kc)
            tw_rows = slice(ch * cr_rows, (ch + 1) * cr_rows)
            for h in range(2):
                cols = slice(h * dt, (h + 1) * dt)
                ar, ai = p[r0:r0 + cr_rows, cols], p[r0 + cr_rows:r0 + 2 * cr_rows, cols]
                c, s = twc_ref[tw_rows, bo + h:bo + h + 1], tws_ref[tw_rows, bo + h:bo + h + 1]
                a_scr[ks, 0, bo + h] = (ar * c + ai * s).reshape(kc, OCT, dt)
                a_scr[ks, 1, bo + h] = (ai * c - ar * s).reshape(kc, OCT, dt)

    n_pairs = a_scr.shape[0] // 2
    unroll = max(u for u in range(1, 12) if n_pairs % u == 0)

    def minor_stage(step, carry):
        for j in range(unroll):
            k = 2 * (step * unroll + j)
            b2 = side_by_side(a_scr[k].reshape(2 * MINOR, dt), a_scr[k + 1].reshape(2 * MINOR, dt))
            x = _dot(fm_ref[...], b2.astype(BF16))
            xr, xi = x[:MINOR], x[MINOR:]
            gr = side_by_side(spec_ref[k, 0], spec_ref[k + 1, 0]).astype(F32)
            gi = side_by_side(spec_ref[k, 1], spec_ref[k + 1, 1]).astype(F32)
            y = jnp.concatenate([xr * gr - xi * gi, xr * gi + xi * gr], axis=0).astype(BF16)
            c2 = _dot(fmi_ref[...], y)
            a_scr[k] = c2[:, :dt].reshape(2, n_bo, OCT, dt)
            a_scr[k + 1] = c2[:, dt:].reshape(2, n_bo, OCT, dt)
        return carry

    lax.fori_loop(0, n_pairs // unroll, minor_stage, 0)

    for bo in range(0, n_bo, 2):
        halves = []
        for h in range(2):
            pieces = []
            for ch in range(n_k // kc):
                ks = slice(ch * kc, (ch + 1) * kc)
                tw_rows = slice(ch * cr_rows, (ch + 1) * cr_rows)
                cr = a_scr[ks, 0, bo + h].reshape(cr_rows, dt)
                ci = a_scr[ks, 1, bo + h].reshape(cr_rows, dt)
                c, s = twc_ref[tw_rows, bo + h:bo + h + 1], tws_ref[tw_rows, bo + h:bo + h + 1]
                pieces += [cr * c - ci * s, cr * s + ci * c]
            halves.append(jnp.concatenate(pieces, axis=0))
        conv = _dot(kgi_ref[...], side_by_side(*halves).astype(BF16))
        for h in range(2):
            rows = pl.ds(bo + h, n_a, stride=n_bo)
            zin = z_ref[rows].reshape(n_a * OCT, dt)
            out = gate_ref[rows].reshape(n_a * OCT, dt) * (conv[:, h * dt:(h + 1) * dt] + zin * d_ref[...])
            o_ref[rows] = out.reshape(n_a, OCT, dt)


def _hyena_conv(zin, zin_col0, gate, gate_col0, spec, bias_d, consts, *, seq_len, n_seqs, row0, out_rows, prev_out,
                order=0):
    d = spec.shape[-1]
    dt = LANES
    n_k = spec.shape[1]
    seq_blk = seq_len // OCT
    blk0 = row0 // seq_len
    as_oct = lambda arr: arr.reshape(arr.shape[0] // OCT, OCT, arr.shape[1])
    const_spec = lambda arr: pl.BlockSpec(arr.shape, lambda j, b: (0, 0))
    in_specs = [pl.BlockSpec((seq_blk, OCT, dt), lambda j, b: (blk0 + b, 0, zin_col0 // dt + j)),
                pl.BlockSpec((seq_blk, OCT, dt), lambda j, b: (blk0 + b, 0, gate_col0 // dt + j)),
                pl.BlockSpec((None, n_k, 2, MINOR, dt), lambda j, b: (order, 0, 0, 0, j)),
                pl.BlockSpec((None, 1, dt), lambda j, b: (order, 0, j))] + [const_spec(cst) for cst in consts]
    args = [as_oct(zin), as_oct(gate), spec, bias_d.reshape(bias_d.shape[0], 1, d), *consts]
    aliases = {}
    if prev_out is not None:
        in_specs.append(pl.BlockSpec(memory_space=pl.ANY))
        aliases = {len(args): 0}
        args.append(as_oct(prev_out))
    out = pl.pallas_call(
        _hyena_kernel,
        out_shape=jax.ShapeDtypeStruct((out_rows // OCT, OCT, d), F32),
        grid=(d // dt, n_seqs),
        in_specs=in_specs,
        out_specs=pl.BlockSpec((seq_blk, OCT, dt), lambda j, b: (blk0 + b, 0, j)),
        scratch_shapes=[pltpu.VMEM((n_k, 2, MINOR // OCT, OCT, dt), F32)],
        input_output_aliases=aliases,
        compiler_params=_params("arbitrary", "arbitrary"),
        name="hyena_conv",
    )(*args)
    return out.reshape(out_rows, d)


def _hyena_mixer(uc, bias_d, filt, *, n_seq, n_batch, n_ctx, with_ctx):
    d = uc.shape[1] // 3
    rows = uc.shape[0]
    groups = [(n_seq, n_batch, 0)] + ([(n_ctx, n_batch, n_seq * n_batch)] if with_ctx else [])
    consts = [_dft_constants(L) for L, _, _ in groups]
    specs = [_hyena_spectrum(L, cst, *filt, d) for (L, _, _), cst in zip(groups, consts)]
    z = None
    for o in range(HY_ORDER):
        zin, zin_col0 = (uc, 2 * d) if o == 0 else (z, 0)
        out = None
        for (L, n_seqs, row0), spec, cst in zip(groups, specs, consts):
            out = _hyena_conv(zin, zin_col0, uc, o * d, spec, bias_d, cst, seq_len=L, n_seqs=n_seqs, row0=row0,
                              out_rows=rows, prev_out=out, order=o)
        z = out
    return z


def _inproj_kernel(x_ref, xp_ref, xn_ref, g_ref, mod_ref, w_ref, b_ref, cw_ref, cb_ref, o_ref,
                   *, tn, n_seq, n_ctx, lat_tiles):
    i = pl.program_id(0)
    tm = x_ref.shape[0]
    x = jnp.concatenate([xp_ref[...], x_ref[...], xn_ref[...]], axis=0)
    h = _modulated(x, g_ref[...], mod_ref, 0).astype(BF16)
    row = lax.broadcasted_iota(jnp.int32, (tm, 1), 0)
    pos = jnp.where(i < lat_tiles, (i * tm) % n_seq + row, row & (n_ctx - 1))
    last = jnp.where(i < lat_tiles, n_seq - 1, n_ctx - 1)
    has_prev = pos != 0
    has_next = pos != last
    for n0 in range(0, o_ref.shape[1], tn):
        cols = slice(n0, n0 + tn)
        u = _dot(h, w_ref[:, cols]) + b_ref[:, cols]
        prev = jnp.where(has_prev, u[OCT - 1:OCT - 1 + tm], 0.0)
        nxt = jnp.where(has_next, u[OCT + 1:OCT + 1 + tm], 0.0)
        o_ref[:, cols] = (cb_ref[:, cols] + prev * cw_ref[0:1, cols] + u[OCT:OCT + tm] * cw_ref[1:2, cols]
                          + nxt * cw_ref[2:3, cols])


def _inproj(xs, gain, mod, w, bias, conv_w, conv_b, *, n_rows, n_seq, n_batch, n_ctx):
    d = xs.shape[1]
    n_out = w.shape[1]
    tm, tn = TOKEN_TILE, 512
    assert n_seq % tm == 0 and tm % n_ctx == 0 and n_ctx & (n_ctx - 1) == 0
    halo = tm // OCT
    last_halo = xs.shape[0] // OCT - 1
    kern = functools.partial(_inproj_kernel, tn=tn, n_seq=n_seq, n_ctx=n_ctx, lat_tiles=n_seq * n_batch // tm)
    return pl.pallas_call(
        kern,
        out_shape=jax.ShapeDtypeStruct((n_rows, n_out), F32),
        grid=(n_rows // tm,),
        in_specs=[pl.BlockSpec((tm, d), lambda i: (i, 0)),
                  pl.BlockSpec((OCT, d), lambda i: (jnp.maximum(i * halo - 1, 0), 0)),
                  pl.BlockSpec((OCT, d), lambda i: (jnp.minimum((i + 1) * halo, last_halo), 0)),
                  pl.BlockSpec((1, d), lambda i: (0, 0)),
                  pl.BlockSpec((None, 2, d), _mod_index(tm, n_seq, n_batch)),
                  pl.BlockSpec((d, n_out), lambda i: (0, 0)),
                  pl.BlockSpec((1, n_out), lambda i: (0, 0)),
                  pl.BlockSpec((HY_SHORT, n_out), lambda i: (0, 0)),
                  pl.BlockSpec((1, n_out), lambda i: (0, 0))],
        out_specs=pl.BlockSpec((tm, n_out), lambda i: (i, 0)),
        compiler_params=_params("arbitrary"),
        name="hyena_inproj",
    )(xs, xs, xs, gain.reshape(1, d), mod, w, bias.reshape(1, n_out), conv_w, conv_b.reshape(1, n_out))


def _rope_tables(n_seq, tm):
    rows = n_seq // GRID_W
    row = jnp.repeat(jnp.arange(rows), GRID_W).astype(F32)
    col = jnp.tile(jnp.arange(GRID_W), rows).astype(F32)
    half = HEAD_DIM // 2
    n_freq = half // 2
    inv = ROPE_THETA ** (-jnp.arange(n_freq, dtype=F32) / n_freq)
    ang = jnp.concatenate([row[:, None] * inv, col[:, None] * inv], axis=-1)
    cos = jnp.tile(jnp.cos(ang), (1, LANES // half))
    sin = jnp.tile(jnp.concatenate([-jnp.sin(ang), jnp.sin(ang)], axis=-1), (1, LANES // HEAD_DIM))
    cos = jnp.concatenate([cos, jnp.ones((tm, LANES), F32)], axis=0)
    sin = jnp.concatenate([sin, jnp.zeros((tm, LANES), F32)], axis=0)
    return cos, sin


def _doubled(w, n_heads):
    lead = w.shape[:-1]
    w = w.reshape(lead + (n_heads, 1, HEAD_DIM))
    return jnp.broadcast_to(w, lead + (n_heads, 2, HEAD_DIM)).reshape(lead + (n_heads * 2 * HEAD_DIM,))


def kernel(x, c, ctx, c_ctx, ada_w, ada_b, norm_mix, norm_ffn, attn_w_qkv, attn_b_qkv, attn_w_o, attn_b_o, attn_sinks, hy_w_in, hy_b_in, hy_conv_w, hy_conv_b, hy_f_w1, hy_f_b1, hy_f_w2, hy_f_b2, hy_f_w3, hy_f_b3, hy_f_freq, hy_f_wout, hy_bias_d, hy_w_o, hy_b_o, moe_router_w, moe_router_b, moe_w_gu, moe_b_gu, moe_w_down, moe_b_down, final_norm):
    B, N, D = x.shape
    C = ctx.shape[1]
    T, TC = B * N, B * C
    q_dim = N_HEADS * HEAD_DIM
    kv_dim = N_KV_HEADS * HEAD_DIM
    dims = dict(n_seq=N, n_batch=B)

    c_all = jnp.concatenate([c, c_ctx[None, :], jnp.zeros((16 - B - 1, D), F32)], axis=0)
    mod_all = _ada_table(c_all, ada_w, ada_b)[:, :B + 1].reshape(DEPTH, B + 1, 6, D)
    cos, sin = _rope_tables(N, TOKEN_TILE)

    xs = jnp.concatenate([x.reshape(T, D), ctx.reshape(TC, D)], axis=0)
    for i in range(DEPTH):
        kind, j = i % N_MIXERS, i // N_MIXERS
        update_ctx = any(l % N_MIXERS == 0 for l in range(i + 1, DEPTH))
        need_ctx = update_ctx or kind == 0
        mod = mod_all[i]
        n_in = T + TC if need_ctx else T
        n_out = T + TC if update_ctx else T
        if kind == 0:
            wq, wk, wv = (attn_w_qkv[j][:, :q_dim], attn_w_qkv[j][:, q_dim:q_dim + kv_dim],
                          attn_w_qkv[j][:, q_dim + kv_dim:])
            bq, bk, bv = (attn_b_qkv[j][:q_dim], attn_b_qkv[j][q_dim:q_dim + kv_dim],
                          attn_b_qkv[j][q_dim + kv_dim:])
            w = jnp.concatenate([wq, _doubled(wk, N_KV_HEADS), _doubled(wv, N_KV_HEADS)], axis=1).astype(BF16)
            b = jnp.concatenate([bq, _doubled(bk, N_KV_HEADS), _doubled(bv, N_KV_HEADS)])
            qkv = _norm_matmul(xs, norm_mix[i], mod[:, 0:2], w, b, n_rows=n_in, out_dtype=BF16,
                               rope=(cos, sin, q_dim + 2 * kv_dim, q_dim), **dims)
            o = _attention(qkv, attn_sinks[j], n_ctx=C, ctx_out=update_ctx, **dims)
            xs = _matmul_residual(o, attn_w_o[j].astype(BF16), attn_b_o[j], xs, mod[:, 2:3], n_rows=n_out, **dims)
        else:
            uc = _inproj(xs, norm_mix[i], mod[:, 0:2], hy_w_in[j].astype(BF16), hy_b_in[j], hy_conv_w[j],
                         hy_conv_b[j], n_rows=n_in, n_ctx=C, **dims)
            filt = (hy_f_w1[j], hy_f_b1[j], hy_f_w2[j], hy_f_b2[j], hy_f_w3[j], hy_f_b3[j], hy_f_freq[j],
                    hy_f_wout[j])
            z = _hyena_mixer(uc, hy_bias_d[j], filt, n_ctx=C, with_ctx=update_ctx, **dims)
            xs = _matmul_residual(z, hy_w_o[j].astype(BF16), hy_b_o[j], xs, mod[:, 2:3], n_rows=n_out, **dims)
        xs = _moe_layer(xs, i, mod, norm_ffn[i], moe_router_w[i], moe_router_b[i], moe_w_gu, moe_b_gu,
                        moe_w_down, moe_b_down, final_norm, n_rows=n_out, final=(i == DEPTH - 1), **dims)
    return xs.reshape(B, N, D)
```

```python
import functools
import math

import jax
import jax.numpy as jnp
import numpy as np
from jax import lax
from jax.experimental import pallas as pl
from jax.experimental.pallas import tpu as pltpu

DEPTH = 4
N_MIXERS = 2
GRID_W = 64
N_HEADS = 16
N_KV_HEADS = 4
HEAD_DIM = 64
GROUP = N_HEADS // N_KV_HEADS
WINDOW = 128
ROPE_THETA = 10000.0
HY_ORDER = 2
HY_SHORT = 3
HY_BANDS = 16
HY_TARGET = 1e-2
HY_FAST_DECAY = 0.3
HY_SLOW_DECAY = 1.5
N_EXPERTS = 32
TOP_K = 4
SWIGLU_LIMIT = 7.0
SWIGLU_ALPHA = 1.702
NORM_EPS = 1e-6

LANES = 128
OCT = 8
MINOR = 128
TOKEN_TILE = 512
Q_TILE = 128
MOE_TILE = 512
VMEM_LIMIT = 56 * 1024 * 1024

F32 = jnp.float32
BF16 = jnp.bfloat16


def _dot(a, b):
    return jnp.dot(a, b, preferred_element_type=F32)


def _params(*sem):
    return pltpu.CompilerParams(dimension_semantics=sem, vmem_limit_bytes=VMEM_LIMIT)


def _ada_kernel(c_ref, w_ref, b_ref, o_ref):
    c = c_ref[...]
    s = c * (1.0 / (1.0 + jnp.exp(-c)))
    o_ref[...] = jnp.dot(s, w_ref[...], preferred_element_type=F32,
                         precision=lax.Precision.HIGHEST) + b_ref[...]


def _ada_table(c_all, ada_w, ada_b):
    r, d = c_all.shape
    depth, _, n6 = ada_w.shape
    tn = n6 // 4
    return pl.pallas_call(
        _ada_kernel,
        out_shape=jax.ShapeDtypeStruct((depth, r, n6), F32),
        grid=(depth, n6 // tn),
        in_specs=[pl.BlockSpec((r, d), lambda l, j: (0, 0)),
                  pl.BlockSpec((None, d, tn), lambda l, j: (l, 0, j)),
                  pl.BlockSpec((None, 1, tn), lambda l, j: (l, 0, j))],
        out_specs=pl.BlockSpec((None, r, tn), lambda l, j: (l, 0, j)),
        compiler_params=_params("arbitrary", "arbitrary"),
        name="ada_table",
    )(c_all, ada_w, ada_b.reshape(depth, 1, n6))


def _modulated(x, g, mod_ref, row):
    y = x * lax.rsqrt(jnp.mean(x * x, axis=-1, keepdims=True) + NORM_EPS) * g
    return y * (1.0 + mod_ref[row + 1:row + 2, :]) + mod_ref[row:row + 1, :]


def _mod_index(tm, n_seq, n_batch):
    return lambda i: (jnp.minimum(i * tm // n_seq, n_batch), 0, 0)


def _nm_kernel(x_ref, g_ref, mod_ref, w_ref, b_ref, cos_ref, sin_ref, o_ref, *, tn, rope_cols, q_cols):
    h = _modulated(x_ref[...], g_ref[...], mod_ref, 0).astype(BF16)
    n_out = o_ref.shape[1]
    if rope_cols:
        reps = tn // LANES
        cos = jnp.tile(cos_ref[...], (1, reps))
        sin = jnp.tile(sin_ref[...], (1, reps))
        lane = lax.broadcasted_iota(jnp.int32, (1, tn), 1)
        first_half = (lane & (HEAD_DIM - 1)) < (HEAD_DIM // 2)
    for n0 in range(0, n_out, tn):
        acc = _dot(h, w_ref[:, n0:n0 + tn]) + b_ref[:, n0:n0 + tn]
        if n0 < rope_cols:
            partner = jnp.where(first_half, pltpu.roll(acc, tn - HEAD_DIM // 2, 1),
                                pltpu.roll(acc, HEAD_DIM // 2, 1))
            acc = acc * cos + partner * sin
            if n0 < q_cols:
                acc = acc * (HEAD_DIM ** -0.5)
        o_ref[:, n0:n0 + tn] = acc.astype(o_ref.dtype)


def _norm_matmul(xs, gain, mod, w, bias, *, n_rows, n_seq, n_batch, out_dtype, rope=None):
    d = xs.shape[1]
    n_out = w.shape[1]
    tm, tn = TOKEN_TILE, 512
    if rope is None:
        cos = jnp.zeros((tm, LANES), F32)
        sin = cos
        rope_cols = q_cols = 0
        rope_index = lambda i: (0, 0)
    else:
        cos, sin, rope_cols, q_cols = rope
        seq_tiles = n_seq // tm
        lat_tiles = n_seq * n_batch // tm
        rope_index = lambda i: (jnp.where(i < lat_tiles, i % seq_tiles, seq_tiles), 0)
    kern = functools.partial(_nm_kernel, tn=tn, rope_cols=rope_cols, q_cols=q_cols)
    return pl.pallas_call(
        kern,
        out_shape=jax.ShapeDtypeStruct((n_rows, n_out), out_dtype),
        grid=(n_rows // tm,),
        in_specs=[pl.BlockSpec((tm, d), lambda i: (i, 0)),
                  pl.BlockSpec((1, d), lambda i: (0, 0)),
                  pl.BlockSpec((None, 2, d), _mod_index(tm, n_seq, n_batch)),
                  pl.BlockSpec((d, n_out), lambda i: (0, 0)),
                  pl.BlockSpec((1, n_out), lambda i: (0, 0)),
                  pl.BlockSpec((tm, LANES), rope_index),
                  pl.BlockSpec((tm, LANES), rope_index)],
        out_specs=pl.BlockSpec((tm, n_out), lambda i: (i, 0)),
        compiler_params=_params("arbitrary"),
        name="norm_matmul",
    )(xs, gain.reshape(1, d), mod, w, bias.reshape(1, n_out), cos, sin)


def _res_kernel(a_ref, w_ref, b_ref, x_ref, gate_ref, o_ref):
    y = _dot(a_ref[...].astype(BF16), w_ref[...]) + b_ref[...]
    o_ref[...] = x_ref[...] + gate_ref[...] * y


def _matmul_residual(a, w, bias, xs, gate, *, n_rows, n_seq, n_batch):
    k = a.shape[1]
    d = w.shape[1]
    tm = TOKEN_TILE
    return pl.pallas_call(
        _res_kernel,
        out_shape=jax.ShapeDtypeStruct((n_rows, d), F32),
        grid=(n_rows // tm,),
        in_specs=[pl.BlockSpec((tm, k), lambda i: (i, 0)),
                  pl.BlockSpec((k, d), lambda i: (0, 0)),
                  pl.BlockSpec((1, d), lambda i: (0, 0)),
                  pl.BlockSpec((tm, d), lambda i: (i, 0)),
                  pl.BlockSpec((None, 1, d), _mod_index(tm, n_seq, n_batch))],
        out_specs=pl.BlockSpec((tm, d), lambda i: (i, 0)),
        compiler_params=_params("arbitrary"),
        name="matmul_residual",
    )(a, w, bias.reshape(1, d), xs, gate)


def _attend_pairs(q_ref, segments, sink_ref, o_ref, kvh):
    tq = q_ref.shape[0]
    lane = lax.broadcasted_iota(jnp.int32, (tq, LANES), 1)
    low = lane < HEAD_DIM
    row = lax.broadcasted_iota(jnp.int32, (2 * tq, 1), 0)
    for j in range(GROUP // 2):
        c0 = kvh * GROUP * HEAD_DIM + j * LANES
        qp = q_ref[:, c0:c0 + LANES]
        zero = jnp.zeros_like(qp)
        q2 = jnp.concatenate([jnp.where(low, qp, zero), jnp.where(low, zero, qp)], axis=0)
        head = kvh * GROUP + 2 * j
        sink = jnp.where(row < tq, sink_ref[head], sink_ref[head + 1])
        lane_tiles = lambda a: [a[:, t:t + LANES] for t in range(0, a.shape[1], LANES)]
        scores, folded = [], None
        for k, _, mask in segments:
            s = lax.dot_general(q2, k, (((1,), (1,)), ((), ())), preferred_element_type=F32)
            if mask is not None:
                s = jnp.where(mask, s, -jnp.inf)
            scores.append(s)
            for t in lane_tiles(s):
                folded = t if folded is None else jnp.maximum(folded, t)
        m = jnp.maximum(sink, jnp.max(folded, axis=-1, keepdims=True))
        o2, folded = None, None
        for s, (_, v, _) in zip(scores, segments):
            p = jnp.exp(s - m)
            for t in lane_tiles(p):
                folded = t if folded is None else folded + t
            pv = _dot(p.astype(BF16), v)
            o2 = pv if o2 is None else o2 + pv
        o2 = o2 / (jnp.exp(sink - m) + jnp.sum(folded, axis=-1, keepdims=True))
        o_ref[:, c0:c0 + LANES] = jnp.where(low, o2[:tq], o2[tq:]).astype(o_ref.dtype)


def _attn_kernel(sink_ref, q_ref, kp_ref, kc_ref, kn_ref, vp_ref, vc_ref, vn_ref, kx_ref, vx_ref, o_ref,
                 *, n_q_blocks):
    qi = pl.program_id(1)
    tq = q_ref.shape[0]

    @pl.when(qi < n_q_blocks)
    def _():
        r = lax.broadcasted_iota(jnp.int32, (2 * tq, tq), 0) & (tq - 1)
        c = lax.broadcasted_iota(jnp.int32, (2 * tq, tq), 1)
        prev_ok = (c >= r) & (qi >= 1)
        next_ok = (c <= r) & (qi <= n_q_blocks - 2)
        for kvh in range(N_KV_HEADS):
            cs = slice(kvh * LANES, (kvh + 1) * LANES)
            segments = [(kc_ref[:, cs], vc_ref[:, cs], None), (kx_ref[:, cs], vx_ref[:, cs], None),
                        (kp_ref[:, cs], vp_ref[:, cs], prev_ok), (kn_ref[:, cs], vn_ref[:, cs], next_ok)]
            _attend_pairs(q_ref, segments, sink_ref, o_ref, kvh)

    @pl.when(qi >= n_q_blocks)
    def _():
        for kvh in range(N_KV_HEADS):
            cs = slice(kvh * LANES, (kvh + 1) * LANES)
            _attend_pairs(q_ref, [(kx_ref[:, cs], vx_ref[:, cs], None)], sink_ref, o_ref, kvh)


def _attention(qkv, sinks, *, n_seq, n_batch, n_ctx, ctx_out):
    tq = Q_TILE
    assert tq == WINDOW
    nq = n_seq // tq
    ncq = n_ctx // tq if ctx_out else 0
    lat_blocks = n_batch * nq
    q_dim = N_HEADS * HEAD_DIM
    kv_w = N_KV_HEADS * LANES
    k_col = q_dim // kv_w
    v_col = k_col + 1
    ctx_row0 = n_batch * n_seq // n_ctx

    def q_index(b, i, s):
        return (jnp.where(i < nq, b * nq + i, lat_blocks + b * (n_ctx // tq) + (i - nq)), 0)

    def kv_index(off, col):
        def index(b, i, s):
            return (b * nq + jnp.clip(i + off, 0, nq - 1), col)
        return index

    n_rows = n_batch * n_seq + (n_batch * n_ctx if ctx_out else 0)
    grid_spec = pltpu.PrefetchScalarGridSpec(
        num_scalar_prefetch=1,
        grid=(n_batch, nq + ncq),
        in_specs=[pl.BlockSpec((tq, q_dim), q_index),
                  pl.BlockSpec((tq, kv_w), kv_index(-1, k_col)),
                  pl.BlockSpec((tq, kv_w), kv_index(0, k_col)),
                  pl.BlockSpec((tq, kv_w), kv_index(1, k_col)),
                  pl.BlockSpec((tq, kv_w), kv_index(-1, v_col)),
                  pl.BlockSpec((tq, kv_w), kv_index(0, v_col)),
                  pl.BlockSpec((tq, kv_w), kv_index(1, v_col)),
                  pl.BlockSpec((n_ctx, kv_w), lambda b, i, s: (ctx_row0 + b, k_col)),
                  pl.BlockSpec((n_ctx, kv_w), lambda b, i, s: (ctx_row0 + b, v_col))],
        out_specs=pl.BlockSpec((tq, q_dim), q_index),
    )
    return pl.pallas_call(
        functools.partial(_attn_kernel, n_q_blocks=nq),
        out_shape=jax.ShapeDtypeStruct((n_rows, q_dim), BF16),
        grid_spec=grid_spec,
        compiler_params=_params("arbitrary", "arbitrary"),
        name="attention",
    )(sinks, qkv, qkv, qkv, qkv, qkv, qkv, qkv, qkv, qkv)


def _pack_pairs(x):
    half = x.shape[1] // 2
    bits = lax.bitcast_convert_type(x.astype(BF16).astype(F32), jnp.int32)
    return lax.shift_right_logical(bits[:, :half], 16) | (bits[:, half:] & -65536)


def _unpack_pairs(u):
    return (lax.bitcast_convert_type(lax.shift_left(u, 16), F32),
            lax.bitcast_convert_type(u & -65536, F32))


def _router_kernel(x_ref, g_ref, mod_ref, w_ref, b_ref, tri_ref, tok_ref, route_ref, count_ref, seen):
    @pl.when(pl.program_id(0) == 0)
    def _():
        seen[...] = jnp.zeros_like(seen)

    h = _modulated(x_ref[...], g_ref[...], mod_ref, 0)
    tok_ref[...] = _pack_pairs(h)
    logits = jnp.dot(h, w_ref[...], preferred_element_type=F32, precision=lax.Precision.HIGHEST) + b_ref[...]
    lane = lax.broadcasted_iota(jnp.int32, logits.shape, 1)
    rest = jnp.where(lane < N_EXPERTS, logits, -jnp.inf)
    chosen = jnp.zeros(logits.shape, F32)
    top_v, top_i = [], []
    for _ in range(TOP_K):
        best = jnp.max(rest, axis=-1, keepdims=True)
        idx = jnp.min(jnp.where(rest == best, lane, LANES), axis=-1, keepdims=True)
        hit = lane == idx
        top_v.append(best)
        top_i.append(idx)
        rest = jnp.where(hit, -jnp.inf, rest)
        chosen = jnp.where(hit, 1.0, chosen)
    weights = [jnp.exp(v - top_v[0]) for v in top_v]
    denom = (weights[0] + weights[1]) + (weights[2] + weights[3])
    earlier = _dot(tri_ref[...], chosen.astype(BF16)) + seen[...]
    seen[...] = seen[...] + jnp.sum(chosen, axis=0, keepdims=True)
    route = jnp.zeros(logits.shape, F32)
    for k in range(TOP_K):
        rank = jnp.sum(jnp.where(lane == top_i[k], earlier, 0.0), axis=-1, keepdims=True)
        route = jnp.where(lane == k, top_i[k].astype(F32), route)
        route = jnp.where(lane == TOP_K + k, weights[k] / denom, route)
        route = jnp.where(lane == 2 * TOP_K + k, rank, route)
    route_ref[...] = route
    count_ref[...] = jnp.broadcast_to(seen[...], count_ref.shape)


def _router(xs, gain, mod, w, bias, *, n_rows, n_seq, n_batch):
    d = xs.shape[1]
    tm = TOKEN_TILE
    tri = jnp.asarray(np.tril(np.ones((tm, tm)), -1), BF16)
    return pl.pallas_call(
        _router_kernel,
        out_shape=(jax.ShapeDtypeStruct((n_rows, d // 2), jnp.int32), jax.ShapeDtypeStruct((n_rows, LANES), F32),
                   jax.ShapeDtypeStruct((OCT, LANES), F32)),
        grid=(n_rows // tm,),
        in_specs=[pl.BlockSpec((tm, d), lambda i: (i, 0)),
                  pl.BlockSpec((1, d), lambda i: (0, 0)),
                  pl.BlockSpec((None, 2, d), _mod_index(tm, n_seq, n_batch)),
                  pl.BlockSpec((d, LANES), lambda i: (0, 0)),
                  pl.BlockSpec((1, LANES), lambda i: (0, 0)),
                  pl.BlockSpec((tm, tm), lambda i: (0, 0))],
        out_specs=(pl.BlockSpec((tm, d // 2), lambda i: (i, 0)), pl.BlockSpec((tm, LANES), lambda i: (i, 0)),
                   pl.BlockSpec((OCT, LANES), lambda i: (0, 0))),
        scratch_shapes=[pltpu.VMEM((1, LANES), F32)],
        compiler_params=_params("arbitrary"),
        name="router",
    )(xs, gain.reshape(1, d), mod, w, bias, tri)


def _expert_kernel(be_ref, nb_ref, x_ref, wgu_ref, bgu_ref, wd_ref, bd_ref, o_ref, wgu_bf, wd_bf):
    i = pl.program_id(0)
    fresh = jnp.logical_or(i == 0, be_ref[i] != be_ref[jnp.maximum(i - 1, 0)])

    @pl.when(jnp.logical_and(fresh, i < nb_ref[0]))
    def _():
        wgu_bf[...] = wgu_ref[...].astype(BF16)
        wd_bf[...] = wd_ref[...].astype(BF16)

    @pl.when(i < nb_ref[0])
    def _():
        d_e = wd_ref.shape[0]
        x = jnp.concatenate(_unpack_pairs(x_ref[...]), axis=1).astype(BF16)
        g = jnp.minimum(_dot(x, wgu_bf[:, :d_e]) + bgu_ref[:, :d_e], SWIGLU_LIMIT)
        up = jnp.clip(_dot(x, wgu_bf[:, d_e:]) + bgu_ref[:, d_e:], -SWIGLU_LIMIT, SWIGLU_LIMIT)
        act = g * (1.0 / (1.0 + jnp.exp(-SWIGLU_ALPHA * g))) * (up + 1.0)
        o_ref[...] = _pack_pairs(_dot(act.astype(BF16), wd_bf[...]) + bd_ref[...])

    @pl.when(i >= nb_ref[0])
    def _():
        o_ref[...] = jnp.zeros_like(o_ref)


def _experts(xs, block_e, n_used, w_gu, b_gu, w_down, b_down, layer):
    r = xs.shape[0]
    n_e, d, d_gu = w_gu.shape[1:]
    tm = MOE_TILE
    grid_spec = pltpu.PrefetchScalarGridSpec(
        num_scalar_prefetch=2,
        grid=(r // tm,),
        in_specs=[pl.BlockSpec((tm, d // 2), lambda i, be, nb: (i, 0)),
                  pl.BlockSpec((None, None, d, d_gu), lambda i, be, nb: (layer, be[i], 0, 0)),
                  pl.BlockSpec((None, None, 1, d_gu), lambda i, be, nb: (layer, be[i], 0, 0)),
                  pl.BlockSpec((None, None, d_gu // 2, d), lambda i, be, nb: (layer, be[i], 0, 0)),
                  pl.BlockSpec((None, None, 1, d), lambda i, be, nb: (layer, be[i], 0, 0))],
        out_specs=pl.BlockSpec((tm, d // 2), lambda i, be, nb: (i, 0)),
        scratch_shapes=[pltpu.VMEM((d, d_gu), BF16), pltpu.VMEM((d_gu // 2, d), BF16)],
    )
    return pl.pallas_call(
        _expert_kernel,
        out_shape=jax.ShapeDtypeStruct((r, d // 2), jnp.int32),
        grid_spec=grid_spec,
        compiler_params=_params("arbitrary"),
        name="experts",
    )(block_e, n_used, xs, w_gu, b_gu.reshape(b_gu.shape[0], n_e, 1, d_gu), w_down,
      b_down.reshape(b_down.shape[0], n_e, 1, d))


def _combine_kernel(x_ref, y0_ref, y1_ref, y2_ref, y3_ref, route_ref, gate_ref, g_ref, o_ref, *, final):
    w = [route_ref[:, TOP_K + k:TOP_K + k + 1] for k in range(TOP_K)]
    y = [_unpack_pairs(r[...]) for r in (y0_ref, y1_ref, y2_ref, y3_ref)]
    f = jnp.concatenate([(w[0] * y[0][h] + w[1] * y[1][h]) + (w[2] * y[2][h] + w[3] * y[3][h]) for h in range(2)],
                        axis=1)
    x = x_ref[...] + gate_ref[...] * f
    if final:
        x = x * lax.rsqrt(jnp.mean(x * x, axis=-1, keepdims=True) + NORM_EPS) * g_ref[...]
    o_ref[...] = x


def _combine(xs, ys, route, gate, final_gain, *, n_rows, n_seq, n_batch, final):
    d = xs.shape[1]
    tm = TOKEN_TILE
    row_spec = pl.BlockSpec((tm, d), lambda i: (i, 0))
    return pl.pallas_call(
        functools.partial(_combine_kernel, final=final),
        out_shape=jax.ShapeDtypeStruct((n_rows, d), F32),
        grid=(n_rows // tm,),
        in_specs=[row_spec] + [pl.BlockSpec((tm, d // 2), lambda i: (i, 0))] * TOP_K
                 + [pl.BlockSpec((tm, LANES), lambda i: (i, 0)),
                                             pl.BlockSpec((None, 1, d), _mod_index(tm, n_seq, n_batch)),
                                             pl.BlockSpec((1, d), lambda i: (0, 0))],
        out_specs=row_spec,
        compiler_params=_params("arbitrary"),
        name="combine",
    )(xs, *ys, route, gate, final_gain.reshape(1, d))


def _moe_layer(xs, layer, mod, norm_g, router_w, router_b, w_gu, b_gu, w_down, b_down, final_gain,
               *, n_rows, n_seq, n_batch, final):
    d = xs.shape[1]
    rw = jnp.zeros((d, LANES), F32).at[:, :N_EXPERTS].set(router_w)
    rb = jnp.zeros((1, LANES), F32).at[0, :N_EXPERTS].set(router_b)
    tok, route, count = _router(xs, norm_g, mod[:, 3:5], rw, rb, n_rows=n_rows, n_seq=n_seq, n_batch=n_batch)
    top_i = route[:, :TOP_K].astype(jnp.int32)
    rank = route[:, 2 * TOP_K:3 * TOP_K].astype(jnp.int32)
    counts = count[0, :N_EXPERTS].astype(jnp.int32)
    n_assign = n_rows * TOP_K
    padded = (counts + MOE_TILE - 1) // MOE_TILE * MOE_TILE
    pad_end = jnp.cumsum(padded)
    pad_start = pad_end - padded
    start = jnp.cumsum(counts) - counts
    n_blocks = -(-n_assign // MOE_TILE) + N_EXPERTS
    n_slots = n_blocks * MOE_TILE
    block_e = jnp.minimum(jnp.sum(pad_end[None, :] <= (jnp.arange(n_blocks) * MOE_TILE)[:, None], axis=1),
                          N_EXPERTS - 1).astype(jnp.int32)
    n_used = (pad_end[-1] // MOE_TILE).astype(jnp.int32).reshape(1)
    slot_of = pad_start[top_i] + rank
    _, by_slot = lax.sort_key_val(slot_of.reshape(-1), jnp.arange(n_assign, dtype=jnp.int32), is_stable=False)
    slot_e = jnp.repeat(block_e, MOE_TILE)
    within = jnp.arange(n_slots, dtype=jnp.int32) - pad_start[slot_e]
    src = by_slot[jnp.clip(start[slot_e] + within, 0, n_assign - 1)]
    ys = _experts(tok[src // TOP_K], block_e, n_used, w_gu, b_gu, w_down, b_down, layer)
    parts = [ys[slot_of[:, k]] for k in range(TOP_K)]
    return _combine(xs, parts, route, mod[:, 5:6], final_gain, n_rows=n_rows, n_seq=n_seq, n_batch=n_batch,
                    final=final)


def _hyena_spectrum(L, consts, w1, b1, w2, b2, w3, b3, freq, w_out, d):
    t = jnp.linspace(0.0, 1.0, L, dtype=F32)[:, None]
    w = 2.0 * math.pi * jnp.arange(L, dtype=F32)[:, None] / L
    bands = jnp.linspace(1e-4, HY_BANDS - 1, HY_BANDS, dtype=F32)[None, :]
    z = jnp.concatenate([t, jnp.cos(bands * w), -jnp.sin(bands * w)], axis=-1)
    a = jnp.sin(freq * (z @ w1 + b1))
    a = jnp.sin(freq * (a @ w2 + b2))
    a = jnp.sin(freq * (a @ w3 + b3))
    hf = (a @ w_out).reshape(L, 2, HY_ORDER, d)
    deltas = jnp.linspace(math.log(HY_TARGET) / HY_SLOW_DECAY, math.log(HY_TARGET) / HY_FAST_DECAY, d, dtype=F32)
    hf = hf * jnp.exp(-t * jnp.abs(deltas))[:, None, None, :]
    return _filter_spectrum(hf.reshape(L // OCT, OCT, 2 * HY_ORDER * d), consts, d)


def _filter_kernel(f_ref, b_ref, kgf_ref, fm_ref, twc_ref, tws_ref, spec_ref, a_scr):
    n_k = kgf_ref.shape[0] // (2 * OCT)
    n_a = n_k - 1
    n_bo = MINOR // OCT
    dt = f_ref.shape[2]
    kc = _major_chunk(n_k)
    cr_rows = kc * OCT
    total = jnp.zeros((1, 2 * dt), F32)
    for bo in range(n_bo):
        rows = pl.ds(bo, n_a, stride=n_bo)
        x2 = jnp.concatenate([f_ref[rows].reshape(n_a * OCT, dt), b_ref[rows].reshape(n_a * OCT, dt)], axis=1)
        total = total + jnp.sum(jnp.abs(x2), axis=0, keepdims=True)
        p = _dot(kgf_ref[...], x2.astype(BF16))
        for ch in range(n_k // kc):
            r0 = 2 * ch * cr_rows
            ks = slice(ch * kc, (ch + 1) * kc)
            tw_rows = slice(ch * cr_rows, (ch + 1) * cr_rows)
            ar, ai = p[r0:r0 + cr_rows], p[r0 + cr_rows:r0 + 2 * cr_rows]
            c, s = twc_ref[tw_rows, bo:bo + 1], tws_ref[tw_rows, bo:bo + 1]
            a_scr[ks, 0, bo] = (ar * c + ai * s).reshape(kc, OCT, 2 * dt)
            a_scr[ks, 1, bo] = (ai * c - ar * s).reshape(kc, OCT, 2 * dt)
    b0 = b_ref[0, 0:1, :]
    inv = 1.0 / (total[:, :dt] + total[:, dt:] - jnp.abs(b0))
    unroll = max(u for u in range(1, 12) if n_k % u == 0)

    def minor_stage(step, carry):
        for j in range(unroll):
            k = step * unroll + j
            x = _dot(fm_ref[...], a_scr[k].reshape(2 * MINOR, 2 * dt).astype(BF16))
            spec_ref[k, 0] = ((x[:MINOR, :dt] + x[:MINOR, dt:] - b0) * inv).astype(spec_ref.dtype)
            spec_ref[k, 1] = ((x[MINOR:, :dt] - x[MINOR:, dt:]) * inv).astype(spec_ref.dtype)
        return carry

    lax.fori_loop(0, n_k // unroll, minor_stage, 0)
    for k in range(n_k, spec_ref.shape[0]):
        spec_ref[k] = jnp.zeros(spec_ref.shape[1:], spec_ref.dtype)


def _filter_spectrum(hf, consts, d):
    kgf, _, fm, _, twc, tws = consts
    dt = LANES
    seq_blk = hf.shape[0]
    n_k = kgf.shape[0] // (2 * OCT)
    n_kp = n_k + n_k % 2
    tiles = d // dt
    const_spec = lambda arr: pl.BlockSpec(arr.shape, lambda o, j: (0, 0))
    return pl.pallas_call(
        _filter_kernel,
        out_shape=jax.ShapeDtypeStruct((HY_ORDER, n_kp, 2, MINOR, d), BF16),
        grid=(HY_ORDER, tiles),
        in_specs=[pl.BlockSpec((seq_blk, OCT, dt), lambda o, j: (0, 0, o * tiles + j)),
                  pl.BlockSpec((seq_blk, OCT, dt), lambda o, j: (0, 0, (HY_ORDER + o) * tiles + j)),
                  const_spec(kgf), const_spec(fm), const_spec(twc), const_spec(tws)],
        out_specs=pl.BlockSpec((None, n_kp, 2, MINOR, dt), lambda o, j: (o, 0, 0, 0, j)),
        scratch_shapes=[pltpu.VMEM((n_k, 2, MINOR // OCT, OCT, 2 * dt), F32)],
        compiler_params=_params("arbitrary", "arbitrary"),
        name="hyena_filter",
    )(hf, hf, kgf, fm, twc, tws)


def _major_chunk(n_k):
    return max(c for c in range(1, 14) if n_k % c == 0)


def _dft_constants(L):
    n = 2 * L
    n1 = n // MINOR
    n_a, n_k = n1 // 2, n1 // 2 + 1
    k = np.arange(n_k)[:, None]
    a = np.arange(n_a)[None, :]
    th = 2.0 * np.pi * k * a / n1
    eye = np.eye(OCT)
    kgf = np.concatenate([np.kron(np.cos(th), eye), np.kron(-np.sin(th), eye)], axis=0)
    w = np.where((k == 0) | (k == n_a), 1.0, 2.0) / n
    kgi = np.concatenate([np.kron((np.cos(th) * w).T, eye), np.kron((-np.sin(th) * w).T, eye)], axis=1)
    b = np.arange(MINOR)
    ph = 2.0 * np.pi * np.outer(b, b) / MINOR
    fr, fi = np.cos(ph), -np.sin(ph)
    fm = np.block([[fr, -fi], [fi, fr]])
    fmi = np.block([[fr, fi], [-fi, fr]])
    bb = OCT * np.arange(MINOR // OCT)[None, None, :] + np.arange(OCT)[None, :, None]
    tw = 2.0 * np.pi * np.arange(n_k)[:, None, None] * bb / n
    pad = ((0, 0), (0, LANES - MINOR // OCT))
    twc = np.pad(np.cos(tw).reshape(n_k * OCT, -1), pad)
    tws = np.pad(np.sin(tw).reshape(n_k * OCT, -1), pad)
    kc = _major_chunk(n_k)
    kgf = kgf.reshape(2, n_k // kc, kc * OCT, -1).transpose(1, 0, 2, 3).reshape(2 * n_k * OCT, -1)
    kgi = kgi.reshape(-1, 2, n_k // kc, kc * OCT).transpose(0, 2, 1, 3).reshape(-1, 2 * n_k * OCT)
    first_im = kc * OCT
    last_im = 2 * n_k * OCT - OCT
    kgi = np.delete(kgi, np.r_[first_im:first_im + OCT, last_im:last_im + OCT], axis=1)
    return (jnp.asarray(kgf, BF16), jnp.asarray(kgi, BF16), jnp.asarray(fm, BF16), jnp.asarray(fmi, BF16),
            jnp.asarray(twc, F32), jnp.asarray(tws, F32))


def _hyena_kernel(z_ref, gate_ref, spec_ref, d_ref, kgf_ref, kgi_ref, fm_ref, fmi_ref, twc_ref, tws_ref, *rest):
    o_ref, a_scr = rest[-2], rest[-1]
    n_k = kgf_ref.shape[0] // (2 * OCT)
    n_a = n_k - 1
    n_bo = MINOR // OCT
    dt = z_ref.shape[2]
    kc = _major_chunk(n_k)
    cr_rows = kc * OCT
    side_by_side = lambda u, v: jnp.concatenate([u, v], axis=1)

    if a_scr.shape[0] > n_k:
        a_scr[n_k] = jnp.zeros(a_scr.shape[1:], F32)

    for bo in range(0, n_bo, 2):
        x2 = side_by_side(*[z_ref[pl.ds(bo + h, n_a, stride=n_bo)].reshape(n_a * OCT, dt) for h in range(2)])
        p = _dot(kgf_ref[...], x2.astype(BF16))
        for ch in range(n_k // kc):
            r0 = 2 * ch * cr_rows
            ks = slice(ch * kc, (ch + 1) * kc)
            tw_rows = slice(ch * cr_rows, (ch + 1) * cr_rows)
            for h in range(2):
                cols = slice(h * dt, (h + 1) * dt)
                ar, ai = p[r0:r0 + cr_rows, cols], p[r0 + cr_rows:r0 + 2 * cr_rows, cols]
                c, s = twc_ref[tw_rows, bo + h:bo + h + 1], tws_ref[tw_rows, bo + h:bo + h + 1]
                a_scr[ks, 0, bo + h] = (ar * c + ai * s).reshape(kc, OCT, dt)
                a_scr[ks, 1, bo + h] = (ai * c - ar * s).reshape(kc, OCT, dt)

    n_pairs = a_scr.shape[0] // 2
    unroll = max(u for u in range(1, 12) if n_pairs % u == 0)

    def minor_stage(step, carry):
        for j in range(unroll):
            k = 2 * (step * unroll + j)
            b2 = side_by_side(a_scr[k].reshape(2 * MINOR, dt), a_scr[k + 1].reshape(2 * MINOR, dt))
            x = _dot(fm_ref[...], b2.astype(BF16))
            xr, xi = x[:MINOR], x[MINOR:]
            gr = side_by_side(spec_ref[k, 0], spec_ref[k + 1, 0]).astype(F32)
            gi = side_by_side(spec_ref[k, 1], spec_ref[k + 1, 1]).astype(F32)
            y = jnp.concatenate([xr * gr - xi * gi, xr * gi + xi * gr], axis=0).astype(BF16)
            c2 = _dot(fmi_ref[...], y)
            a_scr[k] = c2[:, :dt].reshape(2, n_bo, OCT, dt)
            a_scr[k + 1] = c2[:, dt:].reshape(2, n_bo, OCT, dt)
        return carry

    lax.fori_loop(0, n_pairs // unroll, minor_stage, 0)

    for bo in range(0, n_bo, 2):
        halves = []
        for h in range(2):
            pieces = []
            for ch in range(n_k // kc):
                ks = slice(ch * kc, (ch + 1) * kc)
                tw_rows = slice(ch * cr_rows, (ch + 1) * cr_rows)
                cr = a_scr[ks, 0, bo + h].reshape(cr_rows, dt)
                ci = a_scr[ks, 1, bo + h].reshape(cr_rows, dt)
                c, s = twc_ref[tw_rows, bo + h:bo + h + 1], tws_ref[tw_rows, bo + h:bo + h + 1]
                im = cr * s + ci * c
                im = im[OCT if ch == 0 else 0:cr_rows - OCT if ch == n_k // kc - 1 else cr_rows]
                pieces += [cr * c - ci * s, im]
            halves.append(jnp.concatenate(pieces, axis=0))
        conv = _dot(kgi_ref[...], side_by_side(*halves).astype(BF16))
        for h in range(2):
            rows = pl.ds(bo + h, n_a, stride=n_bo)
            zin = z_ref[rows].reshape(n_a * OCT, dt)
            out = gate_ref[rows].reshape(n_a * OCT, dt) * (conv[:, h * dt:(h + 1) * dt] + zin * d_ref[...])
            o_ref[rows] = out.reshape(n_a, OCT, dt)


def _hyena_conv(zin, zin_col0, gate, gate_col0, spec, bias_d, consts, *, seq_len, n_seqs, row0, out_rows, prev_out,
                order=0):
    d = spec.shape[-1]
    dt = LANES
    n_k = spec.shape[1]
    seq_blk = seq_len // OCT
    blk0 = row0 // seq_len
    as_oct = lambda arr: arr.reshape(arr.shape[0] // OCT, OCT, arr.shape[1])
    const_spec = lambda arr: pl.BlockSpec(arr.shape, lambda j, b: (0, 0))
    in_specs = [pl.BlockSpec((seq_blk, OCT, dt), lambda j, b: (blk0 + b, 0, zin_col0 // dt + j)),
                pl.BlockSpec((seq_blk, OCT, dt), lambda j, b: (blk0 + b, 0, gate_col0 // dt + j)),
                pl.BlockSpec((None, n_k, 2, MINOR, dt), lambda j, b: (order, 0, 0, 0, j)),
                pl.BlockSpec((None, 1, dt), lambda j, b: (order, 0, j))] + [const_spec(cst) for cst in consts]
    args = [as_oct(zin), as_oct(gate), spec, bias_d.reshape(bias_d.shape[0], 1, d), *consts]
    aliases = {}
    if prev_out is not None:
        in_specs.append(pl.BlockSpec(memory_space=pl.ANY))
        aliases = {len(args): 0}
        args.append(as_oct(prev_out))
    out = pl.pallas_call(
        _hyena_kernel,
        out_shape=jax.ShapeDtypeStruct((out_rows // OCT, OCT, d), F32),
        grid=(d // dt, n_seqs),
        in_specs=in_specs,
        out_specs=pl.BlockSpec((seq_blk, OCT, dt), lambda j, b: (blk0 + b, 0, j)),
        scratch_shapes=[pltpu.VMEM((n_k, 2, MINOR // OCT, OCT, dt), F32)],
        input_output_aliases=aliases,
        compiler_params=_params("arbitrary", "arbitrary"),
        name="hyena_conv",
    )(*args)
    return out.reshape(out_rows, d)


def _hyena_mixer(uc, bias_d, filt, *, n_seq, n_batch, n_ctx, with_ctx):
    d = uc.shape[1] // 3
    rows = uc.shape[0]
    groups = [(n_seq, n_batch, 0)] + ([(n_ctx, n_batch, n_seq * n_batch)] if with_ctx else [])
    consts = [_dft_constants(L) for L, _, _ in groups]
    specs = [_hyena_spectrum(L, cst, *filt, d) for (L, _, _), cst in zip(groups, consts)]
    z = None
    for o in range(HY_ORDER):
        zin, zin_col0 = (uc, 2 * d) if o == 0 else (z, 0)
        out = None
        for (L, n_seqs, row0), spec, cst in zip(groups, specs, consts):
            out = _hyena_conv(zin, zin_col0, uc, o * d, spec, bias_d, cst, seq_len=L, n_seqs=n_seqs, row0=row0,
                              out_rows=rows, prev_out=out, order=o)
        z = out
    return z


def _inproj_kernel(x_ref, xp_ref, xn_ref, g_ref, mod_ref, w_ref, b_ref, cw_ref, cb_ref, o_ref,
                   *, tn, n_seq, n_ctx, lat_tiles):
    i = pl.program_id(0)
    tm = x_ref.shape[0]
    x = jnp.concatenate([xp_ref[...], x_ref[...], xn_ref[...]], axis=0)
    h = _modulated(x, g_ref[...], mod_ref, 0).astype(BF16)
    row = lax.broadcasted_iota(jnp.int32, (tm, 1), 0)
    pos = jnp.where(i < lat_tiles, (i * tm) % n_seq + row, row & (n_ctx - 1))
    last = jnp.where(i < lat_tiles, n_seq - 1, n_ctx - 1)
    has_prev = pos != 0
    has_next = pos != last
    for n0 in range(0, o_ref.shape[1], tn):
        cols = slice(n0, n0 + tn)
        u = _dot(h, w_ref[:, cols]) + b_ref[:, cols]
        prev = jnp.where(has_prev, u[OCT - 1:OCT - 1 + tm], 0.0)
        nxt = jnp.where(has_next, u[OCT + 1:OCT + 1 + tm], 0.0)
        o_ref[:, cols] = (cb_ref[:, cols] + prev * cw_ref[0:1, cols] + u[OCT:OCT + tm] * cw_ref[1:2, cols]
                          + nxt * cw_ref[2:3, cols])


def _inproj(xs, gain, mod, w, bias, conv_w, conv_b, *, n_rows, n_seq, n_batch, n_ctx):
    d = xs.shape[1]
    n_out = w.shape[1]
    tm, tn = TOKEN_TILE, 512
    assert n_seq % tm == 0 and tm % n_ctx == 0 and n_ctx & (n_ctx - 1) == 0
    halo = tm // OCT
    last_halo = xs.shape[0] // OCT - 1
    kern = functools.partial(_inproj_kernel, tn=tn, n_seq=n_seq, n_ctx=n_ctx, lat_tiles=n_seq * n_batch // tm)
    return pl.pallas_call(
        kern,
        out_shape=jax.ShapeDtypeStruct((n_rows, n_out), F32),
        grid=(n_rows // tm,),
        in_specs=[pl.BlockSpec((tm, d), lambda i: (i, 0)),
                  pl.BlockSpec((OCT, d), lambda i: (jnp.maximum(i * halo - 1, 0), 0)),
                  pl.BlockSpec((OCT, d), lambda i: (jnp.minimum((i + 1) * halo, last_halo), 0)),
                  pl.BlockSpec((1, d), lambda i: (0, 0)),
                  pl.BlockSpec((None, 2, d), _mod_index(tm, n_seq, n_batch)),
                  pl.BlockSpec((d, n_out), lambda i: (0, 0)),
                  pl.BlockSpec((1, n_out), lambda i: (0, 0)),
                  pl.BlockSpec((HY_SHORT, n_out), lambda i: (0, 0)),
                  pl.BlockSpec((1, n_out), lambda i: (0, 0))],
        out_specs=pl.BlockSpec((tm, n_out), lambda i: (i, 0)),
        compiler_params=_params("arbitrary"),
        name="hyena_inproj",
    )(xs, xs, xs, gain.reshape(1, d), mod, w, bias.reshape(1, n_out), conv_w, conv_b.reshape(1, n_out))


def _rope_tables(n_seq, tm):
    rows = n_seq // GRID_W
    row = jnp.repeat(jnp.arange(rows), GRID_W).astype(F32)
    col = jnp.tile(jnp.arange(GRID_W), rows).astype(F32)
    half = HEAD_DIM // 2
    n_freq = half // 2
    inv = ROPE_THETA ** (-jnp.arange(n_freq, dtype=F32) / n_freq)
    ang = jnp.concatenate([row[:, None] * inv, col[:, None] * inv], axis=-1)
    cos = jnp.tile(jnp.cos(ang), (1, LANES // half))
    sin = jnp.tile(jnp.concatenate([-jnp.sin(ang), jnp.sin(ang)], axis=-1), (1, LANES // HEAD_DIM))
    cos = jnp.concatenate([cos, jnp.ones((tm, LANES), F32)], axis=0)
    sin = jnp.concatenate([sin, jnp.zeros((tm, LANES), F32)], axis=0)
    return cos, sin


def _doubled(w, n_heads):
    lead = w.shape[:-1]
    w = w.reshape(lead + (n_heads, 1, HEAD_DIM))
    return jnp.broadcast_to(w, lead + (n_heads, 2, HEAD_DIM)).reshape(lead + (n_heads * 2 * HEAD_DIM,))


def kernel(x, c, ctx, c_ctx, ada_w, ada_b, norm_mix, norm_ffn, attn_w_qkv, attn_b_qkv, attn_w_o, attn_b_o, attn_sinks, hy_w_in, hy_b_in, hy_conv_w, hy_conv_b, hy_f_w1, hy_f_b1, hy_f_w2, hy_f_b2, hy_f_w3, hy_f_b3, hy_f_freq, hy_f_wout, hy_bias_d, hy_w_o, hy_b_o, moe_router_w, moe_router_b, moe_w_gu, moe_b_gu, moe_w_down, moe_b_down, final_norm):
    B, N, D = x.shape
    C = ctx.shape[1]
    T, TC = B * N, B * C
    q_dim = N_HEADS * HEAD_DIM
    kv_dim = N_KV_HEADS * HEAD_DIM
    dims = dict(n_seq=N, n_batch=B)

    c_all = jnp.concatenate([c, c_ctx[None, :], jnp.zeros((16 - B - 1, D), F32)], axis=0)
    mod_all = _ada_table(c_all, ada_w, ada_b)[:, :B + 1].reshape(DEPTH, B + 1, 6, D)
    cos, sin = _rope_tables(N, TOKEN_TILE)

    xs = jnp.concatenate([x.reshape(T, D), ctx.reshape(TC, D)], axis=0)
    for i in range(DEPTH):
        kind, j = i % N_MIXERS, i // N_MIXERS
        update_ctx = any(l % N_MIXERS == 0 for l in range(i + 1, DEPTH))
        need_ctx = update_ctx or kind == 0
        mod = mod_all[i]
        n_in = T + TC if need_ctx else T
        n_out = T + TC if update_ctx else T
        if kind == 0:
            wq, wk, wv = (attn_w_qkv[j][:, :q_dim], attn_w_qkv[j][:, q_dim:q_dim + kv_dim],
                          attn_w_qkv[j][:, q_dim + kv_dim:])
            bq, bk, bv = (attn_b_qkv[j][:q_dim], attn_b_qkv[j][q_dim:q_dim + kv_dim],
                          attn_b_qkv[j][q_dim + kv_dim:])
            w = jnp.concatenate([wq, _doubled(wk, N_KV_HEADS), _doubled(wv, N_KV_HEADS)], axis=1).astype(BF16)
            b = jnp.concatenate([bq, _doubled(bk, N_KV_HEADS), _doubled(bv, N_KV_HEADS)])
            qkv = _norm_matmul(xs, norm_mix[i], mod[:, 0:2], w, b, n_rows=n_in, out_dtype=BF16,
                               rope=(cos, sin, q_dim + 2 * kv_dim, q_dim), **dims)
            o = _attention(qkv, attn_sinks[j], n_ctx=C, ctx_out=update_ctx, **dims)
            xs = _matmul_residual(o, attn_w_o[j].astype(BF16), attn_b_o[j], xs, mod[:, 2:3], n_rows=n_out, **dims)
        else:
            uc = _inproj(xs, norm_mix[i], mod[:, 0:2], hy_w_in[j].astype(BF16), hy_b_in[j], hy_conv_w[j],
                         hy_conv_b[j], n_rows=n_in, n_ctx=C, **dims)
            filt = (hy_f_w1[j], hy_f_b1[j], hy_f_w2[j], hy_f_b2[j], hy_f_w3[j], hy_f_b3[j], hy_f_freq[j],
                    hy_f_wout[j])
            z = _hyena_mixer(uc, hy_bias_d[j], filt, n_ctx=C, with_ctx=update_ctx, **dims)
            xs = _matmul_residual(z, hy_w_o[j].astype(BF16), hy_b_o[j], xs, mod[:, 2:3], n_rows=n_out, **dims)
        xs = _moe_layer(xs, i, mod, norm_ffn[i], moe_router_w[i], moe_router_b[i], moe_w_gu, moe_b_gu,
                        moe_w_down, moe_b_down, final_norm, n_rows=n_out, final=(i == DEPTH - 1), **dims)
    return xs.reshape(B, N, D)
```

```python
import functools
import math

import jax
import jax.numpy as jnp
import numpy as np
from jax import lax
from jax.experimental import pallas as pl
from jax.experimental.pallas import tpu as pltpu

DEPTH = 4
N_MIXERS = 2
GRID_W = 64
N_HEADS = 16
N_KV_HEADS = 4
HEAD_DIM = 64
GROUP = N_HEADS // N_KV_HEADS
WINDOW = 128
ROPE_THETA = 10000.0
HY_ORDER = 2
HY_SHORT = 3
HY_BANDS = 16
HY_TARGET = 1e-2
HY_FAST_DECAY = 0.3
HY_SLOW_DECAY = 1.5
N_EXPERTS = 32
TOP_K = 4
SWIGLU_LIMIT = 7.0
SWIGLU_ALPHA = 1.702
NORM_EPS = 1e-6

LANES = 128
OCT = 8
MINOR = 128
TOKEN_TILE = 512
Q_TILE = 128
KV_TOGETHER = 2
MOE_TILE = 512
VMEM_LIMIT = 56 * 1024 * 1024

F32 = jnp.float32
BF16 = jnp.bfloat16


def _dot(a, b):
    return jnp.dot(a, b, preferred_element_type=F32)


def _params(*sem):
    return pltpu.CompilerParams(dimension_semantics=sem, vmem_limit_bytes=VMEM_LIMIT)


def _ada_kernel(c_ref, w_ref, b_ref, o_ref):
    c = c_ref[...]
    s = c * (1.0 / (1.0 + jnp.exp(-c)))
    o_ref[...] = jnp.dot(s, w_ref[...], preferred_element_type=F32,
                         precision=lax.Precision.HIGHEST) + b_ref[...]


def _ada_table(c_all, ada_w, ada_b):
    r, d = c_all.shape
    depth, _, n6 = ada_w.shape
    tn = n6 // 4
    return pl.pallas_call(
        _ada_kernel,
        out_shape=jax.ShapeDtypeStruct((depth, r, n6), F32),
        grid=(depth, n6 // tn),
        in_specs=[pl.BlockSpec((r, d), lambda l, j: (0, 0)),
                  pl.BlockSpec((None, d, tn), lambda l, j: (l, 0, j)),
                  pl.BlockSpec((None, 1, tn), lambda l, j: (l, 0, j))],
        out_specs=pl.BlockSpec((None, r, tn), lambda l, j: (l, 0, j)),
        compiler_params=_params("arbitrary", "arbitrary"),
        name="ada_table",
    )(c_all, ada_w, ada_b.reshape(depth, 1, n6))


def _modulated(x, g, mod_ref, row):
    y = x * lax.rsqrt(jnp.mean(x * x, axis=-1, keepdims=True) + NORM_EPS) * g
    return y * (1.0 + mod_ref[row + 1:row + 2, :]) + mod_ref[row:row + 1, :]


def _mod_index(tm, n_seq, n_batch):
    return lambda i: (jnp.minimum(i * tm // n_seq, n_batch), 0, 0)


def _nm_kernel(x_ref, g_ref, mod_ref, w_ref, b_ref, cos_ref, sin_ref, o_ref, *, tn, rope_cols, q_cols):
    h = _modulated(x_ref[...], g_ref[...], mod_ref, 0).astype(BF16)
    n_out = o_ref.shape[1]
    if rope_cols:
        reps = tn // LANES
        cos = jnp.tile(cos_ref[...], (1, reps))
        sin = jnp.tile(sin_ref[...], (1, reps))
        lane = lax.broadcasted_iota(jnp.int32, (1, tn), 1)
        first_half = (lane & (HEAD_DIM - 1)) < (HEAD_DIM // 2)
    for n0 in range(0, n_out, tn):
        acc = _dot(h, w_ref[:, n0:n0 + tn]) + b_ref[:, n0:n0 + tn]
        if n0 < rope_cols:
            partner = jnp.where(first_half, pltpu.roll(acc, tn - HEAD_DIM // 2, 1),
                                pltpu.roll(acc, HEAD_DIM // 2, 1))
            acc = acc * cos + partner * sin
            if n0 < q_cols:
                acc = acc * (HEAD_DIM ** -0.5)
        o_ref[:, n0:n0 + tn] = acc.astype(o_ref.dtype)


def _norm_matmul(xs, gain, mod, w, bias, *, n_rows, n_seq, n_batch, out_dtype, rope=None):
    d = xs.shape[1]
    n_out = w.shape[1]
    tm, tn = TOKEN_TILE, 512
    if rope is None:
        cos = jnp.zeros((tm, LANES), F32)
        sin = cos
        rope_cols = q_cols = 0
        rope_index = lambda i: (0, 0)
    else:
        cos, sin, rope_cols, q_cols = rope
        seq_tiles = n_seq // tm
        lat_tiles = n_seq * n_batch // tm
        rope_index = lambda i: (jnp.where(i < lat_tiles, i % seq_tiles, seq_tiles), 0)
    kern = functools.partial(_nm_kernel, tn=tn, rope_cols=rope_cols, q_cols=q_cols)
    return pl.pallas_call(
        kern,
        out_shape=jax.ShapeDtypeStruct((n_rows, n_out), out_dtype),
        grid=(n_rows // tm,),
        in_specs=[pl.BlockSpec((tm, d), lambda i: (i, 0)),
                  pl.BlockSpec((1, d), lambda i: (0, 0)),
                  pl.BlockSpec((None, 2, d), _mod_index(tm, n_seq, n_batch)),
                  pl.BlockSpec((d, n_out), lambda i: (0, 0)),
                  pl.BlockSpec((1, n_out), lambda i: (0, 0)),
                  pl.BlockSpec((tm, LANES), rope_index),
                  pl.BlockSpec((tm, LANES), rope_index)],
        out_specs=pl.BlockSpec((tm, n_out), lambda i: (i, 0)),
        compiler_params=_params("arbitrary"),
        name="norm_matmul",
    )(xs, gain.reshape(1, d), mod, w, bias.reshape(1, n_out), cos, sin)


def _res_kernel(a_ref, w_ref, b_ref, x_ref, gate_ref, o_ref):
    y = _dot(a_ref[...].astype(BF16), w_ref[...]) + b_ref[...]
    o_ref[...] = x_ref[...] + gate_ref[...] * y


def _matmul_residual(a, w, bias, xs, gate, *, n_rows, n_seq, n_batch):
    k = a.shape[1]
    d = w.shape[1]
    tm = TOKEN_TILE
    return pl.pallas_call(
        _res_kernel,
        out_shape=jax.ShapeDtypeStruct((n_rows, d), F32),
        grid=(n_rows // tm,),
        in_specs=[pl.BlockSpec((tm, k), lambda i: (i, 0)),
                  pl.BlockSpec((k, d), lambda i: (0, 0)),
                  pl.BlockSpec((1, d), lambda i: (0, 0)),
                  pl.BlockSpec((tm, d), lambda i: (i, 0)),
                  pl.BlockSpec((None, 1, d), _mod_index(tm, n_seq, n_batch))],
        out_specs=pl.BlockSpec((tm, d), lambda i: (i, 0)),
        compiler_params=_params("arbitrary"),
        name="matmul_residual",
    )(a, w, bias.reshape(1, d), xs, gate)


def _attend_heads(q_ref, work, sink_ref, o_ref):
    tq = q_ref.shape[0]
    lane = lax.broadcasted_iota(jnp.int32, (tq, LANES), 1)
    low = lane < HEAD_DIM
    row = lax.broadcasted_iota(jnp.int32, (2 * tq, 1), 0)
    lane_tiles = lambda a: [a[:, t:t + LANES] for t in range(0, a.shape[1], LANES)]
    items = [(kvh * GROUP + 2 * j, segments) for kvh, segments in work for j in range(GROUP // 2)]
    sinks = [jnp.where(row < tq, sink_ref[head], sink_ref[head + 1]) for head, _ in items]
    scores, maxes = [], []
    for (head, segments), sink in zip(items, sinks):
        qp = q_ref[:, head * HEAD_DIM:head * HEAD_DIM + LANES]
        zero = jnp.zeros_like(qp)
        q2 = jnp.concatenate([jnp.where(low, qp, zero), jnp.where(low, zero, qp)], axis=0)
        sj, folded = [], None
        for k, _, mask in segments:
            s = lax.dot_general(q2, k, (((1,), (1,)), ((), ())), preferred_element_type=F32)
            if mask is not None:
                s = jnp.where(mask, s, -jnp.inf)
            sj.append(s)
            for t in lane_tiles(s):
                folded = t if folded is None else jnp.maximum(folded, t)
        scores.append(sj)
        maxes.append(jnp.maximum(sink, jnp.max(folded, axis=-1, keepdims=True)))
    for (head, segments), sink, sj, m in zip(items, sinks, scores, maxes):
        o2, folded = None, None
        for s, (_, v, _) in zip(sj, segments):
            p = jnp.exp(s - m)
            for t in lane_tiles(p):
                folded = t if folded is None else folded + t
            pv = _dot(p.astype(BF16), v)
            o2 = pv if o2 is None else o2 + pv
        o2 = o2 / (jnp.exp(sink - m) + jnp.sum(folded, axis=-1, keepdims=True))
        o_ref[:, head * HEAD_DIM:head * HEAD_DIM + LANES] = jnp.where(low, o2[:tq], o2[tq:]).astype(o_ref.dtype)


def _attn_kernel(sink_ref, q_ref, kp_ref, kc_ref, kn_ref, vp_ref, vc_ref, vn_ref, kx_ref, vx_ref, o_ref,
                 *, n_q_blocks):
    qi = pl.program_id(1)
    tq = q_ref.shape[0]
    head_cols = lambda kvh: slice(kvh * LANES, (kvh + 1) * LANES)

    @pl.when(qi < n_q_blocks)
    def _():
        r = lax.broadcasted_iota(jnp.int32, (2 * tq, tq), 0) & (tq - 1)
        c = lax.broadcasted_iota(jnp.int32, (2 * tq, tq), 1)
        prev_ok = (c >= r) & (qi >= 1)
        next_ok = (c <= r) & (qi <= n_q_blocks - 2)
        for kv0 in range(0, N_KV_HEADS, KV_TOGETHER):
            work = []
            for kvh in range(kv0, kv0 + KV_TOGETHER):
                cs = head_cols(kvh)
                work.append((kvh, [(kc_ref[:, cs], vc_ref[:, cs], None), (kx_ref[:, cs], vx_ref[:, cs], None),
                                   (kp_ref[:, cs], vp_ref[:, cs], prev_ok), (kn_ref[:, cs], vn_ref[:, cs], next_ok)]))
            _attend_heads(q_ref, work, sink_ref, o_ref)

    @pl.when(qi >= n_q_blocks)
    def _():
        for kv0 in range(0, N_KV_HEADS, KV_TOGETHER):
            work = [(kvh, [(kx_ref[:, head_cols(kvh)], vx_ref[:, head_cols(kvh)], None)])
                    for kvh in range(kv0, kv0 + KV_TOGETHER)]
            _attend_heads(q_ref, work, sink_ref, o_ref)


def _attention(qkv, sinks, *, n_seq, n_batch, n_ctx, ctx_out):
    tq = Q_TILE
    assert tq == WINDOW
    nq = n_seq // tq
    ncq = n_ctx // tq if ctx_out else 0
    lat_blocks = n_batch * nq
    q_dim = N_HEADS * HEAD_DIM
    kv_w = N_KV_HEADS * LANES
    k_col = q_dim // kv_w
    v_col = k_col + 1
    ctx_row0 = n_batch * n_seq // n_ctx

    def q_index(b, i, s):
        return (jnp.where(i < nq, b * nq + i, lat_blocks + b * (n_ctx // tq) + (i - nq)), 0)

    def kv_index(off, col):
        def index(b, i, s):
            return (b * nq + jnp.clip(i + off, 0, nq - 1), col)
        return index

    n_rows = n_batch * n_seq + (n_batch * n_ctx if ctx_out else 0)
    grid_spec = pltpu.PrefetchScalarGridSpec(
        num_scalar_prefetch=1,
        grid=(n_batch, nq + ncq),
        in_specs=[pl.BlockSpec((tq, q_dim), q_index),
                  pl.BlockSpec((tq, kv_w), kv_index(-1, k_col)),
                  pl.BlockSpec((tq, kv_w), kv_index(0, k_col)),
                  pl.BlockSpec((tq, kv_w), kv_index(1, k_col)),
                  pl.BlockSpec((tq, kv_w), kv_index(-1, v_col)),
                  pl.BlockSpec((tq, kv_w), kv_index(0, v_col)),
                  pl.BlockSpec((tq, kv_w), kv_index(1, v_col)),
                  pl.BlockSpec((n_ctx, kv_w), lambda b, i, s: (ctx_row0 + b, k_col)),
                  pl.BlockSpec((n_ctx, kv_w), lambda b, i, s: (ctx_row0 + b, v_col))],
        out_specs=pl.BlockSpec((tq, q_dim), q_index),
    )
    return pl.pallas_call(
        functools.partial(_attn_kernel, n_q_blocks=nq),
        out_shape=jax.ShapeDtypeStruct((n_rows, q_dim), BF16),
        grid_spec=grid_spec,
        compiler_params=_params("arbitrary", "arbitrary"),
        name="attention",
    )(sinks, qkv, qkv, qkv, qkv, qkv, qkv, qkv, qkv, qkv)


def _pack_pairs(x):
    half = x.shape[1] // 2
    bits = lax.bitcast_convert_type(x.astype(BF16).astype(F32), jnp.int32)
    return lax.shift_right_logical(bits[:, :half], 16) | (bits[:, half:] & -65536)


def _unpack_pairs(u):
    return (lax.bitcast_convert_type(lax.shift_left(u, 16), F32),
            lax.bitcast_convert_type(u & -65536, F32))


def _router_kernel(x_ref, g_ref, mod_ref, w_ref, b_ref, tri_ref, tok_ref, route_ref, count_ref, seen):
    @pl.when(pl.program_id(0) == 0)
    def _():
        seen[...] = jnp.zeros_like(seen)

    h = _modulated(x_ref[...], g_ref[...], mod_ref, 0)
    tok_ref[...] = _pack_pairs(h)
    h_hi = h.astype(BF16)
    h_lo = (h - h_hi.astype(F32)).astype(BF16)
    logits = (_dot(h_hi, w_ref[0]) + (_dot(h_hi, w_ref[1]) + _dot(h_lo, w_ref[0]))) + b_ref[...]
    lane = lax.broadcasted_iota(jnp.int32, logits.shape, 1)
    rest = jnp.where(lane < N_EXPERTS, logits, -jnp.inf)
    chosen = jnp.zeros(logits.shape, F32)
    top_v, top_i = [], []
    for _ in range(TOP_K):
        best = jnp.max(rest, axis=-1, keepdims=True)
        idx = jnp.min(jnp.where(rest == best, lane, LANES), axis=-1, keepdims=True)
        hit = lane == idx
        top_v.append(best)
        top_i.append(idx)
        rest = jnp.where(hit, -jnp.inf, rest)
        chosen = jnp.where(hit, 1.0, chosen)
    weights = [jnp.exp(v - top_v[0]) for v in top_v]
    denom = (weights[0] + weights[1]) + (weights[2] + weights[3])
    earlier = _dot(tri_ref[...], chosen.astype(BF16)) + seen[...]
    seen[...] = seen[...] + jnp.sum(chosen, axis=0, keepdims=True)
    route = jnp.zeros(logits.shape, F32)
    for k in range(TOP_K):
        rank = jnp.sum(jnp.where(lane == top_i[k], earlier, 0.0), axis=-1, keepdims=True)
        route = jnp.where(lane == k, top_i[k].astype(F32), route)
        route = jnp.where(lane == TOP_K + k, weights[k] / denom, route)
        route = jnp.where(lane == 2 * TOP_K + k, rank, route)
    route_ref[...] = route
    count_ref[...] = jnp.broadcast_to(seen[...], count_ref.shape)


def _router(xs, gain, mod, w, bias, *, n_rows, n_seq, n_batch):
    d = xs.shape[1]
    tm = TOKEN_TILE
    tri = jnp.asarray(np.tril(np.ones((tm, tm)), -1), BF16)
    return pl.pallas_call(
        _router_kernel,
        out_shape=(jax.ShapeDtypeStruct((n_rows, d // 2), jnp.int32), jax.ShapeDtypeStruct((n_rows, LANES), F32),
                   jax.ShapeDtypeStruct((OCT, LANES), F32)),
        grid=(n_rows // tm,),
        in_specs=[pl.BlockSpec((tm, d), lambda i: (i, 0)),
                  pl.BlockSpec((1, d), lambda i: (0, 0)),
                  pl.BlockSpec((None, 2, d), _mod_index(tm, n_seq, n_batch)),
                  pl.BlockSpec((2, d, LANES), lambda i: (0, 0, 0)),
                  pl.BlockSpec((1, LANES), lambda i: (0, 0)),
                  pl.BlockSpec((tm, tm), lambda i: (0, 0))],
        out_specs=(pl.BlockSpec((tm, d // 2), lambda i: (i, 0)), pl.BlockSpec((tm, LANES), lambda i: (i, 0)),
                   pl.BlockSpec((OCT, LANES), lambda i: (0, 0))),
        scratch_shapes=[pltpu.VMEM((1, LANES), F32)],
        compiler_params=_params("arbitrary"),
        name="router",
    )(xs, gain.reshape(1, d), mod, w, bias, tri)


def _expert_kernel(be_ref, nb_ref, x_ref, wgu_ref, bgu_ref, wd_ref, bd_ref, o_ref, wgu_bf, wd_bf):
    i = pl.program_id(0)
    fresh = jnp.logical_or(i == 0, be_ref[i] != be_ref[jnp.maximum(i - 1, 0)])

    @pl.when(jnp.logical_and(fresh, i < nb_ref[0]))
    def _():
        wgu_bf[...] = wgu_ref[...].astype(BF16)
        wd_bf[...] = wd_ref[...].astype(BF16)

    @pl.when(i < nb_ref[0])
    def _():
        d_e = wd_ref.shape[0]
        x = jnp.concatenate(_unpack_pairs(x_ref[...]), axis=1).astype(BF16)
        g = jnp.minimum(_dot(x, wgu_bf[:, :d_e]) + bgu_ref[:, :d_e], SWIGLU_LIMIT)
        up = jnp.clip(_dot(x, wgu_bf[:, d_e:]) + bgu_ref[:, d_e:], -SWIGLU_LIMIT, SWIGLU_LIMIT)
        act = g * (1.0 / (1.0 + jnp.exp(-SWIGLU_ALPHA * g))) * (up + 1.0)
        o_ref[...] = _pack_pairs(_dot(act.astype(BF16), wd_bf[...]) + bd_ref[...])

    @pl.when(i >= nb_ref[0])
    def _():
        o_ref[...] = jnp.zeros_like(o_ref)


def _experts(xs, block_e, n_used, w_gu, b_gu, w_down, b_down, layer):
    r = xs.shape[0]
    n_e, d, d_gu = w_gu.shape[1:]
    tm = MOE_TILE
    grid_spec = pltpu.PrefetchScalarGridSpec(
        num_scalar_prefetch=2,
        grid=(r // tm,),
        in_specs=[pl.BlockSpec((tm, d // 2), lambda i, be, nb: (i, 0)),
                  pl.BlockSpec((None, None, d, d_gu), lambda i, be, nb: (layer, be[i], 0, 0)),
                  pl.BlockSpec((None, None, 1, d_gu), lambda i, be, nb: (layer, be[i], 0, 0)),
                  pl.BlockSpec((None, None, d_gu // 2, d), lambda i, be, nb: (layer, be[i], 0, 0)),
                  pl.BlockSpec((None, None, 1, d), lambda i, be, nb: (layer, be[i], 0, 0))],
        out_specs=pl.BlockSpec((tm, d // 2), lambda i, be, nb: (i, 0)),
        scratch_shapes=[pltpu.VMEM((d, d_gu), BF16), pltpu.VMEM((d_gu // 2, d), BF16)],
    )
    return pl.pallas_call(
        _expert_kernel,
        out_shape=jax.ShapeDtypeStruct((r, d // 2), jnp.int32),
        grid_spec=grid_spec,
        compiler_params=_params("arbitrary"),
        name="experts",
    )(block_e, n_used, xs, w_gu, b_gu.reshape(b_gu.shape[0], n_e, 1, d_gu), w_down,
      b_down.reshape(b_down.shape[0], n_e, 1, d))


def _combine_kernel(x_ref, y0_ref, y1_ref, y2_ref, y3_ref, route_ref, gate_ref, g_ref, o_ref, *, final):
    w = [route_ref[:, TOP_K + k:TOP_K + k + 1] for k in range(TOP_K)]
    y = [_unpack_pairs(r[...]) for r in (y0_ref, y1_ref, y2_ref, y3_ref)]
    f = jnp.concatenate([(w[0] * y[0][h] + w[1] * y[1][h]) + (w[2] * y[2][h] + w[3] * y[3][h]) for h in range(2)],
                        axis=1)
    x = x_ref[...] + gate_ref[...] * f
    if final:
        x = x * lax.rsqrt(jnp.mean(x * x, axis=-1, keepdims=True) + NORM_EPS) * g_ref[...]
    o_ref[...] = x


def _combine(xs, ys, route, gate, final_gain, *, n_rows, n_seq, n_batch, final):
    d = xs.shape[1]
    tm = TOKEN_TILE
    row_spec = pl.BlockSpec((tm, d), lambda i: (i, 0))
    return pl.pallas_call(
        functools.partial(_combine_kernel, final=final),
        out_shape=jax.ShapeDtypeStruct((n_rows, d), F32),
        grid=(n_rows // tm,),
        in_specs=[row_spec] + [pl.BlockSpec((tm, d // 2), lambda i: (i, 0))] * TOP_K
                 + [pl.BlockSpec((tm, LANES), lambda i: (i, 0)),
                                             pl.BlockSpec((None, 1, d), _mod_index(tm, n_seq, n_batch)),
                                             pl.BlockSpec((1, d), lambda i: (0, 0))],
        out_specs=row_spec,
        compiler_params=_params("arbitrary"),
        name="combine",
    )(xs, *ys, route, gate, final_gain.reshape(1, d))


def _moe_layer(xs, layer, mod, norm_g, router_w, router_b, w_gu, b_gu, w_down, b_down, final_gain,
               *, n_rows, n_seq, n_batch, final):
    d = xs.shape[1]
    rw = jnp.zeros((d, LANES), F32).at[:, :N_EXPERTS].set(router_w)
    rw_hi = rw.astype(BF16)
    rw = jnp.stack([rw_hi, (rw - rw_hi.astype(F32)).astype(BF16)])
    rb = jnp.zeros((1, LANES), F32).at[0, :N_EXPERTS].set(router_b)
    tok, route, count = _router(xs, norm_g, mod[:, 3:5], rw, rb, n_rows=n_rows, n_seq=n_seq, n_batch=n_batch)
    top_i = route[:, :TOP_K].astype(jnp.int32)
    rank = route[:, 2 * TOP_K:3 * TOP_K].astype(jnp.int32)
    counts = count[0, :N_EXPERTS].astype(jnp.int32)
    n_assign = n_rows * TOP_K
    padded = (counts + MOE_TILE - 1) // MOE_TILE * MOE_TILE
    pad_end = jnp.cumsum(padded)
    pad_start = pad_end - padded
    start = jnp.cumsum(counts) - counts
    n_blocks = -(-n_assign // MOE_TILE) + N_EXPERTS
    n_slots = n_blocks * MOE_TILE
    block_e = jnp.minimum(jnp.sum(pad_end[None, :] <= (jnp.arange(n_blocks) * MOE_TILE)[:, None], axis=1),
                          N_EXPERTS - 1).astype(jnp.int32)
    n_used = (pad_end[-1] // MOE_TILE).astype(jnp.int32).reshape(1)
    slot_of = pad_start[top_i] + rank
    _, by_slot = lax.sort_key_val(slot_of.reshape(-1), jnp.arange(n_assign, dtype=jnp.int32), is_stable=False)
    slot_e = jnp.repeat(block_e, MOE_TILE)
    within = jnp.arange(n_slots, dtype=jnp.int32) - pad_start[slot_e]
    src = by_slot[jnp.clip(start[slot_e] + within, 0, n_assign - 1)]
    ys = _experts(tok[src // TOP_K], block_e, n_used, w_gu, b_gu, w_down, b_down, layer)
    parts = [ys[slot_of[:, k]] for k in range(TOP_K)]
    return _combine(xs, parts, route, mod[:, 5:6], final_gain, n_rows=n_rows, n_seq=n_seq, n_batch=n_batch,
                    final=final)


def _hyena_spectrum(L, consts, w1, b1, w2, b2, w3, b3, freq, w_out, d):
    t = jnp.linspace(0.0, 1.0, L, dtype=F32)[:, None]
    w = 2.0 * math.pi * jnp.arange(L, dtype=F32)[:, None] / L
    bands = jnp.linspace(1e-4, HY_BANDS - 1, HY_BANDS, dtype=F32)[None, :]
    z = jnp.concatenate([t, jnp.cos(bands * w), -jnp.sin(bands * w)], axis=-1)
    a = jnp.sin(freq * (z @ w1 + b1))
    a = jnp.sin(freq * (a @ w2 + b2))
    a = jnp.sin(freq * (a @ w3 + b3))
    hf = (a @ w_out).reshape(L, 2, HY_ORDER, d)
    deltas = jnp.linspace(math.log(HY_TARGET) / HY_SLOW_DECAY, math.log(HY_TARGET) / HY_FAST_DECAY, d, dtype=F32)
    hf = hf * jnp.exp(-t * jnp.abs(deltas))[:, None, None, :]
    return _filter_spectrum(hf.reshape(L // OCT, OCT, 2 * HY_ORDER * d), consts, d)


def _filter_kernel(f_ref, b_ref, kgf_ref, fm_ref, twc_ref, tws_ref, spec_ref, a_scr):
    n_k = kgf_ref.shape[0] // (2 * OCT)
    n_a = n_k - 1
    n_bo = MINOR // OCT
    dt = f_ref.shape[2]
    kc = _major_chunk(n_k)
    cr_rows = kc * OCT
    total = jnp.zeros((1, 2 * dt), F32)
    for bo in range(n_bo):
        rows = pl.ds(bo, n_a, stride=n_bo)
        x2 = jnp.concatenate([f_ref[rows].reshape(n_a * OCT, dt), b_ref[rows].reshape(n_a * OCT, dt)], axis=1)
        total = total + jnp.sum(jnp.abs(x2), axis=0, keepdims=True)
        p = _dot(kgf_ref[...], x2.astype(BF16))
        for ch in range(n_k // kc):
            r0 = 2 * ch * cr_rows
            ks = slice(ch * kc, (ch + 1) * kc)
            tw_rows = slice(ch * cr_rows, (ch + 1) * cr_rows)
            ar, ai = p[r0:r0 + cr_rows], p[r0 + cr_rows:r0 + 2 * cr_rows]
            c, s = twc_ref[tw_rows, bo:bo + 1], tws_ref[tw_rows, bo:bo + 1]
            a_scr[ks, 0, bo] = (ar * c + ai * s).reshape(kc, OCT, 2 * dt)
            a_scr[ks, 1, bo] = (ai * c - ar * s).reshape(kc, OCT, 2 * dt)
    b0 = b_ref[0, 0:1, :]
    inv = 1.0 / (total[:, :dt] + total[:, dt:] - jnp.abs(b0))
    unroll = max(u for u in range(1, 12) if n_k % u == 0)

    def minor_stage(step, carry):
        for j in range(unroll):
            k = step * unroll + j
            x = _dot(fm_ref[...], a_scr[k].reshape(2 * MINOR, 2 * dt).astype(BF16))
            spec_ref[k, 0] = ((x[:MINOR, :dt] + x[:MINOR, dt:] - b0) * inv).astype(spec_ref.dtype)
            spec_ref[k, 1] = ((x[MINOR:, :dt] - x[MINOR:, dt:]) * inv).astype(spec_ref.dtype)
        return carry

    lax.fori_loop(0, n_k // unroll, minor_stage, 0)
    for k in range(n_k, spec_ref.shape[0]):
        spec_ref[k] = jnp.zeros(spec_ref.shape[1:], spec_ref.dtype)


def _filter_spectrum(hf, consts, d):
    kgf, _, fm, _, twc, tws = consts
    dt = LANES
    seq_blk = hf.shape[0]
    n_k = kgf.shape[0] // (2 * OCT)
    n_kp = n_k + n_k % 2
    tiles = d // dt
    const_spec = lambda arr: pl.BlockSpec(arr.shape, lambda o, j: (0, 0))
    return pl.pallas_call(
        _filter_kernel,
        out_shape=jax.ShapeDtypeStruct((HY_ORDER, n_kp, 2, MINOR, d), BF16),
        grid=(HY_ORDER, tiles),
        in_specs=[pl.BlockSpec((seq_blk, OCT, dt), lambda o, j: (0, 0, o * tiles + j)),
                  pl.BlockSpec((seq_blk, OCT, dt), lambda o, j: (0, 0, (HY_ORDER + o) * tiles + j)),
                  const_spec(kgf), const_spec(fm), const_spec(twc), const_spec(tws)],
        out_specs=pl.BlockSpec((None, n_kp, 2, MINOR, dt), lambda o, j: (o, 0, 0, 0, j)),
        scratch_shapes=[pltpu.VMEM((n_k, 2, MINOR // OCT, OCT, 2 * dt), F32)],
        compiler_params=_params("arbitrary", "arbitrary"),
        name="hyena_filter",
    )(hf, hf, kgf, fm, twc, tws)


def _major_chunk(n_k):
    return max(c for c in range(1, 14) if n_k % c == 0)


def _dft_constants(L):
    n = 2 * L
    n1 = n // MINOR
    n_a, n_k = n1 // 2, n1 // 2 + 1
    k = np.arange(n_k)[:, None]
    a = np.arange(n_a)[None, :]
    th = 2.0 * np.pi * k * a / n1
    eye = np.eye(OCT)
    kgf = np.concatenate([np.kron(np.cos(th), eye), np.kron(-np.sin(th), eye)], axis=0)
    w = np.where((k == 0) | (k == n_a), 1.0, 2.0) / n
    kgi = np.concatenate([np.kron((np.cos(th) * w).T, eye), np.kron((-np.sin(th) * w).T, eye)], axis=1)
    b = np.arange(MINOR)
    ph = 2.0 * np.pi * np.outer(b, b) / MINOR
    fr, fi = np.cos(ph), -np.sin(ph)
    fm = np.block([[fr, -fi], [fi, fr]])
    fmi = np.block([[fr, fi], [-fi, fr]])
    bb = OCT * np.arange(MINOR // OCT)[None, None, :] + np.arange(OCT)[None, :, None]
    tw = 2.0 * np.pi * np.arange(n_k)[:, None, None] * bb / n
    pad = ((0, 0), (0, LANES - MINOR // OCT))
    twc = np.pad(np.cos(tw).reshape(n_k * OCT, -1), pad)
    tws = np.pad(np.sin(tw).reshape(n_k * OCT, -1), pad)
    kc = _major_chunk(n_k)
    kgf = kgf.reshape(2, n_k // kc, kc * OCT, -1).transpose(1, 0, 2, 3).reshape(2 * n_k * OCT, -1)
    kgi = kgi.reshape(-1, 2, n_k // kc, kc * OCT).transpose(0, 2, 1, 3).reshape(-1, 2 * n_k * OCT)
    first_im = kc * OCT
    last_im = 2 * n_k * OCT - OCT
    kgi = np.delete(kgi, np.r_[first_im:first_im + OCT, last_im:last_im + OCT], axis=1)
    return (jnp.asarray(kgf, BF16), jnp.asarray(kgi, BF16), jnp.asarray(fm, BF16), jnp.asarray(fmi, BF16),
            jnp.asarray(twc, F32), jnp.asarray(tws, F32))


def _hyena_kernel(z_ref, gate_ref, spec_ref, d_ref, kgf_ref, kgi_ref, fm_ref, fmi_ref, twc_ref, tws_ref, *rest):
    o_ref, a_scr = rest[-2], rest[-1]
    n_k = kgf_ref.shape[0] // (2 * OCT)
    n_a = n_k - 1
    n_bo = MINOR // OCT
    dt = z_ref.shape[2]
    kc = _major_chunk(n_k)
    cr_rows = kc * OCT
    side_by_side = lambda u, v: jnp.concatenate([u, v], axis=1)

    if a_scr.shape[0] > n_k:
        a_scr[n_k] = jnp.zeros(a_scr.shape[1:], F32)

    for bo in range(0, n_bo, 2):
        x2 = side_by_side(*[z_ref[pl.ds(bo + h, n_a, stride=n_bo)].reshape(n_a * OCT, dt) for h in range(2)])
        p = _dot(kgf_ref[...], x2.astype(BF16))
        for ch in range(n_k // kc):
            r0 = 2 * ch * cr_rows
            ks = slice(ch * kc, (ch + 1) * kc)
            tw_rows = slice(ch * cr_rows, (ch + 1) * cr_rows)
            for h in range(2):
                cols = slice(h * dt, (h + 1) * dt)
                ar, ai = p[r0:r0 + cr_rows, cols], p[r0 + cr_rows:r0 + 2 * cr_rows, cols]
                c, s = twc_ref[tw_rows, bo + h:bo + h + 1], tws_ref[tw_rows, bo + h:bo + h + 1]
                a_scr[ks, 0, bo + h] = (ar * c + ai * s).reshape(kc, OCT, dt)
                a_scr[ks, 1, bo + h] = (ai * c - ar * s).reshape(kc, OCT, dt)

    n_pairs = a_scr.shape[0] // 2
    unroll = max(u for u in range(1, 12) if n_pairs % u == 0)

    def minor_stage(step, carry):
        for j in range(unroll):
            k = 2 * (step * unroll + j)
            b2 = side_by_side(a_scr[k].reshape(2 * MINOR, dt), a_scr[k + 1].reshape(2 * MINOR, dt))
            x = _dot(fm_ref[...], b2.astype(BF16))
            xr, xi = x[:MINOR], x[MINOR:]
            gr = side_by_side(spec_ref[k, 0], spec_ref[k + 1, 0]).astype(F32)
            gi = side_by_side(spec_ref[k, 1], spec_ref[k + 1, 1]).astype(F32)
            y = jnp.concatenate([xr * gr - xi * gi, xr * gi + xi * gr], axis=0).astype(BF16)
            c2 = _dot(fmi_ref[...], y)
            a_scr[k] = c2[:, :dt].reshape(2, n_bo, OCT, dt)
            a_scr[k + 1] = c2[:, dt:].reshape(2, n_bo, OCT, dt)
        return carry

    lax.fori_loop(0, n_pairs // unroll, minor_stage, 0)

    for bo in range(0, n_bo, 2):
        halves = []
        for h in range(2):
            pieces = []
            for ch in range(n_k // kc):
                ks = slice(ch * kc, (ch + 1) * kc)
                tw_rows = slice(ch * cr_rows, (ch + 1) * cr_rows)
                cr = a_scr[ks, 0, bo + h].reshape(cr_rows, dt)
                ci = a_scr[ks, 1, bo + h].reshape(cr_rows, dt)
                c, s = twc_ref[tw_rows, bo + h:bo + h + 1], tws_ref[tw_rows, bo + h:bo + h + 1]
                im = cr * s + ci * c
                im = im[OCT if ch == 0 else 0:cr_rows - OCT if ch == n_k // kc - 1 else cr_rows]
                pieces += [cr * c - ci * s, im]
            halves.append(jnp.concatenate(pieces, axis=0))
        conv = _dot(kgi_ref[...], side_by_side(*halves).astype(BF16))
        for h in range(2):
            rows = pl.ds(bo + h, n_a, stride=n_bo)
            zin = z_ref[rows].reshape(n_a * OCT, dt)
            out = gate_ref[rows].reshape(n_a * OCT, dt) * (conv[:, h * dt:(h + 1) * dt] + zin * d_ref[...])
            o_ref[rows] = out.reshape(n_a, OCT, dt)


def _hyena_conv(zin, zin_col0, gate, gate_col0, spec, bias_d, consts, *, seq_len, n_seqs, row0, out_rows, prev_out,
                order=0):
    d = spec.shape[-1]
    dt = LANES
    n_k = spec.shape[1]
    seq_blk = seq_len // OCT
    blk0 = row0 // seq_len
    as_oct = lambda arr: arr.reshape(arr.shape[0] // OCT, OCT, arr.shape[1])
    const_spec = lambda arr: pl.BlockSpec(arr.shape, lambda j, b: (0, 0))
    in_specs = [pl.BlockSpec((seq_blk, OCT, dt), lambda j, b: (blk0 + b, 0, zin_col0 // dt + j)),
                pl.BlockSpec((seq_blk, OCT, dt), lambda j, b: (blk0 + b, 0, gate_col0 // dt + j)),
                pl.BlockSpec((None, n_k, 2, MINOR, dt), lambda j, b: (order, 0, 0, 0, j)),
                pl.BlockSpec((None, 1, dt), lambda j, b: (order, 0, j))] + [const_spec(cst) for cst in consts]
    args = [as_oct(zin), as_oct(gate), spec, bias_d.reshape(bias_d.shape[0], 1, d), *consts]
    aliases = {}
    if prev_out is not None:
        in_specs.append(pl.BlockSpec(memory_space=pl.ANY))
        aliases = {len(args): 0}
        args.append(as_oct(prev_out))
    out = pl.pallas_call(
        _hyena_kernel,
        out_shape=jax.ShapeDtypeStruct((out_rows // OCT, OCT, d), F32),
        grid=(d // dt, n_seqs),
        in_specs=in_specs,
        out_specs=pl.BlockSpec((seq_blk, OCT, dt), lambda j, b: (blk0 + b, 0, j)),
        scratch_shapes=[pltpu.VMEM((n_k, 2, MINOR // OCT, OCT, dt), F32)],
        input_output_aliases=aliases,
        compiler_params=_params("arbitrary", "arbitrary"),
        name="hyena_conv",
    )(*args)
    return out.reshape(out_rows, d)


def _hyena_mixer(uc, bias_d, filt, *, n_seq, n_batch, n_ctx, with_ctx):
    d = uc.shape[1] // 3
    rows = uc.shape[0]
    groups = [(n_seq, n_batch, 0)] + ([(n_ctx, n_batch, n_seq * n_batch)] if with_ctx else [])
    consts = [_dft_constants(L) for L, _, _ in groups]
    specs = [_hyena_spectrum(L, cst, *filt, d) for (L, _, _), cst in zip(groups, consts)]
    z = None
    for o in range(HY_ORDER):
        zin, zin_col0 = (uc, 2 * d) if o == 0 else (z, 0)
        out = None
        for (L, n_seqs, row0), spec, cst in zip(groups, specs, consts):
            out = _hyena_conv(zin, zin_col0, uc, o * d, spec, bias_d, cst, seq_len=L, n_seqs=n_seqs, row0=row0,
                              out_rows=rows, prev_out=out, order=o)
        z = out
    return z


def _inproj_kernel(x_ref, xp_ref, xn_ref, g_ref, mod_ref, w_ref, b_ref, cw_ref, cb_ref, o_ref,
                   *, tn, n_seq, n_ctx, lat_tiles):
    i = pl.program_id(0)
    tm = x_ref.shape[0]
    x = jnp.concatenate([xp_ref[...], x_ref[...], xn_ref[...]], axis=0)
    h = _modulated(x, g_ref[...], mod_ref, 0).astype(BF16)
    row = lax.broadcasted_iota(jnp.int32, (tm, 1), 0)
    pos = jnp.where(i < lat_tiles, (i * tm) % n_seq + row, row & (n_ctx - 1))
    last = jnp.where(i < lat_tiles, n_seq - 1, n_ctx - 1)
    has_prev = pos != 0
    has_next = pos != last
    for n0 in range(0, o_ref.shape[1], tn):
        cols = slice(n0, n0 + tn)
        u = _dot(h, w_ref[:, cols]) + b_ref[:, cols]
        prev = jnp.where(has_prev, u[OCT - 1:OCT - 1 + tm], 0.0)
        nxt = jnp.where(has_next, u[OCT + 1:OCT + 1 + tm], 0.0)
        o_ref[:, cols] = (cb_ref[:, cols] + prev * cw_ref[0:1, cols] + u[OCT:OCT + tm] * cw_ref[1:2, cols]
                          + nxt * cw_ref[2:3, cols])


def _inproj(xs, gain, mod, w, bias, conv_w, conv_b, *, n_rows, n_seq, n_batch, n_ctx):
    d = xs.shape[1]
    n_out = w.shape[1]
    tm, tn = TOKEN_TILE, 512
    assert n_seq % tm == 0 and tm % n_ctx == 0 and n_ctx & (n_ctx - 1) == 0
    halo = tm // OCT
    last_halo = xs.shape[0] // OCT - 1
    kern = functools.partial(_inproj_kernel, tn=tn, n_seq=n_seq, n_ctx=n_ctx, lat_tiles=n_seq * n_batch // tm)
    return pl.pallas_call(
        kern,
        out_shape=jax.ShapeDtypeStruct((n_rows, n_out), F32),
        grid=(n_rows // tm,),
        in_specs=[pl.BlockSpec((tm, d), lambda i: (i, 0)),
                  pl.BlockSpec((OCT, d), lambda i: (jnp.maximum(i * halo - 1, 0), 0)),
                  pl.BlockSpec((OCT, d), lambda i: (jnp.minimum((i + 1) * halo, last_halo), 0)),
                  pl.BlockSpec((1, d), lambda i: (0, 0)),
                  pl.BlockSpec((None, 2, d), _mod_index(tm, n_seq, n_batch)),
                  pl.BlockSpec((d, n_out), lambda i: (0, 0)),
                  pl.BlockSpec((1, n_out), lambda i: (0, 0)),
                  pl.BlockSpec((HY_SHORT, n_out), lambda i: (0, 0)),
                  pl.BlockSpec((1, n_out), lambda i: (0, 0))],
        out_specs=pl.BlockSpec((tm, n_out), lambda i: (i, 0)),
        compiler_params=_params("arbitrary"),
        name="hyena_inproj",
    )(xs, xs, xs, gain.reshape(1, d), mod, w, bias.reshape(1, n_out), conv_w, conv_b.reshape(1, n_out))


def _rope_tables(n_seq, tm):
    rows = n_seq // GRID_W
    row = jnp.repeat(jnp.arange(rows), GRID_W).astype(F32)
    col = jnp.tile(jnp.arange(GRID_W), rows).astype(F32)
    half = HEAD_DIM // 2
    n_freq = half // 2
    inv = ROPE_THETA ** (-jnp.arange(n_freq, dtype=F32) / n_freq)
    ang = jnp.concatenate([row[:, None] * inv, col[:, None] * inv], axis=-1)
    cos = jnp.tile(jnp.cos(ang), (1, LANES // half))
    sin = jnp.tile(jnp.concatenate([-jnp.sin(ang), jnp.sin(ang)], axis=-1), (1, LANES // HEAD_DIM))
    cos = jnp.concatenate([cos, jnp.ones((tm, LANES), F32)], axis=0)
    sin = jnp.concatenate([sin, jnp.zeros((tm, LANES), F32)], axis=0)
    return cos, sin


def _doubled(w, n_heads):
    lead = w.shape[:-1]
    w = w.reshape(lead + (n_heads, 1, HEAD_DIM))
    return jnp.broadcast_to(w, lead + (n_heads, 2, HEAD_DIM)).reshape(lead + (n_heads * 2 * HEAD_DIM,))


def kernel(x, c, ctx, c_ctx, ada_w, ada_b, norm_mix, norm_ffn, attn_w_qkv, attn_b_qkv, attn_w_o, attn_b_o, attn_sinks, hy_w_in, hy_b_in, hy_conv_w, hy_conv_b, hy_f_w1, hy_f_b1, hy_f_w2, hy_f_b2, hy_f_w3, hy_f_b3, hy_f_freq, hy_f_wout, hy_bias_d, hy_w_o, hy_b_o, moe_router_w, moe_router_b, moe_w_gu, moe_b_gu, moe_w_down, moe_b_down, final_norm):
    B, N, D = x.shape
    C = ctx.shape[1]
    T, TC = B * N, B * C
    q_dim = N_HEADS * HEAD_DIM
    kv_dim = N_KV_HEADS * HEAD_DIM
    dims = dict(n_seq=N, n_batch=B)

    c_all = jnp.concatenate([c, c_ctx[None, :], jnp.zeros((16 - B - 1, D), F32)], axis=0)
    mod_all = _ada_table(c_all, ada_w, ada_b)[:, :B + 1].reshape(DEPTH, B + 1, 6, D)
    cos, sin = _rope_tables(N, TOKEN_TILE)

    xs = jnp.concatenate([x.reshape(T, D), ctx.reshape(TC, D)], axis=0)
    for i in range(DEPTH):
        kind, j = i % N_MIXERS, i // N_MIXERS
        update_ctx = any(l % N_MIXERS == 0 for l in range(i + 1, DEPTH))
        need_ctx = update_ctx or kind == 0
        mod = mod_all[i]
        n_in = T + TC if need_ctx else T
        n_out = T + TC if update_ctx else T
        if kind == 0:
            wq, wk, wv = (attn_w_qkv[j][:, :q_dim], attn_w_qkv[j][:, q_dim:q_dim + kv_dim],
                          attn_w_qkv[j][:, q_dim + kv_dim:])
            bq, bk, bv = (attn_b_qkv[j][:q_dim], attn_b_qkv[j][q_dim:q_dim + kv_dim],
                          attn_b_qkv[j][q_dim + kv_dim:])
            w = jnp.concatenate([wq, _doubled(wk, N_KV_HEADS), _doubled(wv, N_KV_HEADS)], axis=1).astype(BF16)
            b = jnp.concatenate([bq, _doubled(bk, N_KV_HEADS), _doubled(bv, N_KV_HEADS)])
            qkv = _norm_matmul(xs, norm_mix[i], mod[:, 0:2], w, b, n_rows=n_in, out_dtype=BF16,
                               rope=(cos, sin, q_dim + 2 * kv_dim, q_dim), **dims)
            o = _attention(qkv, attn_sinks[j], n_ctx=C, ctx_out=update_ctx, **dims)
            xs = _matmul_residual(o, attn_w_o[j].astype(BF16), attn_b_o[j], xs, mod[:, 2:3], n_rows=n_out, **dims)
        else:
            uc = _inproj(xs, norm_mix[i], mod[:, 0:2], hy_w_in[j].astype(BF16), hy_b_in[j], hy_conv_w[j],
                         hy_conv_b[j], n_rows=n_in, n_ctx=C, **dims)
            filt = (hy_f_w1[j], hy_f_b1[j], hy_f_w2[j], hy_f_b2[j], hy_f_w3[j], hy_f_b3[j], hy_f_freq[j],
                    hy_f_wout[j])
            z = _hyena_mixer(uc, hy_bias_d[j], filt, n_ctx=C, with_ctx=update_ctx, **dims)
            xs = _matmul_residual(z, hy_w_o[j].astype(BF16), hy_b_o[j], xs, mod[:, 2:3], n_rows=n_out, **dims)
        xs = _moe_layer(xs, i, mod, norm_ffn[i], moe_router_w[i], moe_router_b[i], moe_w_gu, moe_b_gu,
                        moe_w_down, moe_b_down, final_norm, n_rows=n_out, final=(i == DEPTH - 1), **dims)
    return xs.reshape(B, N, D)
```

```python
import functools
import math

import jax
import jax.numpy as jnp
import numpy as np
from jax import lax
from jax.experimental import pallas as pl
from jax.experimental.pallas import tpu as pltpu

DEPTH = 4
N_MIXERS = 2
GRID_W = 64
N_HEADS = 16
N_KV_HEADS = 4
HEAD_DIM = 64
GROUP = N_HEADS // N_KV_HEADS
WINDOW = 128
ROPE_THETA = 10000.0
HY_ORDER = 2
HY_SHORT = 3
HY_BANDS = 16
HY_TARGET = 1e-2
HY_FAST_DECAY = 0.3
HY_SLOW_DECAY = 1.5
N_EXPERTS = 32
TOP_K = 4
SWIGLU_LIMIT = 7.0
SWIGLU_ALPHA = 1.702
NORM_EPS = 1e-6

LANES = 128
OCT = 8
MINOR = 128
TOKEN_TILE = 512
Q_TILE = 128
KV_TOGETHER = 2
MOE_TILE = 512
VMEM_LIMIT = 56 * 1024 * 1024

F32 = jnp.float32
BF16 = jnp.bfloat16


def _dot(a, b):
    return jnp.dot(a, b, preferred_element_type=F32)


def _params(*sem):
    return pltpu.CompilerParams(dimension_semantics=sem, vmem_limit_bytes=VMEM_LIMIT)


def _ada_kernel(c_ref, w_ref, b_ref, o_ref):
    c = c_ref[...]
    s = c * (1.0 / (1.0 + jnp.exp(-c)))
    o_ref[...] = jnp.dot(s, w_ref[...], preferred_element_type=F32,
                         precision=lax.Precision.HIGHEST) + b_ref[...]


def _ada_table(c_all, ada_w, ada_b):
    r, d = c_all.shape
    depth, _, n6 = ada_w.shape
    tn = n6 // 4
    return pl.pallas_call(
        _ada_kernel,
        out_shape=jax.ShapeDtypeStruct((depth, r, n6), F32),
        grid=(depth, n6 // tn),
        in_specs=[pl.BlockSpec((r, d), lambda l, j: (0, 0)),
                  pl.BlockSpec((None, d, tn), lambda l, j: (l, 0, j)),
                  pl.BlockSpec((None, 1, tn), lambda l, j: (l, 0, j))],
        out_specs=pl.BlockSpec((None, r, tn), lambda l, j: (l, 0, j)),
        compiler_params=_params("arbitrary", "arbitrary"),
        name="ada_table",
    )(c_all, ada_w, ada_b.reshape(depth, 1, n6))


def _modulated(x, g, mod_ref, row):
    y = x * lax.rsqrt(jnp.mean(x * x, axis=-1, keepdims=True) + NORM_EPS) * g
    return y * (1.0 + mod_ref[row + 1:row + 2, :]) + mod_ref[row:row + 1, :]


def _mod_index(tm, n_seq, n_batch):
    return lambda i: (jnp.minimum(i * tm // n_seq, n_batch), 0, 0)


def _nm_kernel(x_ref, g_ref, mod_ref, w_ref, b_ref, cos_ref, sin_ref, o_ref, *, tn, rope_cols, q_cols):
    h = _modulated(x_ref[...], g_ref[...], mod_ref, 0).astype(BF16)
    n_out = o_ref.shape[1]
    if rope_cols:
        reps = tn // LANES
        cos = jnp.tile(cos_ref[...], (1, reps))
        sin = jnp.tile(sin_ref[...], (1, reps))
        lane = lax.broadcasted_iota(jnp.int32, (1, tn), 1)
        first_half = (lane & (HEAD_DIM - 1)) < (HEAD_DIM // 2)
    for n0 in range(0, n_out, tn):
        acc = _dot(h, w_ref[:, n0:n0 + tn]) + b_ref[:, n0:n0 + tn]
        if n0 < rope_cols:
            partner = jnp.where(first_half, pltpu.roll(acc, tn - HEAD_DIM // 2, 1),
                                pltpu.roll(acc, HEAD_DIM // 2, 1))
            acc = acc * cos + partner * sin
            if n0 < q_cols:
                acc = acc * (HEAD_DIM ** -0.5)
        o_ref[:, n0:n0 + tn] = acc.astype(o_ref.dtype)


def _norm_matmul(xs, gain, mod, w, bias, *, n_rows, n_seq, n_batch, out_dtype, rope=None):
    d = xs.shape[1]
    n_out = w.shape[1]
    tm, tn = TOKEN_TILE, 512
    if rope is None:
        cos = jnp.zeros((tm, LANES), F32)
        sin = cos
        rope_cols = q_cols = 0
        rope_index = lambda i: (0, 0)
    else:
        cos, sin, rope_cols, q_cols = rope
        seq_tiles = n_seq // tm
        lat_tiles = n_seq * n_batch // tm
        rope_index = lambda i: (jnp.where(i < lat_tiles, i % seq_tiles, seq_tiles), 0)
    kern = functools.partial(_nm_kernel, tn=tn, rope_cols=rope_cols, q_cols=q_cols)
    return pl.pallas_call(
        kern,
        out_shape=jax.ShapeDtypeStruct((n_rows, n_out), out_dtype),
        grid=(n_rows // tm,),
        in_specs=[pl.BlockSpec((tm, d), lambda i: (i, 0)),
                  pl.BlockSpec((1, d), lambda i: (0, 0)),
                  pl.BlockSpec((None, 2, d), _mod_index(tm, n_seq, n_batch)),
                  pl.BlockSpec((d, n_out), lambda i: (0, 0)),
                  pl.BlockSpec((1, n_out), lambda i: (0, 0)),
                  pl.BlockSpec((tm, LANES), rope_index),
                  pl.BlockSpec((tm, LANES), rope_index)],
        out_specs=pl.BlockSpec((tm, n_out), lambda i: (i, 0)),
        compiler_params=_params("arbitrary"),
        name="norm_matmul",
    )(xs, gain.reshape(1, d), mod, w, bias.reshape(1, n_out), cos, sin)


def _res_kernel(a_ref, w_ref, b_ref, x_ref, gate_ref, o_ref):
    y = _dot(a_ref[...].astype(BF16), w_ref[...]) + b_ref[...]
    o_ref[...] = x_ref[...] + gate_ref[...] * y


def _matmul_residual(a, w, bias, xs, gate, *, n_rows, n_seq, n_batch):
    k = a.shape[1]
    d = w.shape[1]
    tm = TOKEN_TILE
    return pl.pallas_call(
        _res_kernel,
        out_shape=jax.ShapeDtypeStruct((n_rows, d), F32),
        grid=(n_rows // tm,),
        in_specs=[pl.BlockSpec((tm, k), lambda i: (i, 0)),
                  pl.BlockSpec((k, d), lambda i: (0, 0)),
                  pl.BlockSpec((1, d), lambda i: (0, 0)),
                  pl.BlockSpec((tm, d), lambda i: (i, 0)),
                  pl.BlockSpec((None, 1, d), _mod_index(tm, n_seq, n_batch))],
        out_specs=pl.BlockSpec((tm, d), lambda i: (i, 0)),
        compiler_params=_params("arbitrary"),
        name="matmul_residual",
    )(a, w, bias.reshape(1, d), xs, gate)


def _attend_heads(q_ref, work, sink_ref, o_ref):
    tq = q_ref.shape[0]
    lane = lax.broadcasted_iota(jnp.int32, (tq, LANES), 1)
    low = lane < HEAD_DIM
    row = lax.broadcasted_iota(jnp.int32, (2 * tq, 1), 0)
    lane_tiles = lambda a: [a[:, t:t + LANES] for t in range(0, a.shape[1], LANES)]
    items = [(kvh * GROUP + 2 * j, segments) for kvh, segments in work for j in range(GROUP // 2)]
    sinks = [jnp.where(row < tq, sink_ref[head], sink_ref[head + 1]) for head, _ in items]
    scores, maxes = [], []
    for (head, segments), sink in zip(items, sinks):
        qp = q_ref[:, head * HEAD_DIM:head * HEAD_DIM + LANES]
        zero = jnp.zeros_like(qp)
        q2 = jnp.concatenate([jnp.where(low, qp, zero), jnp.where(low, zero, qp)], axis=0)
        sj, folded = [], None
        for k, _, mask in segments:
            s = lax.dot_general(q2, k, (((1,), (1,)), ((), ())), preferred_element_type=F32)
            if mask is not None:
                s = jnp.where(mask, s, -jnp.inf)
            sj.append(s)
            for t in lane_tiles(s):
                folded = t if folded is None else jnp.maximum(folded, t)
        scores.append(sj)
        maxes.append(jnp.maximum(sink, jnp.max(folded, axis=-1, keepdims=True)))
    for (head, segments), sink, sj, m in zip(items, sinks, scores, maxes):
        o2, folded = None, None
        for s, (_, v, _) in zip(sj, segments):
            p = jnp.exp(s - m)
            for t in lane_tiles(p):
                folded = t if folded is None else folded + t
            pv = _dot(p.astype(BF16), v)
            o2 = pv if o2 is None else o2 + pv
        o2 = o2 / (jnp.exp(sink - m) + jnp.sum(folded, axis=-1, keepdims=True))
        o_ref[:, head * HEAD_DIM:head * HEAD_DIM + LANES] = jnp.where(low, o2[:tq], o2[tq:]).astype(o_ref.dtype)


def _attn_kernel(sink_ref, q_ref, kp_ref, kc_ref, kn_ref, vp_ref, vc_ref, vn_ref, kx_ref, vx_ref, o_ref,
                 *, n_q_blocks):
    qi = pl.program_id(1)
    tq = q_ref.shape[0]
    head_cols = lambda kvh: slice(kvh * LANES, (kvh + 1) * LANES)

    @pl.when(qi < n_q_blocks)
    def _():
        r = lax.broadcasted_iota(jnp.int32, (2 * tq, tq), 0) & (tq - 1)
        c = lax.broadcasted_iota(jnp.int32, (2 * tq, tq), 1)
        prev_ok = (c >= r) & (qi >= 1)
        next_ok = (c <= r) & (qi <= n_q_blocks - 2)
        for kv0 in range(0, N_KV_HEADS, KV_TOGETHER):
            work = []
            for kvh in range(kv0, kv0 + KV_TOGETHER):
                cs = head_cols(kvh)
                work.append((kvh, [(kc_ref[:, cs], vc_ref[:, cs], None), (kx_ref[:, cs], vx_ref[:, cs], None),
                                   (kp_ref[:, cs], vp_ref[:, cs], prev_ok), (kn_ref[:, cs], vn_ref[:, cs], next_ok)]))
            _attend_heads(q_ref, work, sink_ref, o_ref)

    @pl.when(qi >= n_q_blocks)
    def _():
        for kv0 in range(0, N_KV_HEADS, KV_TOGETHER):
            work = [(kvh, [(kx_ref[:, head_cols(kvh)], vx_ref[:, head_cols(kvh)], None)])
                    for kvh in range(kv0, kv0 + KV_TOGETHER)]
            _attend_heads(q_ref, work, sink_ref, o_ref)


def _attention(qkv, sinks, *, n_seq, n_batch, n_ctx, ctx_out):
    tq = Q_TILE
    assert tq == WINDOW
    nq = n_seq // tq
    ncq = n_ctx // tq if ctx_out else 0
    lat_blocks = n_batch * nq
    q_dim = N_HEADS * HEAD_DIM
    kv_w = N_KV_HEADS * LANES
    k_col = q_dim // kv_w
    v_col = k_col + 1
    ctx_row0 = n_batch * n_seq // n_ctx

    def q_index(b, i, s):
        return (jnp.where(i < nq, b * nq + i, lat_blocks + b * (n_ctx // tq) + (i - nq)), 0)

    def kv_index(off, col):
        def index(b, i, s):
            return (b * nq + jnp.clip(i + off, 0, nq - 1), col)
        return index

    n_rows = n_batch * n_seq + (n_batch * n_ctx if ctx_out else 0)
    grid_spec = pltpu.PrefetchScalarGridSpec(
        num_scalar_prefetch=1,
        grid=(n_batch, nq + ncq),
        in_specs=[pl.BlockSpec((tq, q_dim), q_index),
                  pl.BlockSpec((tq, kv_w), kv_index(-1, k_col)),
                  pl.BlockSpec((tq, kv_w), kv_index(0, k_col)),
                  pl.BlockSpec((tq, kv_w), kv_index(1, k_col)),
                  pl.BlockSpec((tq, kv_w), kv_index(-1, v_col)),
                  pl.BlockSpec((tq, kv_w), kv_index(0, v_col)),
                  pl.BlockSpec((tq, kv_w), kv_index(1, v_col)),
                  pl.BlockSpec((n_ctx, kv_w), lambda b, i, s: (ctx_row0 + b, k_col)),
                  pl.BlockSpec((n_ctx, kv_w), lambda b, i, s: (ctx_row0 + b, v_col))],
        out_specs=pl.BlockSpec((tq, q_dim), q_index),
    )
    return pl.pallas_call(
        functools.partial(_attn_kernel, n_q_blocks=nq),
        out_shape=jax.ShapeDtypeStruct((n_rows, q_dim), BF16),
        grid_spec=grid_spec,
        compiler_params=_params("arbitrary", "arbitrary"),
        name="attention",
    )(sinks, qkv, qkv, qkv, qkv, qkv, qkv, qkv, qkv, qkv)


def _pack_pairs(x):
    half = x.shape[1] // 2
    bits = lax.bitcast_convert_type(x.astype(BF16).astype(F32), jnp.int32)
    return lax.shift_right_logical(bits[:, :half], 16) | (bits[:, half:] & -65536)


def _unpack_pairs(u):
    return (lax.bitcast_convert_type(lax.shift_left(u, 16), F32),
            lax.bitcast_convert_type(u & -65536, F32))


def _router_kernel(x_ref, g_ref, mod_ref, w_ref, b_ref, tri_ref, tok_ref, route_ref, count_ref, seen):
    @pl.when(pl.program_id(0) == 0)
    def _():
        seen[...] = jnp.zeros_like(seen)

    h = _modulated(x_ref[...], g_ref[...], mod_ref, 0)
    tok_ref[...] = _pack_pairs(h)
    h_hi = h.astype(BF16)
    h_lo = (h - h_hi.astype(F32)).astype(BF16)
    logits = (_dot(h_hi, w_ref[0]) + (_dot(h_hi, w_ref[1]) + _dot(h_lo, w_ref[0]))) + b_ref[...]
    lane = lax.broadcasted_iota(jnp.int32, logits.shape, 1)
    rest = jnp.where(lane < N_EXPERTS, logits, -jnp.inf)
    chosen = jnp.zeros(logits.shape, F32)
    top_v, top_i = [], []
    for _ in range(TOP_K):
        best = jnp.max(rest, axis=-1, keepdims=True)
        idx = jnp.min(jnp.where(rest == best, lane, LANES), axis=-1, keepdims=True)
        hit = lane == idx
        top_v.append(best)
        top_i.append(idx)
        rest = jnp.where(hit, -jnp.inf, rest)
        chosen = jnp.where(hit, 1.0, chosen)
    weights = [jnp.exp(v - top_v[0]) for v in top_v]
    denom = (weights[0] + weights[1]) + (weights[2] + weights[3])
    earlier = _dot(tri_ref[...], chosen.astype(BF16)) + seen[...]
    seen[...] = seen[...] + jnp.sum(chosen, axis=0, keepdims=True)
    route = jnp.zeros(logits.shape, F32)
    for k in range(TOP_K):
        rank = jnp.sum(jnp.where(lane == top_i[k], earlier, 0.0), axis=-1, keepdims=True)
        route = jnp.where(lane == k, top_i[k].astype(F32), route)
        route = jnp.where(lane == TOP_K + k, weights[k] / denom, route)
        route = jnp.where(lane == 2 * TOP_K + k, rank, route)
    route_ref[...] = route
    count_ref[...] = jnp.broadcast_to(seen[...], count_ref.shape)


def _router(xs, gain, mod, w, bias, *, n_rows, n_seq, n_batch):
    d = xs.shape[1]
    tm = TOKEN_TILE
    tri = jnp.asarray(np.tril(np.ones((tm, tm)), -1), BF16)
    return pl.pallas_call(
        _router_kernel,
        out_shape=(jax.ShapeDtypeStruct((n_rows, d // 2), jnp.int32), jax.ShapeDtypeStruct((n_rows, LANES), F32),
                   jax.ShapeDtypeStruct((OCT, LANES), F32)),
        grid=(n_rows // tm,),
        in_specs=[pl.BlockSpec((tm, d), lambda i: (i, 0)),
                  pl.BlockSpec((1, d), lambda i: (0, 0)),
                  pl.BlockSpec((None, 2, d), _mod_index(tm, n_seq, n_batch)),
                  pl.BlockSpec((2, d, LANES), lambda i: (0, 0, 0)),
                  pl.BlockSpec((1, LANES), lambda i: (0, 0)),
                  pl.BlockSpec((tm, tm), lambda i: (0, 0))],
        out_specs=(pl.BlockSpec((tm, d // 2), lambda i: (i, 0)), pl.BlockSpec((tm, LANES), lambda i: (i, 0)),
                   pl.BlockSpec((OCT, LANES), lambda i: (0, 0))),
        scratch_shapes=[pltpu.VMEM((1, LANES), F32)],
        compiler_params=_params("arbitrary"),
        name="router",
    )(xs, gain.reshape(1, d), mod, w, bias, tri)


def _expert_kernel(be_ref, nb_ref, x_ref, wgu_ref, bgu_ref, wd_ref, bd_ref, o_ref, wgu_bf, wd_bf):
    i = pl.program_id(0)
    fresh = jnp.logical_or(i == 0, be_ref[i] != be_ref[jnp.maximum(i - 1, 0)])

    @pl.when(jnp.logical_and(fresh, i < nb_ref[0]))
    def _():
        wgu_bf[...] = wgu_ref[...].astype(BF16)
        wd_bf[...] = wd_ref[...].astype(BF16)

    @pl.when(i < nb_ref[0])
    def _():
        d_e = wd_ref.shape[0]
        x = jnp.concatenate(_unpack_pairs(x_ref[...]), axis=1).astype(BF16)
        g = jnp.minimum(_dot(x, wgu_bf[:, :d_e]) + bgu_ref[:, :d_e], SWIGLU_LIMIT)
        up = jnp.clip(_dot(x, wgu_bf[:, d_e:]) + bgu_ref[:, d_e:], -SWIGLU_LIMIT, SWIGLU_LIMIT)
        act = g * (1.0 / (1.0 + jnp.exp(-SWIGLU_ALPHA * g))) * (up + 1.0)
        o_ref[...] = _pack_pairs(_dot(act.astype(BF16), wd_bf[...]) + bd_ref[...])

    @pl.when(i >= nb_ref[0])
    def _():
        o_ref[...] = jnp.zeros_like(o_ref)


def _experts(xs, block_e, n_used, w_gu, b_gu, w_down, b_down, layer):
    r = xs.shape[0]
    n_e, d, d_gu = w_gu.shape[1:]
    tm = MOE_TILE
    grid_spec = pltpu.PrefetchScalarGridSpec(
        num_scalar_prefetch=2,
        grid=(r // tm,),
        in_specs=[pl.BlockSpec((tm, d // 2), lambda i, be, nb: (i, 0)),
                  pl.BlockSpec((None, None, d, d_gu), lambda i, be, nb: (layer, be[i], 0, 0)),
                  pl.BlockSpec((None, None, 1, d_gu), lambda i, be, nb: (layer, be[i], 0, 0)),
                  pl.BlockSpec((None, None, d_gu // 2, d), lambda i, be, nb: (layer, be[i], 0, 0)),
                  pl.BlockSpec((None, None, 1, d), lambda i, be, nb: (layer, be[i], 0, 0))],
        out_specs=pl.BlockSpec((tm, d // 2), lambda i, be, nb: (i, 0)),
        scratch_shapes=[pltpu.VMEM((d, d_gu), BF16), pltpu.VMEM((d_gu // 2, d), BF16)],
    )
    return pl.pallas_call(
        _expert_kernel,
        out_shape=jax.ShapeDtypeStruct((r, d // 2), jnp.int32),
        grid_spec=grid_spec,
        compiler_params=_params("arbitrary"),
        name="experts",
    )(block_e, n_used, xs, w_gu, b_gu.reshape(b_gu.shape[0], n_e, 1, d_gu), w_down,
      b_down.reshape(b_down.shape[0], n_e, 1, d))


def _combine_kernel(x_ref, y0_ref, y1_ref, y2_ref, y3_ref, route_ref, gate_ref, g_ref, o_ref, *, final):
    w = [route_ref[:, TOP_K + k:TOP_K + k + 1] for k in range(TOP_K)]
    y = [_unpack_pairs(r[...]) for r in (y0_ref, y1_ref, y2_ref, y3_ref)]
    f = jnp.concatenate([(w[0] * y[0][h] + w[1] * y[1][h]) + (w[2] * y[2][h] + w[3] * y[3][h]) for h in range(2)],
                        axis=1)
    x = x_ref[...] + gate_ref[...] * f
    if final:
        x = x * lax.rsqrt(jnp.mean(x * x, axis=-1, keepdims=True) + NORM_EPS) * g_ref[...]
    o_ref[...] = x


def _combine(xs, ys, route, gate, final_gain, *, n_rows, n_seq, n_batch, final):
    d = xs.shape[1]
    tm = TOKEN_TILE
    row_spec = pl.BlockSpec((tm, d), lambda i: (i, 0))
    return pl.pallas_call(
        functools.partial(_combine_kernel, final=final),
        out_shape=jax.ShapeDtypeStruct((n_rows, d), F32),
        grid=(n_rows // tm,),
        in_specs=[row_spec] + [pl.BlockSpec((tm, d // 2), lambda i: (i, 0))] * TOP_K
                 + [pl.BlockSpec((tm, LANES), lambda i: (i, 0)),
                                             pl.BlockSpec((None, 1, d), _mod_index(tm, n_seq, n_batch)),
                                             pl.BlockSpec((1, d), lambda i: (0, 0))],
        out_specs=row_spec,
        compiler_params=_params("arbitrary"),
        name="combine",
    )(xs, *ys, route, gate, final_gain.reshape(1, d))


def _moe_layer(xs, layer, mod, norm_g, router_w, router_b, w_gu, b_gu, w_down, b_down, final_gain,
               *, n_rows, n_seq, n_batch, final):
    d = xs.shape[1]
    rw = jnp.zeros((d, LANES), F32).at[:, :N_EXPERTS].set(router_w)
    rw_hi = rw.astype(BF16)
    rw = jnp.stack([rw_hi, (rw - rw_hi.astype(F32)).astype(BF16)])
    rb = jnp.zeros((1, LANES), F32).at[0, :N_EXPERTS].set(router_b)
    tok, route, count = _router(xs, norm_g, mod[:, 3:5], rw, rb, n_rows=n_rows, n_seq=n_seq, n_batch=n_batch)
    top_i = route[:, :TOP_K].astype(jnp.int32)
    rank = route[:, 2 * TOP_K:3 * TOP_K].astype(jnp.int32)
    counts = count[0, :N_EXPERTS].astype(jnp.int32)
    n_assign = n_rows * TOP_K
    padded = (counts + MOE_TILE - 1) // MOE_TILE * MOE_TILE
    pad_end = jnp.cumsum(padded)
    pad_start = pad_end - padded
    start = jnp.cumsum(counts) - counts
    n_blocks = -(-n_assign // MOE_TILE) + N_EXPERTS
    n_slots = n_blocks * MOE_TILE
    block_e = jnp.minimum(jnp.sum(pad_end[None, :] <= (jnp.arange(n_blocks) * MOE_TILE)[:, None], axis=1),
                          N_EXPERTS - 1).astype(jnp.int32)
    n_used = (pad_end[-1] // MOE_TILE).astype(jnp.int32).reshape(1)
    slot_of = pad_start[top_i] + rank
    assert n_assign < 1 << 20
    _, by_rank = lax.sort_key_val((top_i * (1 << 20) + rank).reshape(-1), jnp.arange(n_assign, dtype=jnp.int32),
                                  is_stable=False)
    block_shift = (start - pad_start)[block_e]
    src = by_rank[jnp.clip(jnp.arange(n_slots, dtype=jnp.int32) + jnp.repeat(block_shift, MOE_TILE),
                           0, n_assign - 1)]
    ys = _experts(tok[src // TOP_K], block_e, n_used, w_gu, b_gu, w_down, b_down, layer)
    parts = [ys[slot_of[:, k]] for k in range(TOP_K)]
    return _combine(xs, parts, route, mod[:, 5:6], final_gain, n_rows=n_rows, n_seq=n_seq, n_batch=n_batch,
                    final=final)


def _hyena_spectrum(L, consts, w1, b1, w2, b2, w3, b3, freq, w_out, d):
    t = jnp.linspace(0.0, 1.0, L, dtype=F32)[:, None]
    w = 2.0 * math.pi * jnp.arange(L, dtype=F32)[:, None] / L
    bands = jnp.linspace(1e-4, HY_BANDS - 1, HY_BANDS, dtype=F32)[None, :]
    z = jnp.concatenate([t, jnp.cos(bands * w), -jnp.sin(bands * w)], axis=-1)
    a = jnp.sin(freq * (z @ w1 + b1))
    a = jnp.sin(freq * (a @ w2 + b2))
    a = jnp.sin(freq * (a @ w3 + b3))
    deltas = jnp.linspace(math.log(HY_TARGET) / HY_SLOW_DECAY, math.log(HY_TARGET) / HY_FAST_DECAY, d, dtype=F32)
    return _filter_spectrum(a, w_out, deltas, consts)


def _filter_kernel(feat_ref, wf_ref, wb_ref, delta_ref, kgf_ref, fm_ref, twc_ref, tws_ref, spec_ref, a_scr):
    n_k = kgf_ref.shape[0] // (2 * OCT)
    n_a = n_k - 1
    n_bo = MINOR // OCT
    dt = wf_ref.shape[1]
    kc = _major_chunk(n_k)
    cr_rows = kc * OCT
    w2 = jnp.concatenate([wf_ref[...], wb_ref[...]], axis=1).astype(BF16)
    decay = jnp.abs(jnp.concatenate([delta_ref[...], delta_ref[...]], axis=1)) * (1.0 / (n_a * MINOR - 1))
    r = lax.broadcasted_iota(jnp.int32, (n_a * OCT, 1), 0)
    lag0 = (lax.shift_right_logical(r, 3) * MINOR + (r & (OCT - 1))).astype(F32)
    total = jnp.zeros((1, 2 * dt), F32)
    b0 = None
    for bo in range(n_bo):
        feat = feat_ref[pl.ds(bo, n_a, stride=n_bo)].reshape(n_a * OCT, feat_ref.shape[2]).astype(BF16)
        x2 = _dot(feat, w2) * jnp.exp(-(lag0 + float(OCT * bo)) * decay)
        if bo == 0:
            b0 = x2[0:1, dt:]
        total = total + jnp.sum(jnp.abs(x2), axis=0, keepdims=True)
        p = _dot(kgf_ref[...], x2.astype(BF16))
        for ch in range(n_k // kc):
            r0 = 2 * ch * cr_rows
            ks = slice(ch * kc, (ch + 1) * kc)
            tw_rows = slice(ch * cr_rows, (ch + 1) * cr_rows)
            ar, ai = p[r0:r0 + cr_rows], p[r0 + cr_rows:r0 + 2 * cr_rows]
            c, s = twc_ref[tw_rows, bo:bo + 1], tws_ref[tw_rows, bo:bo + 1]
            a_scr[ks, 0, bo] = (ar * c + ai * s).reshape(kc, OCT, 2 * dt)
            a_scr[ks, 1, bo] = (ai * c - ar * s).reshape(kc, OCT, 2 * dt)
    inv = 1.0 / (total[:, :dt] + total[:, dt:] - jnp.abs(b0))
    unroll = max(u for u in range(1, 12) if n_k % u == 0)

    def minor_stage(step, carry):
        for j in range(unroll):
            k = step * unroll + j
            x = _dot(fm_ref[...], a_scr[k].reshape(2 * MINOR, 2 * dt).astype(BF16))
            spec_ref[k, 0] = ((x[:MINOR, :dt] + x[:MINOR, dt:] - b0) * inv).astype(spec_ref.dtype)
            spec_ref[k, 1] = ((x[MINOR:, :dt] - x[MINOR:, dt:]) * inv).astype(spec_ref.dtype)
        return carry

    lax.fori_loop(0, n_k // unroll, minor_stage, 0)
    for k in range(n_k, spec_ref.shape[0]):
        spec_ref[k] = jnp.zeros(spec_ref.shape[1:], spec_ref.dtype)


def _filter_spectrum(feat, w_out, deltas, consts):
    kgf, _, fm, _, twc, tws = consts
    dt = LANES
    d = deltas.shape[0]
    n_feat = feat.shape[1]
    seq_blk = feat.shape[0] // OCT
    n_k = kgf.shape[0] // (2 * OCT)
    n_kp = n_k + n_k % 2
    tiles = d // dt
    const_spec = lambda arr: pl.BlockSpec(arr.shape, lambda o, j: (0, 0))
    return pl.pallas_call(
        _filter_kernel,
        out_shape=jax.ShapeDtypeStruct((HY_ORDER, n_kp, 2, MINOR, d), BF16),
        grid=(HY_ORDER, tiles),
        in_specs=[pl.BlockSpec((seq_blk, OCT, n_feat), lambda o, j: (0, 0, 0)),
                  pl.BlockSpec((n_feat, dt), lambda o, j: (0, o * tiles + j)),
                  pl.BlockSpec((n_feat, dt), lambda o, j: (0, (HY_ORDER + o) * tiles + j)),
                  pl.BlockSpec((1, dt), lambda o, j: (0, j)),
                  const_spec(kgf), const_spec(fm), const_spec(twc), const_spec(tws)],
        out_specs=pl.BlockSpec((None, n_kp, 2, MINOR, dt), lambda o, j: (o, 0, 0, 0, j)),
        scratch_shapes=[pltpu.VMEM((n_k, 2, MINOR // OCT, OCT, 2 * dt), F32)],
        compiler_params=_params("arbitrary", "arbitrary"),
        name="hyena_filter",
    )(feat.reshape(seq_blk, OCT, n_feat), w_out, w_out, deltas.reshape(1, d), kgf, fm, twc, tws)


def _major_chunk(n_k):
    return max(c for c in range(1, 14) if n_k % c == 0)


def _dft_constants(L):
    n = 2 * L
    n1 = n // MINOR
    n_a, n_k = n1 // 2, n1 // 2 + 1
    k = np.arange(n_k)[:, None]
    a = np.arange(n_a)[None, :]
    th = 2.0 * np.pi * k * a / n1
    eye = np.eye(OCT)
    kgf = np.concatenate([np.kron(np.cos(th), eye), np.kron(-np.sin(th), eye)], axis=0)
    w = np.where((k == 0) | (k == n_a), 1.0, 2.0) / n
    kgi = np.concatenate([np.kron((np.cos(th) * w).T, eye), np.kron((-np.sin(th) * w).T, eye)], axis=1)
    b = np.arange(MINOR)
    ph = 2.0 * np.pi * np.outer(b, b) / MINOR
    fr, fi = np.cos(ph), -np.sin(ph)
    fm = np.block([[fr, -fi], [fi, fr]])
    fmi = np.block([[fr, fi], [-fi, fr]])
    bb = OCT * np.arange(MINOR // OCT)[None, None, :] + np.arange(OCT)[None, :, None]
    tw = 2.0 * np.pi * np.arange(n_k)[:, None, None] * bb / n
    pad = ((0, 0), (0, LANES - MINOR // OCT))
    twc = np.pad(np.cos(tw).reshape(n_k * OCT, -1), pad)
    tws = np.pad(np.sin(tw).reshape(n_k * OCT, -1), pad)
    kc = _major_chunk(n_k)
    kgf = kgf.reshape(2, n_k // kc, kc * OCT, -1).transpose(1, 0, 2, 3).reshape(2 * n_k * OCT, -1)
    kgi = kgi.reshape(-1, 2, n_k // kc, kc * OCT).transpose(0, 2, 1, 3).reshape(-1, 2 * n_k * OCT)
    first_im = kc * OCT
    last_im = 2 * n_k * OCT - OCT
    kgi = np.delete(kgi, np.r_[first_im:first_im + OCT, last_im:last_im + OCT], axis=1)
    return (jnp.asarray(kgf, BF16), jnp.asarray(kgi, BF16), jnp.asarray(fm, BF16), jnp.asarray(fmi, BF16),
            jnp.asarray(twc, F32), jnp.asarray(tws, F32))


def _hyena_kernel(z_ref, gate_ref, spec_ref, d_ref, kgf_ref, kgi_ref, fm_ref, fmi_ref, twc_ref, tws_ref, *rest):
    o_ref, a_scr = rest[-2], rest[-1]
    n_k = kgf_ref.shape[0] // (2 * OCT)
    n_a = n_k - 1
    n_bo = MINOR // OCT
    dt = z_ref.shape[2]
    kc = _major_chunk(n_k)
    cr_rows = kc * OCT
    side_by_side = lambda u, v: jnp.concatenate([u, v], axis=1)

    if a_scr.shape[0] > n_k:
        a_scr[n_k] = jnp.zeros(a_scr.shape[1:], F32)

    for bo in range(0, n_bo, 2):
        x2 = side_by_side(*[z_ref[pl.ds(bo + h, n_a, stride=n_bo)].reshape(n_a * OCT, dt) for h in range(2)])
        p = _dot(kgf_ref[...], x2.astype(BF16))
        for ch in range(n_k // kc):
            r0 = 2 * ch * cr_rows
            ks = slice(ch * kc, (ch + 1) * kc)
            tw_rows = slice(ch * cr_rows, (ch + 1) * cr_rows)
            for h in range(2):
                cols = slice(h * dt, (h + 1) * dt)
                ar, ai = p[r0:r0 + cr_rows, cols], p[r0 + cr_rows:r0 + 2 * cr_rows, cols]
                c, s = twc_ref[tw_rows, bo + h:bo + h + 1], tws_ref[tw_rows, bo + h:bo + h + 1]
                a_scr[ks, 0, bo + h] = (ar * c + ai * s).reshape(kc, OCT, dt)
                a_scr[ks, 1, bo + h] = (ai * c - ar * s).reshape(kc, OCT, dt)

    n_pairs = a_scr.shape[0] // 2
    unroll = max(u for u in range(1, 12) if n_pairs % u == 0)

    def minor_stage(step, carry):
        for j in range(unroll):
            k = 2 * (step * unroll + j)
            b2 = side_by_side(a_scr[k].reshape(2 * MINOR, dt), a_scr[k + 1].reshape(2 * MINOR, dt))
            x = _dot(fm_ref[...], b2.astype(BF16))
            xr, xi = x[:MINOR], x[MINOR:]
            gr = side_by_side(spec_ref[k, 0], spec_ref[k + 1, 0]).astype(F32)
            gi = side_by_side(spec_ref[k, 1], spec_ref[k + 1, 1]).astype(F32)
            y = jnp.concatenate([xr * gr - xi * gi, xr * gi + xi * gr], axis=0).astype(BF16)
            c2 = _dot(fmi_ref[...], y)
            a_scr[k] = c2[:, :dt].reshape(2, n_bo, OCT, dt)
            a_scr[k + 1] = c2[:, dt:].reshape(2, n_bo, OCT, dt)
        return carry

    lax.fori_loop(0, n_pairs // unroll, minor_stage, 0)

    for bo in range(0, n_bo, 2):
        halves = []
        for h in range(2):
            pieces = []
            for ch in range(n_k // kc):
                ks = slice(ch * kc, (ch + 1) * kc)
                tw_rows = slice(ch * cr_rows, (ch + 1) * cr_rows)
                cr = a_scr[ks, 0, bo + h].reshape(cr_rows, dt)
                ci = a_scr[ks, 1, bo + h].reshape(cr_rows, dt)
                c, s = twc_ref[tw_rows, bo + h:bo + h + 1], tws_ref[tw_rows, bo + h:bo + h + 1]
                im = cr * s + ci * c
                im = im[OCT if ch == 0 else 0:cr_rows - OCT if ch == n_k // kc - 1 else cr_rows]
                pieces += [cr * c - ci * s, im]
            halves.append(jnp.concatenate(pieces, axis=0))
        conv = _dot(kgi_ref[...], side_by_side(*halves).astype(BF16))
        for h in range(2):
            rows = pl.ds(bo + h, n_a, stride=n_bo)
            zin = z_ref[rows].reshape(n_a * OCT, dt)
            out = gate_ref[rows].reshape(n_a * OCT, dt) * (conv[:, h * dt:(h + 1) * dt] + zin * d_ref[...])
            o_ref[rows] = out.reshape(n_a, OCT, dt)


def _hyena_conv(zin, zin_col0, gate, gate_col0, spec, bias_d, consts, *, seq_len, n_seqs, row0, out_rows, prev_out,
                order=0):
    d = spec.shape[-1]
    dt = LANES
    n_k = spec.shape[1]
    seq_blk = seq_len // OCT
    blk0 = row0 // seq_len
    as_oct = lambda arr: arr.reshape(arr.shape[0] // OCT, OCT, arr.shape[1])
    const_spec = lambda arr: pl.BlockSpec(arr.shape, lambda j, b: (0, 0))
    in_specs = [pl.BlockSpec((seq_blk, OCT, dt), lambda j, b: (blk0 + b, 0, zin_col0 // dt + j)),
                pl.BlockSpec((seq_blk, OCT, dt), lambda j, b: (blk0 + b, 0, gate_col0 // dt + j)),
                pl.BlockSpec((None, n_k, 2, MINOR, dt), lambda j, b: (order, 0, 0, 0, j)),
                pl.BlockSpec((None, 1, dt), lambda j, b: (order, 0, j))] + [const_spec(cst) for cst in consts]
    args = [as_oct(zin), as_oct(gate), spec, bias_d.reshape(bias_d.shape[0], 1, d), *consts]
    aliases = {}
    if prev_out is not None:
        in_specs.append(pl.BlockSpec(memory_space=pl.ANY))
        aliases = {len(args): 0}
        args.append(as_oct(prev_out))
    out = pl.pallas_call(
        _hyena_kernel,
        out_shape=jax.ShapeDtypeStruct((out_rows // OCT, OCT, d), F32),
        grid=(d // dt, n_seqs),
        in_specs=in_specs,
        out_specs=pl.BlockSpec((seq_blk, OCT, dt), lambda j, b: (blk0 + b, 0, j)),
        scratch_shapes=[pltpu.VMEM((n_k, 2, MINOR // OCT, OCT, dt), F32)],
        input_output_aliases=aliases,
        compiler_params=_params("arbitrary", "arbitrary"),
        name="hyena_conv",
    )(*args)
    return out.reshape(out_rows, d)


def _hyena_mixer(uc, bias_d, filt, *, n_seq, n_batch, n_ctx, with_ctx):
    d = uc.shape[1] // 3
    rows = uc.shape[0]
    groups = [(n_seq, n_batch, 0)] + ([(n_ctx, n_batch, n_seq * n_batch)] if with_ctx else [])
    consts = [_dft_constants(L) for L, _, _ in groups]
    specs = [_hyena_spectrum(L, cst, *filt, d) for (L, _, _), cst in zip(groups, consts)]
    z = None
    for o in range(HY_ORDER):
        zin, zin_col0 = (uc, 2 * d) if o == 0 else (z, 0)
        out = None
        for (L, n_seqs, row0), spec, cst in zip(groups, specs, consts):
            out = _hyena_conv(zin, zin_col0, uc, o * d, spec, bias_d, cst, seq_len=L, n_seqs=n_seqs, row0=row0,
                              out_rows=rows, prev_out=out, order=o)
        z = out
    return z


def _inproj_kernel(x_ref, xp_ref, xn_ref, g_ref, mod_ref, w_ref, b_ref, cw_ref, cb_ref, o_ref,
                   *, tn, n_seq, n_ctx, lat_tiles):
    i = pl.program_id(0)
    tm = x_ref.shape[0]
    x = jnp.concatenate([xp_ref[...], x_ref[...], xn_ref[...]], axis=0)
    h = _modulated(x, g_ref[...], mod_ref, 0).astype(BF16)
    row = lax.broadcasted_iota(jnp.int32, (tm, 1), 0)
    pos = jnp.where(i < lat_tiles, (i * tm) % n_seq + row, row & (n_ctx - 1))
    last = jnp.where(i < lat_tiles, n_seq - 1, n_ctx - 1)
    has_prev = pos != 0
    has_next = pos != last
    for n0 in range(0, o_ref.shape[1], tn):
        cols = slice(n0, n0 + tn)
        u = _dot(h, w_ref[:, cols]) + b_ref[:, cols]
        prev = jnp.where(has_prev, u[OCT - 1:OCT - 1 + tm], 0.0)
        nxt = jnp.where(has_next, u[OCT + 1:OCT + 1 + tm], 0.0)
        o_ref[:, cols] = (cb_ref[:, cols] + prev * cw_ref[0:1, cols] + u[OCT:OCT + tm] * cw_ref[1:2, cols]
                          + nxt * cw_ref[2:3, cols])


def _inproj(xs, gain, mod, w, bias, conv_w, conv_b, *, n_rows, n_seq, n_batch, n_ctx):
    d = xs.shape[1]
    n_out = w.shape[1]
    tm, tn = TOKEN_TILE, 512
    assert n_seq % tm == 0 and tm % n_ctx == 0 and n_ctx & (n_ctx - 1) == 0
    halo = tm // OCT
    last_halo = xs.shape[0] // OCT - 1
    kern = functools.partial(_inproj_kernel, tn=tn, n_seq=n_seq, n_ctx=n_ctx, lat_tiles=n_seq * n_batch // tm)
    return pl.pallas_call(
        kern,
        out_shape=jax.ShapeDtypeStruct((n_rows, n_out), F32),
        grid=(n_rows // tm,),
        in_specs=[pl.BlockSpec((tm, d), lambda i: (i, 0)),
                  pl.BlockSpec((OCT, d), lambda i: (jnp.maximum(i * halo - 1, 0), 0)),
                  pl.BlockSpec((OCT, d), lambda i: (jnp.minimum((i + 1) * halo, last_halo), 0)),
                  pl.BlockSpec((1, d), lambda i: (0, 0)),
                  pl.BlockSpec((None, 2, d), _mod_index(tm, n_seq, n_batch)),
                  pl.BlockSpec((d, n_out), lambda i: (0, 0)),
                  pl.BlockSpec((1, n_out), lambda i: (0, 0)),
                  pl.BlockSpec((HY_SHORT, n_out), lambda i: (0, 0)),
                  pl.BlockSpec((1, n_out), lambda i: (0, 0))],
        out_specs=pl.BlockSpec((tm, n_out), lambda i: (i, 0)),
        compiler_params=_params("arbitrary"),
        name="hyena_inproj",
    )(xs, xs, xs, gain.reshape(1, d), mod, w, bias.reshape(1, n_out), conv_w, conv_b.reshape(1, n_out))


def _rope_tables(n_seq, tm):
    rows = n_seq // GRID_W
    row = jnp.repeat(jnp.arange(rows), GRID_W).astype(F32)
    col = jnp.tile(jnp.arange(GRID_W), rows).astype(F32)
    half = HEAD_DIM // 2
    n_freq = half // 2
    inv = ROPE_THETA ** (-jnp.arange(n_freq, dtype=F32) / n_freq)
    ang = jnp.concatenate([row[:, None] * inv, col[:, None] * inv], axis=-1)
    cos = jnp.tile(jnp.cos(ang), (1, LANES // half))
    sin = jnp.tile(jnp.concatenate([-jnp.sin(ang), jnp.sin(ang)], axis=-1), (1, LANES // HEAD_DIM))
    cos = jnp.concatenate([cos, jnp.ones((tm, LANES), F32)], axis=0)
    sin = jnp.concatenate([sin, jnp.zeros((tm, LANES), F32)], axis=0)
    return cos, sin


def _doubled(w, n_heads):
    lead = w.shape[:-1]
    w = w.reshape(lead + (n_heads, 1, HEAD_DIM))
    return jnp.broadcast_to(w, lead + (n_heads, 2, HEAD_DIM)).reshape(lead + (n_heads * 2 * HEAD_DIM,))


def kernel(x, c, ctx, c_ctx, ada_w, ada_b, norm_mix, norm_ffn, attn_w_qkv, attn_b_qkv, attn_w_o, attn_b_o, attn_sinks, hy_w_in, hy_b_in, hy_conv_w, hy_conv_b, hy_f_w1, hy_f_b1, hy_f_w2, hy_f_b2, hy_f_w3, hy_f_b3, hy_f_freq, hy_f_wout, hy_bias_d, hy_w_o, hy_b_o, moe_router_w, moe_router_b, moe_w_gu, moe_b_gu, moe_w_down, moe_b_down, final_norm):
    B, N, D = x.shape
    C = ctx.shape[1]
    T, TC = B * N, B * C
    q_dim = N_HEADS * HEAD_DIM
    kv_dim = N_KV_HEADS * HEAD_DIM
    dims = dict(n_seq=N, n_batch=B)

    c_all = jnp.concatenate([c, c_ctx[None, :], jnp.zeros((16 - B - 1, D), F32)], axis=0)
    mod_all = _ada_table(c_all, ada_w, ada_b)[:, :B + 1].reshape(DEPTH, B + 1, 6, D)
    cos, sin = _rope_tables(N, TOKEN_TILE)

    xs = jnp.concatenate([x.reshape(T, D), ctx.reshape(TC, D)], axis=0)
    for i in range(DEPTH):
        kind, j = i % N_MIXERS, i // N_MIXERS
        update_ctx = any(l % N_MIXERS == 0 for l in range(i + 1, DEPTH))
        need_ctx = update_ctx or kind == 0
        mod = mod_all[i]
        n_in = T + TC if need_ctx else T
        n_out = T + TC if update_ctx else T
        if kind == 0:
            wq, wk, wv = (attn_w_qkv[j][:, :q_dim], attn_w_qkv[j][:, q_dim:q_dim + kv_dim],
                          attn_w_qkv[j][:, q_dim + kv_dim:])
            bq, bk, bv = (attn_b_qkv[j][:q_dim], attn_b_qkv[j][q_dim:q_dim + kv_dim],
                          attn_b_qkv[j][q_dim + kv_dim:])
            w = jnp.concatenate([wq, _doubled(wk, N_KV_HEADS), _doubled(wv, N_KV_HEADS)], axis=1).astype(BF16)
            b = jnp.concatenate([bq, _doubled(bk, N_KV_HEADS), _doubled(bv, N_KV_HEADS)])
            qkv = _norm_matmul(xs, norm_mix[i], mod[:, 0:2], w, b, n_rows=n_in, out_dtype=BF16,
                               rope=(cos, sin, q_dim + 2 * kv_dim, q_dim), **dims)
            o = _attention(qkv, attn_sinks[j], n_ctx=C, ctx_out=update_ctx, **dims)
            xs = _matmul_residual(o, attn_w_o[j].astype(BF16), attn_b_o[j], xs, mod[:, 2:3], n_rows=n_out, **dims)
        else:
            uc = _inproj(xs, norm_mix[i], mod[:, 0:2], hy_w_in[j].astype(BF16), hy_b_in[j], hy_conv_w[j],
                         hy_conv_b[j], n_rows=n_in, n_ctx=C, **dims)
            filt = (hy_f_w1[j], hy_f_b1[j], hy_f_w2[j], hy_f_b2[j], hy_f_w3[j], hy_f_b3[j], hy_f_freq[j],
                    hy_f_wout[j])
            z = _hyena_mixer(uc, hy_bias_d[j], filt, n_ctx=C, with_ctx=update_ctx, **dims)
            xs = _matmul_residual(z, hy_w_o[j].astype(BF16), hy_b_o[j], xs, mod[:, 2:3], n_rows=n_out, **dims)
        xs = _moe_layer(xs, i, mod, norm_ffn[i], moe_router_w[i], moe_router_b[i], moe_w_gu, moe_b_gu,
                        moe_w_down, moe_b_down, final_norm, n_rows=n_out, final=(i == DEPTH - 1), **dims)
    return xs.reshape(B, N, D)
```

```python
import functools
import math

import jax
import jax.numpy as jnp
import numpy as np
from jax import lax
from jax.experimental import pallas as pl
from jax.experimental.pallas import tpu as pltpu

DEPTH = 4
N_MIXERS = 2
GRID_W = 64
N_HEADS = 16
N_KV_HEADS = 4
HEAD_DIM = 64
GROUP = N_HEADS // N_KV_HEADS
WINDOW = 128
ROPE_THETA = 10000.0
HY_ORDER = 2
HY_SHORT = 3
HY_BANDS = 16
HY_TARGET = 1e-2
HY_FAST_DECAY = 0.3
HY_SLOW_DECAY = 1.5
N_EXPERTS = 32
TOP_K = 4
SWIGLU_LIMIT = 7.0
SWIGLU_ALPHA = 1.702
NORM_EPS = 1e-6

LANES = 128
OCT = 8
MINOR = 128
TOKEN_TILE = 512
Q_TILE = 128
KV_TOGETHER = 2
MOE_TILE = 512
VMEM_LIMIT = 56 * 1024 * 1024

F32 = jnp.float32
BF16 = jnp.bfloat16


def _dot(a, b):
    return jnp.dot(a, b, preferred_element_type=F32)


def _params(*sem):
    return pltpu.CompilerParams(dimension_semantics=sem, vmem_limit_bytes=VMEM_LIMIT)


def _ada_kernel(c_ref, w_ref, b_ref, o_ref):
    c = c_ref[...]
    s = c * (1.0 / (1.0 + jnp.exp(-c)))
    o_ref[...] = jnp.dot(s, w_ref[...], preferred_element_type=F32,
                         precision=lax.Precision.HIGHEST) + b_ref[...]


def _ada_table(c_all, ada_w, ada_b):
    r, d = c_all.shape
    depth, _, n6 = ada_w.shape
    tn = n6 // 4
    return pl.pallas_call(
        _ada_kernel,
        out_shape=jax.ShapeDtypeStruct((depth, r, n6), F32),
        grid=(depth, n6 // tn),
        in_specs=[pl.BlockSpec((r, d), lambda l, j: (0, 0)),
                  pl.BlockSpec((None, d, tn), lambda l, j: (l, 0, j)),
                  pl.BlockSpec((None, 1, tn), lambda l, j: (l, 0, j))],
        out_specs=pl.BlockSpec((None, r, tn), lambda l, j: (l, 0, j)),
        compiler_params=_params("arbitrary", "arbitrary"),
        name="ada_table",
    )(c_all, ada_w, ada_b.reshape(depth, 1, n6))


def _modulated(x, g, mod_ref, row):
    y = x * lax.rsqrt(jnp.mean(x * x, axis=-1, keepdims=True) + NORM_EPS) * g
    return y * (1.0 + mod_ref[row + 1:row + 2, :]) + mod_ref[row:row + 1, :]


def _mod_index(tm, n_seq, n_batch):
    return lambda i: (jnp.minimum(i * tm // n_seq, n_batch), 0, 0)


def _nm_kernel(x_ref, g_ref, mod_ref, w_ref, b_ref, cos_ref, sin_ref, o_ref, *, tn, rope_cols, q_cols):
    h = _modulated(x_ref[...], g_ref[...], mod_ref, 0).astype(BF16)
    n_out = o_ref.shape[1]
    if rope_cols:
        reps = tn // LANES
        cos = jnp.tile(cos_ref[...], (1, reps))
        sin = jnp.tile(sin_ref[...], (1, reps))
        lane = lax.broadcasted_iota(jnp.int32, (1, tn), 1)
        first_half = (lane & (HEAD_DIM - 1)) < (HEAD_DIM // 2)
    for n0 in range(0, n_out, tn):
        acc = _dot(h, w_ref[:, n0:n0 + tn]) + b_ref[:, n0:n0 + tn]
        if n0 < rope_cols:
            partner = jnp.where(first_half, pltpu.roll(acc, tn - HEAD_DIM // 2, 1),
                                pltpu.roll(acc, HEAD_DIM // 2, 1))
            acc = acc * cos + partner * sin
            if n0 < q_cols:
                acc = acc * (HEAD_DIM ** -0.5)
        o_ref[:, n0:n0 + tn] = acc.astype(o_ref.dtype)


def _norm_matmul(xs, gain, mod, w, bias, *, n_rows, n_seq, n_batch, out_dtype, rope=None):
    d = xs.shape[1]
    n_out = w.shape[1]
    tm, tn = TOKEN_TILE, 512
    if rope is None:
        cos = jnp.zeros((tm, LANES), F32)
        sin = cos
        rope_cols = q_cols = 0
        rope_index = lambda i: (0, 0)
    else:
        cos, sin, rope_cols, q_cols = rope
        seq_tiles = n_seq // tm
        lat_tiles = n_seq * n_batch // tm
        rope_index = lambda i: (jnp.where(i < lat_tiles, i % seq_tiles, seq_tiles), 0)
    kern = functools.partial(_nm_kernel, tn=tn, rope_cols=rope_cols, q_cols=q_cols)
    return pl.pallas_call(
        kern,
        out_shape=jax.ShapeDtypeStruct((n_rows, n_out), out_dtype),
        grid=(n_rows // tm,),
        in_specs=[pl.BlockSpec((tm, d), lambda i: (i, 0)),
                  pl.BlockSpec((1, d), lambda i: (0, 0)),
                  pl.BlockSpec((None, 2, d), _mod_index(tm, n_seq, n_batch)),
                  pl.BlockSpec((d, n_out), lambda i: (0, 0)),
                  pl.BlockSpec((1, n_out), lambda i: (0, 0)),
                  pl.BlockSpec((tm, LANES), rope_index),
                  pl.BlockSpec((tm, LANES), rope_index)],
        out_specs=pl.BlockSpec((tm, n_out), lambda i: (i, 0)),
        compiler_params=_params("arbitrary"),
        name="norm_matmul",
    )(xs, gain.reshape(1, d), mod, w, bias.reshape(1, n_out), cos, sin)


def _attend_heads(q_ref, work, sink_ref, o_ref):
    tq = q_ref.shape[0]
    lane = lax.broadcasted_iota(jnp.int32, (tq, LANES), 1)
    low = lane < HEAD_DIM
    row = lax.broadcasted_iota(jnp.int32, (2 * tq, 1), 0)
    lane_tiles = lambda a: [a[:, t:t + LANES] for t in range(0, a.shape[1], LANES)]
    items = [(kvh * GROUP + 2 * j, segments) for kvh, segments in work for j in range(GROUP // 2)]
    sinks = [jnp.where(row < tq, sink_ref[head], sink_ref[head + 1]) for head, _ in items]
    scores, maxes = [], []
    for (head, segments), sink in zip(items, sinks):
        qp = q_ref[:, head * HEAD_DIM:head * HEAD_DIM + LANES]
        zero = jnp.zeros_like(qp)
        q2 = jnp.concatenate([jnp.where(low, qp, zero), jnp.where(low, zero, qp)], axis=0)
        sj, folded = [], None
        for k, _, mask in segments:
            s = lax.dot_general(q2, k, (((1,), (1,)), ((), ())), preferred_element_type=F32)
            if mask is not None:
                s = jnp.where(mask, s, -jnp.inf)
            sj.append(s)
            for t in lane_tiles(s):
                folded = t if folded is None else jnp.maximum(folded, t)
        scores.append(sj)
        maxes.append(jnp.maximum(sink, jnp.max(folded, axis=-1, keepdims=True)))
    for (head, segments), sink, sj, m in zip(items, sinks, scores, maxes):
        o2, folded = None, None
        for s, (_, v, _) in zip(sj, segments):
            p = jnp.exp(s - m)
            for t in lane_tiles(p):
                folded = t if folded is None else folded + t
            pv = _dot(p.astype(BF16), v)
            o2 = pv if o2 is None else o2 + pv
        o2 = o2 / (jnp.exp(sink - m) + jnp.sum(folded, axis=-1, keepdims=True))
        o_ref[:, head * HEAD_DIM:head * HEAD_DIM + LANES] = jnp.where(low, o2[:tq], o2[tq:]).astype(o_ref.dtype)


def _attn_kernel(sink_ref, q_ref, kp_ref, kc_ref, kn_ref, vp_ref, vc_ref, vn_ref, kx_ref, vx_ref, o_ref,
                 *, n_q_blocks):
    qi = pl.program_id(1)
    tq = q_ref.shape[0]
    head_cols = lambda kvh: slice(kvh * LANES, (kvh + 1) * LANES)

    @pl.when(qi < n_q_blocks)
    def _():
        r = lax.broadcasted_iota(jnp.int32, (2 * tq, tq), 0) & (tq - 1)
        c = lax.broadcasted_iota(jnp.int32, (2 * tq, tq), 1)
        prev_ok = (c >= r) & (qi >= 1)
        next_ok = (c <= r) & (qi <= n_q_blocks - 2)
        for kv0 in range(0, N_KV_HEADS, KV_TOGETHER):
            work = []
            for kvh in range(kv0, kv0 + KV_TOGETHER):
                cs = head_cols(kvh)
                work.append((kvh, [(kc_ref[:, cs], vc_ref[:, cs], None), (kx_ref[:, cs], vx_ref[:, cs], None),
                                   (kp_ref[:, cs], vp_ref[:, cs], prev_ok), (kn_ref[:, cs], vn_ref[:, cs], next_ok)]))
            _attend_heads(q_ref, work, sink_ref, o_ref)

    @pl.when(qi >= n_q_blocks)
    def _():
        for kv0 in range(0, N_KV_HEADS, KV_TOGETHER):
            work = [(kvh, [(kx_ref[:, head_cols(kvh)], vx_ref[:, head_cols(kvh)], None)])
                    for kvh in range(kv0, kv0 + KV_TOGETHER)]
            _attend_heads(q_ref, work, sink_ref, o_ref)


def _attention(qkv, sinks, *, n_seq, n_batch, n_ctx, ctx_out):
    tq = Q_TILE
    assert tq == WINDOW
    nq = n_seq // tq
    ncq = n_ctx // tq if ctx_out else 0
    lat_blocks = n_batch * nq
    q_dim = N_HEADS * HEAD_DIM
    kv_w = N_KV_HEADS * LANES
    k_col = q_dim // kv_w
    v_col = k_col + 1
    ctx_row0 = n_batch * n_seq // n_ctx

    def q_index(b, i, s):
        return (jnp.where(i < nq, b * nq + i, lat_blocks + b * (n_ctx // tq) + (i - nq)), 0)

    def kv_index(off, col):
        def index(b, i, s):
            return (b * nq + jnp.clip(i + off, 0, nq - 1), col)
        return index

    n_rows = n_batch * n_seq + (n_batch * n_ctx if ctx_out else 0)
    grid_spec = pltpu.PrefetchScalarGridSpec(
        num_scalar_prefetch=1,
        grid=(n_batch, nq + ncq),
        in_specs=[pl.BlockSpec((tq, q_dim), q_index),
                  pl.BlockSpec((tq, kv_w), kv_index(-1, k_col)),
                  pl.BlockSpec((tq, kv_w), kv_index(0, k_col)),
                  pl.BlockSpec((tq, kv_w), kv_index(1, k_col)),
                  pl.BlockSpec((tq, kv_w), kv_index(-1, v_col)),
                  pl.BlockSpec((tq, kv_w), kv_index(0, v_col)),
                  pl.BlockSpec((tq, kv_w), kv_index(1, v_col)),
                  pl.BlockSpec((n_ctx, kv_w), lambda b, i, s: (ctx_row0 + b, k_col)),
                  pl.BlockSpec((n_ctx, kv_w), lambda b, i, s: (ctx_row0 + b, v_col))],
        out_specs=pl.BlockSpec((tq, q_dim), q_index),
    )
    return pl.pallas_call(
        functools.partial(_attn_kernel, n_q_blocks=nq),
        out_shape=jax.ShapeDtypeStruct((n_rows, q_dim), BF16),
        grid_spec=grid_spec,
        compiler_params=_params("arbitrary", "arbitrary"),
        name="attention",
    )(sinks, qkv, qkv, qkv, qkv, qkv, qkv, qkv, qkv, qkv)


def _pack_pairs(x):
    half = x.shape[1] // 2
    bits = lax.bitcast_convert_type(x.astype(BF16).astype(F32), jnp.int32)
    return lax.shift_right_logical(bits[:, :half], 16) | (bits[:, half:] & -65536)


def _unpack_pairs(u):
    return (lax.bitcast_convert_type(lax.shift_left(u, 16), F32),
            lax.bitcast_convert_type(u & -65536, F32))


def _mix_router_kernel(a_ref, wo_ref, bo_ref, x_ref, g_ref, mod_ref, w_ref, b_ref, tri_ref,
                       xnew_ref, tok_ref, route_ref, count_ref, seen):
    @pl.when(pl.program_id(0) == 0)
    def _():
        seen[...] = jnp.zeros_like(seen)

    x = x_ref[...] + mod_ref[0:1, :] * (_dot(a_ref[...].astype(BF16), wo_ref[...]) + bo_ref[...])
    xnew_ref[...] = x
    h = _modulated(x, g_ref[...], mod_ref, 1)
    tok_ref[...] = _pack_pairs(h)
    h_hi = h.astype(BF16)
    h_lo = (h - h_hi.astype(F32)).astype(BF16)
    logits = (_dot(h_hi, w_ref[0]) + (_dot(h_hi, w_ref[1]) + _dot(h_lo, w_ref[0]))) + b_ref[...]
    lane = lax.broadcasted_iota(jnp.int32, logits.shape, 1)
    rest = jnp.where(lane < N_EXPERTS, logits, -jnp.inf)
    chosen = jnp.zeros(logits.shape, F32)
    top_v, top_i = [], []
    for _ in range(TOP_K):
        best = jnp.max(rest, axis=-1, keepdims=True)
        idx = jnp.min(jnp.where(rest == best, lane, LANES), axis=-1, keepdims=True)
        hit = lane == idx
        top_v.append(best)
        top_i.append(idx)
        rest = jnp.where(hit, -jnp.inf, rest)
        chosen = jnp.where(hit, 1.0, chosen)
    weights = [jnp.exp(v - top_v[0]) for v in top_v]
    denom = (weights[0] + weights[1]) + (weights[2] + weights[3])
    earlier = _dot(tri_ref[...], chosen.astype(BF16)) + seen[...]
    seen[...] = seen[...] + jnp.sum(chosen, axis=0, keepdims=True)
    route = jnp.zeros(logits.shape, F32)
    for k in range(TOP_K):
        rank = jnp.sum(jnp.where(lane == top_i[k], earlier, 0.0), axis=-1, keepdims=True)
        route = jnp.where(lane == k, top_i[k].astype(F32), route)
        route = jnp.where(lane == TOP_K + k, weights[k] / denom, route)
        route = jnp.where(lane == 2 * TOP_K + k, rank, route)
    route_ref[...] = route
    count_ref[...] = jnp.broadcast_to(seen[...], count_ref.shape)


def _mix_router(a, w_o, b_o, xs, gain, mod, w, bias, *, n_rows, n_seq, n_batch):
    d = xs.shape[1]
    k = a.shape[1]
    tm = TOKEN_TILE
    tri = jnp.asarray(np.tril(np.ones((tm, tm)), -1), BF16)
    return pl.pallas_call(
        _mix_router_kernel,
        out_shape=(jax.ShapeDtypeStruct((n_rows, d), F32), jax.ShapeDtypeStruct((n_rows, d // 2), jnp.int32),
                   jax.ShapeDtypeStruct((n_rows, LANES), F32), jax.ShapeDtypeStruct((OCT, LANES), F32)),
        grid=(n_rows // tm,),
        in_specs=[pl.BlockSpec((tm, k), lambda i: (i, 0)),
                  pl.BlockSpec((k, d), lambda i: (0, 0)),
                  pl.BlockSpec((1, d), lambda i: (0, 0)),
                  pl.BlockSpec((tm, d), lambda i: (i, 0)),
                  pl.BlockSpec((1, d), lambda i: (0, 0)),
                  pl.BlockSpec((None, 3, d), _mod_index(tm, n_seq, n_batch)),
                  pl.BlockSpec((2, d, LANES), lambda i: (0, 0, 0)),
                  pl.BlockSpec((1, LANES), lambda i: (0, 0)),
                  pl.BlockSpec((tm, tm), lambda i: (0, 0))],
        out_specs=(pl.BlockSpec((tm, d), lambda i: (i, 0)), pl.BlockSpec((tm, d // 2), lambda i: (i, 0)),
                   pl.BlockSpec((tm, LANES), lambda i: (i, 0)), pl.BlockSpec((OCT, LANES), lambda i: (0, 0))),
        scratch_shapes=[pltpu.VMEM((1, LANES), F32)],
        compiler_params=_params("arbitrary"),
        name="mix_router",
    )(a, w_o, b_o.reshape(1, d), xs, gain.reshape(1, d), mod, w, bias, tri)


def _expert_kernel(be_ref, nb_ref, x_ref, wgu_ref, bgu_ref, wd_ref, bd_ref, o_ref, wgu_bf, wd_bf):
    i = pl.program_id(0)
    fresh = jnp.logical_or(i == 0, be_ref[i] != be_ref[jnp.maximum(i - 1, 0)])

    @pl.when(jnp.logical_and(fresh, i < nb_ref[0]))
    def _():
        wgu_bf[...] = wgu_ref[...].astype(BF16)
        wd_bf[...] = wd_ref[...].astype(BF16)

    @pl.when(i < nb_ref[0])
    def _():
        d_e = wd_ref.shape[0]
        x = jnp.concatenate(_unpack_pairs(x_ref[...]), axis=1).astype(BF16)
        g = jnp.minimum(_dot(x, wgu_bf[:, :d_e]) + bgu_ref[:, :d_e], SWIGLU_LIMIT)
        up = jnp.clip(_dot(x, wgu_bf[:, d_e:]) + bgu_ref[:, d_e:], -SWIGLU_LIMIT, SWIGLU_LIMIT)
        act = g * (1.0 / (1.0 + jnp.exp(-SWIGLU_ALPHA * g))) * (up + 1.0)
        o_ref[...] = _pack_pairs(_dot(act.astype(BF16), wd_bf[...]) + bd_ref[...])

    @pl.when(i >= nb_ref[0])
    def _():
        o_ref[...] = jnp.zeros_like(o_ref)


def _experts(xs, block_e, n_used, w_gu, b_gu, w_down, b_down, layer):
    r = xs.shape[0]
    n_e, d, d_gu = w_gu.shape[1:]
    tm = MOE_TILE
    grid_spec = pltpu.PrefetchScalarGridSpec(
        num_scalar_prefetch=2,
        grid=(r // tm,),
        in_specs=[pl.BlockSpec((tm, d // 2), lambda i, be, nb: (i, 0)),
                  pl.BlockSpec((None, None, d, d_gu), lambda i, be, nb: (layer, be[i], 0, 0)),
                  pl.BlockSpec((None, None, 1, d_gu), lambda i, be, nb: (layer, be[i], 0, 0)),
                  pl.BlockSpec((None, None, d_gu // 2, d), lambda i, be, nb: (layer, be[i], 0, 0)),
                  pl.BlockSpec((None, None, 1, d), lambda i, be, nb: (layer, be[i], 0, 0))],
        out_specs=pl.BlockSpec((tm, d // 2), lambda i, be, nb: (i, 0)),
        scratch_shapes=[pltpu.VMEM((d, d_gu), BF16), pltpu.VMEM((d_gu // 2, d), BF16)],
    )
    return pl.pallas_call(
        _expert_kernel,
        out_shape=jax.ShapeDtypeStruct((r, d // 2), jnp.int32),
        grid_spec=grid_spec,
        compiler_params=_params("arbitrary"),
        name="experts",
    )(block_e, n_used, xs, w_gu, b_gu.reshape(b_gu.shape[0], n_e, 1, d_gu), w_down,
      b_down.reshape(b_down.shape[0], n_e, 1, d))


def _combine_kernel(x_ref, y0_ref, y1_ref, y2_ref, y3_ref, route_ref, gate_ref, g_ref, o_ref, *, final):
    w = [route_ref[:, TOP_K + k:TOP_K + k + 1] for k in range(TOP_K)]
    y = [_unpack_pairs(r[...]) for r in (y0_ref, y1_ref, y2_ref, y3_ref)]
    f = jnp.concatenate([(w[0] * y[0][h] + w[1] * y[1][h]) + (w[2] * y[2][h] + w[3] * y[3][h]) for h in range(2)],
                        axis=1)
    x = x_ref[...] + gate_ref[...] * f
    if final:
        x = x * lax.rsqrt(jnp.mean(x * x, axis=-1, keepdims=True) + NORM_EPS) * g_ref[...]
    o_ref[...] = x


def _combine(xs, ys, route, gate, final_gain, *, n_rows, n_seq, n_batch, final):
    d = xs.shape[1]
    tm = TOKEN_TILE
    row_spec = pl.BlockSpec((tm, d), lambda i: (i, 0))
    return pl.pallas_call(
        functools.partial(_combine_kernel, final=final),
        out_shape=jax.ShapeDtypeStruct((n_rows, d), F32),
        grid=(n_rows // tm,),
        in_specs=[row_spec] + [pl.BlockSpec((tm, d // 2), lambda i: (i, 0))] * TOP_K
                 + [pl.BlockSpec((tm, LANES), lambda i: (i, 0)),
                                             pl.BlockSpec((None, 1, d), _mod_index(tm, n_seq, n_batch)),
                                             pl.BlockSpec((1, d), lambda i: (0, 0))],
        out_specs=row_spec,
        compiler_params=_params("arbitrary"),
        name="combine",
    )(xs, *ys, route, gate, final_gain.reshape(1, d))


def _mix_out_and_moe(a, w_o, b_o, xs, layer, mod, norm_g, router_w, router_b, w_gu, b_gu, w_down, b_down, final_gain,
                     *, n_rows, n_seq, n_batch, final):
    d = xs.shape[1]
    rw = jnp.zeros((d, LANES), F32).at[:, :N_EXPERTS].set(router_w)
    rw_hi = rw.astype(BF16)
    rw = jnp.stack([rw_hi, (rw - rw_hi.astype(F32)).astype(BF16)])
    rb = jnp.zeros((1, LANES), F32).at[0, :N_EXPERTS].set(router_b)
    xs, tok, route, count = _mix_router(a, w_o, b_o, xs, norm_g, mod[:, 2:5], rw, rb, n_rows=n_rows, n_seq=n_seq,
                                        n_batch=n_batch)
    top_i = route[:, :TOP_K].astype(jnp.int32)
    rank = route[:, 2 * TOP_K:3 * TOP_K].astype(jnp.int32)
    counts = count[0, :N_EXPERTS].astype(jnp.int32)
    n_assign = n_rows * TOP_K
    padded = (counts + MOE_TILE - 1) // MOE_TILE * MOE_TILE
    pad_end = jnp.cumsum(padded)
    pad_start = pad_end - padded
    start = jnp.cumsum(counts) - counts
    n_blocks = -(-n_assign // MOE_TILE) + N_EXPERTS
    n_slots = n_blocks * MOE_TILE
    block_e = jnp.minimum(jnp.sum(pad_end[None, :] <= (jnp.arange(n_blocks) * MOE_TILE)[:, None], axis=1),
                          N_EXPERTS - 1).astype(jnp.int32)
    n_used = (pad_end[-1] // MOE_TILE).astype(jnp.int32).reshape(1)
    slot_of = pad_start[top_i] + rank
    assert n_assign < 1 << 20
    _, by_rank = lax.sort_key_val((top_i * (1 << 20) + rank).reshape(-1), jnp.arange(n_assign, dtype=jnp.int32),
                                  is_stable=False)
    block_shift = (start - pad_start)[block_e]
    src = by_rank[jnp.clip(jnp.arange(n_slots, dtype=jnp.int32) + jnp.repeat(block_shift, MOE_TILE),
                           0, n_assign - 1)]
    ys = _experts(tok[src // TOP_K], block_e, n_used, w_gu, b_gu, w_down, b_down, layer)
    parts = [ys[slot_of[:, k]] for k in range(TOP_K)]
    return _combine(xs, parts, route, mod[:, 5:6], final_gain, n_rows=n_rows, n_seq=n_seq, n_batch=n_batch,
                    final=final)


def _hyena_spectrum(L, consts, w1, b1, w2, b2, w3, b3, freq, w_out, d):
    t = jnp.linspace(0.0, 1.0, L, dtype=F32)[:, None]
    w = 2.0 * math.pi * jnp.arange(L, dtype=F32)[:, None] / L
    bands = jnp.linspace(1e-4, HY_BANDS - 1, HY_BANDS, dtype=F32)[None, :]
    z = jnp.concatenate([t, jnp.cos(bands * w), -jnp.sin(bands * w)], axis=-1)
    a = jnp.sin(freq * (z @ w1 + b1))
    a = jnp.sin(freq * (a @ w2 + b2))
    a = jnp.sin(freq * (a @ w3 + b3))
    deltas = jnp.linspace(math.log(HY_TARGET) / HY_SLOW_DECAY, math.log(HY_TARGET) / HY_FAST_DECAY, d, dtype=F32)
    return _filter_spectrum(a, w_out, deltas, consts)


def _filter_kernel(feat_ref, wf_ref, wb_ref, delta_ref, kgf_ref, fm_ref, twc_ref, tws_ref, spec_ref, a_scr):
    n_k = kgf_ref.shape[0] // (2 * OCT)
    n_a = n_k - 1
    n_bo = MINOR // OCT
    dt = wf_ref.shape[1]
    kc = _major_chunk(n_k)
    cr_rows = kc * OCT
    w2 = jnp.concatenate([wf_ref[...], wb_ref[...]], axis=1).astype(BF16)
    decay = jnp.abs(jnp.concatenate([delta_ref[...], delta_ref[...]], axis=1)) * (1.0 / (n_a * MINOR - 1))
    r = lax.broadcasted_iota(jnp.int32, (n_a * OCT, 1), 0)
    lag0 = (lax.shift_right_logical(r, 3) * MINOR + (r & (OCT - 1))).astype(F32)
    total = jnp.zeros((1, 2 * dt), F32)
    b0 = None
    for bo in range(n_bo):
        feat = feat_ref[pl.ds(bo, n_a, stride=n_bo)].reshape(n_a * OCT, feat_ref.shape[2]).astype(BF16)
        x2 = _dot(feat, w2) * jnp.exp(-(lag0 + float(OCT * bo)) * decay)
        if bo == 0:
            b0 = x2[0:1, dt:]
        total = total + jnp.sum(jnp.abs(x2), axis=0, keepdims=True)
        p = _dot(kgf_ref[...], x2.astype(BF16))
        for ch in range(n_k // kc):
            r0 = 2 * ch * cr_rows
            ks = slice(ch * kc, (ch + 1) * kc)
            tw_rows = slice(ch * cr_rows, (ch + 1) * cr_rows)
            ar, ai = p[r0:r0 + cr_rows], p[r0 + cr_rows:r0 + 2 * cr_rows]
            c, s = twc_ref[tw_rows, bo:bo + 1], tws_ref[tw_rows, bo:bo + 1]
            a_scr[ks, 0, bo] = (ar * c + ai * s).reshape(kc, OCT, 2 * dt)
            a_scr[ks, 1, bo] = (ai * c - ar * s).reshape(kc, OCT, 2 * dt)
    inv = 1.0 / (total[:, :dt] + total[:, dt:] - jnp.abs(b0))
    unroll = max(u for u in range(1, 12) if n_k % u == 0)

    def minor_stage(step, carry):
        for j in range(unroll):
            k = step * unroll + j
            x = _dot(fm_ref[...], a_scr[k].reshape(2 * MINOR, 2 * dt).astype(BF16))
            spec_ref[k, 0] = ((x[:MINOR, :dt] + x[:MINOR, dt:] - b0) * inv).astype(spec_ref.dtype)
            spec_ref[k, 1] = ((x[MINOR:, :dt] - x[MINOR:, dt:]) * inv).astype(spec_ref.dtype)
        return carry

    lax.fori_loop(0, n_k // unroll, minor_stage, 0)
    for k in range(n_k, spec_ref.shape[0]):
        spec_ref[k] = jnp.zeros(spec_ref.shape[1:], spec_ref.dtype)


def _filter_spectrum(feat, w_out, deltas, consts):
    kgf, _, fm, _, twc, tws = consts
    dt = LANES
    d = deltas.shape[0]
    n_feat = feat.shape[1]
    seq_blk = feat.shape[0] // OCT
    n_k = kgf.shape[0] // (2 * OCT)
    n_kp = n_k + n_k % 2
    tiles = d // dt
    const_spec = lambda arr: pl.BlockSpec(arr.shape, lambda o, j: (0, 0))
    return pl.pallas_call(
        _filter_kernel,
        out_shape=jax.ShapeDtypeStruct((HY_ORDER, n_kp, 2, MINOR, d), BF16),
        grid=(HY_ORDER, tiles),
        in_specs=[pl.BlockSpec((seq_blk, OCT, n_feat), lambda o, j: (0, 0, 0)),
                  pl.BlockSpec((n_feat, dt), lambda o, j: (0, o * tiles + j)),
                  pl.BlockSpec((n_feat, dt), lambda o, j: (0, (HY_ORDER + o) * tiles + j)),
                  pl.BlockSpec((1, dt), lambda o, j: (0, j)),
                  const_spec(kgf), const_spec(fm), const_spec(twc), const_spec(tws)],
        out_specs=pl.BlockSpec((None, n_kp, 2, MINOR, dt), lambda o, j: (o, 0, 0, 0, j)),
        scratch_shapes=[pltpu.VMEM((n_k, 2, MINOR // OCT, OCT, 2 * dt), F32)],
        compiler_params=_params("arbitrary", "arbitrary"),
        name="hyena_filter",
    )(feat.reshape(seq_blk, OCT, n_feat), w_out, w_out, deltas.reshape(1, d), kgf, fm, twc, tws)


def _major_chunk(n_k):
    return max(c for c in range(1, 14) if n_k % c == 0)


def _dft_constants(L):
    n = 2 * L
    n1 = n // MINOR
    n_a, n_k = n1 // 2, n1 // 2 + 1
    k = np.arange(n_k)[:, None]
    a = np.arange(n_a)[None, :]
    th = 2.0 * np.pi * k * a / n1
    eye = np.eye(OCT)
    kgf = np.concatenate([np.kron(np.cos(th), eye), np.kron(-np.sin(th), eye)], axis=0)
    w = np.where((k == 0) | (k == n_a), 1.0, 2.0) / n
    kgi = np.concatenate([np.kron((np.cos(th) * w).T, eye), np.kron((-np.sin(th) * w).T, eye)], axis=1)
    b = np.arange(MINOR)
    ph = 2.0 * np.pi * np.outer(b, b) / MINOR
    fr, fi = np.cos(ph), -np.sin(ph)
    fm = np.block([[fr, -fi], [fi, fr]])
    fmi = np.block([[fr, fi], [-fi, fr]])
    bb = OCT * np.arange(MINOR // OCT)[None, None, :] + np.arange(OCT)[None, :, None]
    tw = 2.0 * np.pi * np.arange(n_k)[:, None, None] * bb / n
    pad = ((0, 0), (0, LANES - MINOR // OCT))
    twc = np.pad(np.cos(tw).reshape(n_k * OCT, -1), pad)
    tws = np.pad(np.sin(tw).reshape(n_k * OCT, -1), pad)
    kc = _major_chunk(n_k)
    kgf = kgf.reshape(2, n_k // kc, kc * OCT, -1).transpose(1, 0, 2, 3).reshape(2 * n_k * OCT, -1)
    kgi = kgi.reshape(-1, 2, n_k // kc, kc * OCT).transpose(0, 2, 1, 3).reshape(-1, 2 * n_k * OCT)
    first_im = kc * OCT
    last_im = 2 * n_k * OCT - OCT
    kgi = np.delete(kgi, np.r_[first_im:first_im + OCT, last_im:last_im + OCT], axis=1)
    return (jnp.asarray(kgf, BF16), jnp.asarray(kgi, BF16), jnp.asarray(fm, BF16), jnp.asarray(fmi, BF16),
            jnp.asarray(twc, F32), jnp.asarray(tws, F32))


def _hyena_kernel(z_ref, gate_ref, spec_ref, d_ref, kgf_ref, kgi_ref, fm_ref, fmi_ref, twc_ref, tws_ref, *rest):
    o_ref, a_scr = rest[-2], rest[-1]
    n_k = kgf_ref.shape[0] // (2 * OCT)
    n_a = n_k - 1
    n_bo = MINOR // OCT
    dt = z_ref.shape[2]
    kc = _major_chunk(n_k)
    cr_rows = kc * OCT
    side_by_side = lambda u, v: jnp.concatenate([u, v], axis=1)

    if a_scr.shape[0] > n_k:
        a_scr[n_k] = jnp.zeros(a_scr.shape[1:], F32)

    for bo in range(0, n_bo, 2):
        x2 = side_by_side(*[z_ref[pl.ds(bo + h, n_a, stride=n_bo)].reshape(n_a * OCT, dt) for h in range(2)])
        p = _dot(kgf_ref[...], x2.astype(BF16))
        for ch in range(n_k // kc):
            r0 = 2 * ch * cr_rows
            ks = slice(ch * kc, (ch + 1) * kc)
            tw_rows = slice(ch * cr_rows, (ch + 1) * cr_rows)
            for h in range(2):
                cols = slice(h * dt, (h + 1) * dt)
                ar, ai = p[r0:r0 + cr_rows, cols], p[r0 + cr_rows:r0 + 2 * cr_rows, cols]
                c, s = twc_ref[tw_rows, bo + h:bo + h + 1], tws_ref[tw_rows, bo + h:bo + h + 1]
                a_scr[ks, 0, bo + h] = (ar * c + ai * s).reshape(kc, OCT, dt)
                a_scr[ks, 1, bo + h] = (ai * c - ar * s).reshape(kc, OCT, dt)

    n_pairs = a_scr.shape[0] // 2
    unroll = max(u for u in range(1, 12) if n_pairs % u == 0)

    def minor_stage(step, carry):
        for j in range(unroll):
            k = 2 * (step * unroll + j)
            b2 = side_by_side(a_scr[k].reshape(2 * MINOR, dt), a_scr[k + 1].reshape(2 * MINOR, dt))
            x = _dot(fm_ref[...], b2.astype(BF16))
            xr, xi = x[:MINOR], x[MINOR:]
            gr = side_by_side(spec_ref[k, 0], spec_ref[k + 1, 0]).astype(F32)
            gi = side_by_side(spec_ref[k, 1], spec_ref[k + 1, 1]).astype(F32)
            y = jnp.concatenate([xr * gr - xi * gi, xr * gi + xi * gr], axis=0).astype(BF16)
            c2 = _dot(fmi_ref[...], y)
            a_scr[k] = c2[:, :dt].reshape(2, n_bo, OCT, dt)
            a_scr[k + 1] = c2[:, dt:].reshape(2, n_bo, OCT, dt)
        return carry

    lax.fori_loop(0, n_pairs // unroll, minor_stage, 0)

    for bo in range(0, n_bo, 2):
        halves = []
        for h in range(2):
            pieces = []
            for ch in range(n_k // kc):
                ks = slice(ch * kc, (ch + 1) * kc)
                tw_rows = slice(ch * cr_rows, (ch + 1) * cr_rows)
                cr = a_scr[ks, 0, bo + h].reshape(cr_rows, dt)
                ci = a_scr[ks, 1, bo + h].reshape(cr_rows, dt)
                c, s = twc_ref[tw_rows, bo + h:bo + h + 1], tws_ref[tw_rows, bo + h:bo + h + 1]
                im = cr * s + ci * c
                im = im[OCT if ch == 0 else 0:cr_rows - OCT if ch == n_k // kc - 1 else cr_rows]
                pieces += [cr * c - ci * s, im]
            halves.append(jnp.concatenate(pieces, axis=0))
        conv = _dot(kgi_ref[...], side_by_side(*halves).astype(BF16))
        for h in range(2):
            rows = pl.ds(bo + h, n_a, stride=n_bo)
            zin = z_ref[rows].reshape(n_a * OCT, dt)
            out = gate_ref[rows].reshape(n_a * OCT, dt) * (conv[:, h * dt:(h + 1) * dt] + zin * d_ref[...])
            o_ref[rows] = out.reshape(n_a, OCT, dt)


def _hyena_conv(zin, zin_col0, gate, gate_col0, spec, bias_d, consts, *, seq_len, n_seqs, row0, out_rows, prev_out,
                order=0):
    d = spec.shape[-1]
    dt = LANES
    n_k = spec.shape[1]
    seq_blk = seq_len // OCT
    blk0 = row0 // seq_len
    as_oct = lambda arr: arr.reshape(arr.shape[0] // OCT, OCT, arr.shape[1])
    const_spec = lambda arr: pl.BlockSpec(arr.shape, lambda j, b: (0, 0))
    in_specs = [pl.BlockSpec((seq_blk, OCT, dt), lambda j, b: (blk0 + b, 0, zin_col0 // dt + j)),
                pl.BlockSpec((seq_blk, OCT, dt), lambda j, b: (blk0 + b, 0, gate_col0 // dt + j)),
                pl.BlockSpec((None, n_k, 2, MINOR, dt), lambda j, b: (order, 0, 0, 0, j)),
                pl.BlockSpec((None, 1, dt), lambda j, b: (order, 0, j))] + [const_spec(cst) for cst in consts]
    args = [as_oct(zin), as_oct(gate), spec, bias_d.reshape(bias_d.shape[0], 1, d), *consts]
    aliases = {}
    if prev_out is not None:
        in_specs.append(pl.BlockSpec(memory_space=pl.ANY))
        aliases = {len(args): 0}
        args.append(as_oct(prev_out))
    out = pl.pallas_call(
        _hyena_kernel,
        out_shape=jax.ShapeDtypeStruct((out_rows // OCT, OCT, d), F32),
        grid=(d // dt, n_seqs),
        in_specs=in_specs,
        out_specs=pl.BlockSpec((seq_blk, OCT, dt), lambda j, b: (blk0 + b, 0, j)),
        scratch_shapes=[pltpu.VMEM((n_k, 2, MINOR // OCT, OCT, dt), F32)],
        input_output_aliases=aliases,
        compiler_params=_params("arbitrary", "arbitrary"),
        name="hyena_conv",
    )(*args)
    return out.reshape(out_rows, d)


def _hyena_mixer(uc, bias_d, filt, *, n_seq, n_batch, n_ctx, with_ctx):
    d = uc.shape[1] // 3
    rows = uc.shape[0]
    groups = [(n_seq, n_batch, 0)] + ([(n_ctx, n_batch, n_seq * n_batch)] if with_ctx else [])
    consts = [_dft_constants(L) for L, _, _ in groups]
    specs = [_hyena_spectrum(L, cst, *filt, d) for (L, _, _), cst in zip(groups, consts)]
    z = None
    for o in range(HY_ORDER):
        zin, zin_col0 = (uc, 2 * d) if o == 0 else (z, 0)
        out = None
        for (L, n_seqs, row0), spec, cst in zip(groups, specs, consts):
            out = _hyena_conv(zin, zin_col0, uc, o * d, spec, bias_d, cst, seq_len=L, n_seqs=n_seqs, row0=row0,
                              out_rows=rows, prev_out=out, order=o)
        z = out
    return z


def _inproj_kernel(x_ref, xp_ref, xn_ref, g_ref, mod_ref, w_ref, b_ref, cw_ref, cb_ref, o_ref,
                   *, tn, n_seq, n_ctx, lat_tiles):
    i = pl.program_id(0)
    tm = x_ref.shape[0]
    x = jnp.concatenate([xp_ref[...], x_ref[...], xn_ref[...]], axis=0)
    h = _modulated(x, g_ref[...], mod_ref, 0).astype(BF16)
    row = lax.broadcasted_iota(jnp.int32, (tm, 1), 0)
    pos = jnp.where(i < lat_tiles, (i * tm) % n_seq + row, row & (n_ctx - 1))
    last = jnp.where(i < lat_tiles, n_seq - 1, n_ctx - 1)
    has_prev = pos != 0
    has_next = pos != last
    for n0 in range(0, o_ref.shape[1], tn):
        cols = slice(n0, n0 + tn)
        u = _dot(h, w_ref[:, cols]) + b_ref[:, cols]
        prev = jnp.where(has_prev, u[OCT - 1:OCT - 1 + tm], 0.0)
        nxt = jnp.where(has_next, u[OCT + 1:OCT + 1 + tm], 0.0)
        o_ref[:, cols] = (cb_ref[:, cols] + prev * cw_ref[0:1, cols] + u[OCT:OCT + tm] * cw_ref[1:2, cols]
                          + nxt * cw_ref[2:3, cols])


def _inproj(xs, gain, mod, w, bias, conv_w, conv_b, *, n_rows, n_seq, n_batch, n_ctx):
    d = xs.shape[1]
    n_out = w.shape[1]
    tm, tn = TOKEN_TILE, 512
    assert n_seq % tm == 0 and tm % n_ctx == 0 and n_ctx & (n_ctx - 1) == 0
    halo = tm // OCT
    last_halo = xs.shape[0] // OCT - 1
    kern = functools.partial(_inproj_kernel, tn=tn, n_seq=n_seq, n_ctx=n_ctx, lat_tiles=n_seq * n_batch // tm)
    return pl.pallas_call(
        kern,
        out_shape=jax.ShapeDtypeStruct((n_rows, n_out), F32),
        grid=(n_rows // tm,),
        in_specs=[pl.BlockSpec((tm, d), lambda i: (i, 0)),
                  pl.BlockSpec((OCT, d), lambda i: (jnp.maximum(i * halo - 1, 0), 0)),
                  pl.BlockSpec((OCT, d), lambda i: (jnp.minimum((i + 1) * halo, last_halo), 0)),
                  pl.BlockSpec((1, d), lambda i: (0, 0)),
                  pl.BlockSpec((None, 2, d), _mod_index(tm, n_seq, n_batch)),
                  pl.BlockSpec((d, n_out), lambda i: (0, 0)),
                  pl.BlockSpec((1, n_out), lambda i: (0, 0)),
                  pl.BlockSpec((HY_SHORT, n_out), lambda i: (0, 0)),
                  pl.BlockSpec((1, n_out), lambda i: (0, 0))],
        out_specs=pl.BlockSpec((tm, n_out), lambda i: (i, 0)),
        compiler_params=_params("arbitrary"),
        name="hyena_inproj",
    )(xs, xs, xs, gain.reshape(1, d), mod, w, bias.reshape(1, n_out), conv_w, conv_b.reshape(1, n_out))


def _rope_tables(n_seq, tm):
    rows = n_seq // GRID_W
    row = jnp.repeat(jnp.arange(rows), GRID_W).astype(F32)
    col = jnp.tile(jnp.arange(GRID_W), rows).astype(F32)
    half = HEAD_DIM // 2
    n_freq = half // 2
    inv = ROPE_THETA ** (-jnp.arange(n_freq, dtype=F32) / n_freq)
    ang = jnp.concatenate([row[:, None] * inv, col[:, None] * inv], axis=-1)
    cos = jnp.tile(jnp.cos(ang), (1, LANES // half))
    sin = jnp.tile(jnp.concatenate([-jnp.sin(ang), jnp.sin(ang)], axis=-1), (1, LANES // HEAD_DIM))
    cos = jnp.concatenate([cos, jnp.ones((tm, LANES), F32)], axis=0)
    sin = jnp.concatenate([sin, jnp.zeros((tm, LANES), F32)], axis=0)
    return cos, sin


def _doubled(w, n_heads):
    lead = w.shape[:-1]
    w = w.reshape(lead + (n_heads, 1, HEAD_DIM))
    return jnp.broadcast_to(w, lead + (n_heads, 2, HEAD_DIM)).reshape(lead + (n_heads * 2 * HEAD_DIM,))


def kernel(x, c, ctx, c_ctx, ada_w, ada_b, norm_mix, norm_ffn, attn_w_qkv, attn_b_qkv, attn_w_o, attn_b_o, attn_sinks, hy_w_in, hy_b_in, hy_conv_w, hy_conv_b, hy_f_w1, hy_f_b1, hy_f_w2, hy_f_b2, hy_f_w3, hy_f_b3, hy_f_freq, hy_f_wout, hy_bias_d, hy_w_o, hy_b_o, moe_router_w, moe_router_b, moe_w_gu, moe_b_gu, moe_w_down, moe_b_down, final_norm):
    B, N, D = x.shape
    C = ctx.shape[1]
    T, TC = B * N, B * C
    q_dim = N_HEADS * HEAD_DIM
    kv_dim = N_KV_HEADS * HEAD_DIM
    dims = dict(n_seq=N, n_batch=B)

    c_all = jnp.concatenate([c, c_ctx[None, :], jnp.zeros((16 - B - 1, D), F32)], axis=0)
    mod_all = _ada_table(c_all, ada_w, ada_b)[:, :B + 1].reshape(DEPTH, B + 1, 6, D)
    cos, sin = _rope_tables(N, TOKEN_TILE)

    xs = jnp.concatenate([x.reshape(T, D), ctx.reshape(TC, D)], axis=0)
    for i in range(DEPTH):
        kind, j = i % N_MIXERS, i // N_MIXERS
        update_ctx = any(l % N_MIXERS == 0 for l in range(i + 1, DEPTH))
        need_ctx = update_ctx or kind == 0
        mod = mod_all[i]
        n_in = T + TC if need_ctx else T
        n_out = T + TC if update_ctx else T
        if kind == 0:
            wq, wk, wv = (attn_w_qkv[j][:, :q_dim], attn_w_qkv[j][:, q_dim:q_dim + kv_dim],
                          attn_w_qkv[j][:, q_dim + kv_dim:])
            bq, bk, bv = (attn_b_qkv[j][:q_dim], attn_b_qkv[j][q_dim:q_dim + kv_dim],
                          attn_b_qkv[j][q_dim + kv_dim:])
            w = jnp.concatenate([wq, _doubled(wk, N_KV_HEADS), _doubled(wv, N_KV_HEADS)], axis=1).astype(BF16)
            b = jnp.concatenate([bq, _doubled(bk, N_KV_HEADS), _doubled(bv, N_KV_HEADS)])
            qkv = _norm_matmul(xs, norm_mix[i], mod[:, 0:2], w, b, n_rows=n_in, out_dtype=BF16,
                               rope=(cos, sin, q_dim + 2 * kv_dim, q_dim), **dims)
            mixed = _attention(qkv, attn_sinks[j], n_ctx=C, ctx_out=update_ctx, **dims)
            w_o, b_o = attn_w_o[j].astype(BF16), attn_b_o[j]
        else:
            uc = _inproj(xs, norm_mix[i], mod[:, 0:2], hy_w_in[j].astype(BF16), hy_b_in[j], hy_conv_w[j],
                         hy_conv_b[j], n_rows=n_in, n_ctx=C, **dims)
            filt = (hy_f_w1[j], hy_f_b1[j], hy_f_w2[j], hy_f_b2[j], hy_f_w3[j], hy_f_b3[j], hy_f_freq[j],
                    hy_f_wout[j])
            mixed = _hyena_mixer(uc, hy_bias_d[j], filt, n_ctx=C, with_ctx=update_ctx, **dims)
            w_o, b_o = hy_w_o[j].astype(BF16), hy_b_o[j]
        xs = _mix_out_and_moe(mixed, w_o, b_o, xs, i, mod, norm_ffn[i], moe_router_w[i], moe_router_b[i], moe_w_gu,
                              moe_b_gu, moe_w_down, moe_b_down, final_norm, n_rows=n_out, final=(i == DEPTH - 1),
                              **dims)
    return xs.reshape(B, N, D)
```
